```python
import math, functools
import jax, jax.numpy as jnp
from jax import lax
import numpy as np

D_MODEL = 1024
BATCH = 2
SEQ = 8192
DEPTH = 2
DEC_BATCH = 32
DEC_SEQ = 1
PAST_LEN = 16384
PAGE_SIZE = 128

N_MEM = 256
D_MIX = 2 * D_MODEL
HG_WIDTH = D_MIX // 2
HG_DK = 128
HG_HEADS = HG_WIDTH // HG_DK
HG_CHUNK = 64
DA_WIDTH = D_MIX // 4
DA_HEADS = 4
DA_DV = DA_WIDTH // DA_HEADS
DA_DQK = DA_DV // 2
Q_BLOCK = 128
MX_WIDTH = D_MIX // 4
MX_HEADS = 4
MX_DH = MX_WIDTH // MX_HEADS
N_BUCKETS = 32
MAX_DIST = 128
EPS = 1e-6
SPLITS = (HG_WIDTH, 2 * HG_WIDTH, 3 * HG_WIDTH, 4 * HG_WIDTH,
          4 * HG_WIDTH + DA_WIDTH, 4 * HG_WIDTH + 2 * DA_WIDTH,
          4 * HG_WIDTH + 3 * DA_WIDTH, 4 * HG_WIDTH + 4 * DA_WIDTH,
          4 * HG_WIDTH + 4 * DA_WIDTH + MX_WIDTH)
D_IN = 4 * HG_WIDTH + 4 * DA_WIDTH + 2 * MX_WIDTH

kernel_name = 'hybrid_hgrn2_diffattn_memory_decoder_step'


def rmsnorm(x, w):
    xf = x.astype(jnp.float32)
    y = xf * lax.rsqrt(jnp.mean(xf * xf, axis=-1, keepdims=True) + EPS)
    return (y * w.astype(jnp.float32)).astype(x.dtype)


def t5_bias(q_pos, k_pos, table):
    n = jnp.maximum(q_pos[:, None] - k_pos[None, :], 0)
    max_exact = N_BUCKETS // 2
    nf = jnp.maximum(n, 1).astype(jnp.float32)
    large = max_exact + (jnp.log(nf / max_exact) / math.log(MAX_DIST / max_exact)
                         * (N_BUCKETS - max_exact)).astype(jnp.int32)
    large = jnp.minimum(large, N_BUCKETS - 1)
    bucket = jnp.where(n < max_exact, n, large)
    return jnp.transpose(table[bucket].astype(jnp.float32), (2, 0, 1))


def diff_attn_core(q, k, v, q_pos, k_pos, lam, table):
    s = jnp.einsum('bqhcd,bkhcd->bhcqk', q, k).astype(jnp.float32) * (DA_DQK ** -0.5)
    bias = t5_bias(q_pos, k_pos, table)
    mask = k_pos[None, :] <= q_pos[:, None]
    s = jnp.where(mask, s + bias[None, :, None], -jnp.inf)
    p = jax.nn.softmax(s, axis=-1)
    w = p[:, :, 0] - lam * p[:, :, 1]
    return jnp.einsum('bhqk,bkhd->bqhd', w.astype(v.dtype), v)


def da_prompt(q, k, v, lam, table):
    B, T = q.shape[:2]
    nb = T // Q_BLOCK
    qb = q.reshape(B, nb, Q_BLOCK, DA_HEADS, 2, DA_DQK).transpose(1, 0, 2, 3, 4, 5)
    posb = jnp.arange(T).reshape(nb, Q_BLOCK)
    k_pos = jnp.arange(T)

    def block(args):
        qq, pp = args
        return diff_attn_core(qq, k, v, pp, k_pos, lam, table)

    o = lax.map(block, (qb, posb))
    return o.transpose(1, 0, 2, 3, 4).reshape(B, T, DA_HEADS, DA_DV)


def da_sample(q, k_new, v_new, lam, ck, cv, page_table, table):
    DB, TS = q.shape[:2]
    k_past = ck[page_table].reshape(DB, -1, DA_HEADS, 2, DA_DQK)
    v_past = cv[page_table].reshape(DB, -1, DA_HEADS, DA_DV)
    past = k_past.shape[1]
    k_all = jnp.concatenate([k_past, k_new.astype(k_past.dtype)], axis=1)
    v_all = jnp.concatenate([v_past, v_new.astype(v_past.dtype)], axis=1)
    q_pos = past + jnp.arange(TS)
    k_pos = jnp.arange(past + TS)
    return diff_attn_core(q, k_all, v_all, q_pos, k_pos, lam, table)


def hgrn2_chunked(q, k, log_f, v, s0):
    B, T, H, DK = q.shape
    DV = v.shape[-1]
    C = HG_CHUNK if T % HG_CHUNK == 0 else T
    n = T // C

    def to_chunks(a):
        return a.astype(jnp.float32).reshape(B, n, C, H, a.shape[-1]).transpose(1, 0, 3, 2, 4)

    causal = jnp.tril(jnp.ones((C, C), dtype=bool))

    def step(S, inp):
        qc, kc, gc, vc = inp
        cum = jnp.cumsum(gc, axis=2)
        diff = cum[:, :, :, None, :] - cum[:, :, None, :, :]
        decay = jnp.exp(jnp.where(causal[:, :, None], diff, -jnp.inf))
        scores = jnp.einsum('bhtk,bhsk,bhtsk->bhts', qc, kc, decay)
        o = (jnp.einsum('bhtk,bhkv->bhtv', qc * jnp.exp(cum), S)
             + jnp.einsum('bhts,bhsv->bhtv', scores, vc))
        last = cum[:, :, -1]
        S = (jnp.exp(last)[..., None] * S
             + jnp.einsum('bhsk,bhsv->bhkv', kc * jnp.exp(last[:, :, None] - cum), vc))
        return S, o

    S, o = lax.scan(step, s0.astype(jnp.float32),
                    (to_chunks(q), to_chunks(k), to_chunks(log_f), to_chunks(v)))
    o = o.transpose(1, 0, 3, 2, 4).reshape(B, T, H, DV)
    return o, S


def mem_attend(q, mk, mv):
    s = jnp.einsum('bqhd,bmhd->bhqm', q, mk.astype(q.dtype)).astype(jnp.float32) * (MX_DH ** -0.5)
    p = jax.nn.softmax(s, axis=-1)
    return jnp.einsum('bhqm,bmhd->bqhd', p.astype(q.dtype), mv.astype(q.dtype))


def mixer_sublayer(x, s0, mk, mv, da_fn, lb, lam, lam_init, norm_pre_l, w_in_l,
                   hg_norm_l, da_norm_l, w_out_l, norm_post_l):
    B, T, _ = x.shape
    xn = rmsnorm(x, norm_pre_l)
    hq, hf, hi, hg, dq, dk, dv, dg, mq, mg = jnp.split(xn @ w_in_l, SPLITS, axis=-1)
    q_h = jax.nn.silu(hq).reshape(B, T, HG_HEADS, HG_DK)
    f = lb + (1.0 - lb) * jax.nn.sigmoid(hf.astype(jnp.float32))
    o_h, s_new = hgrn2_chunked(q_h, (1.0 - f).reshape(B, T, HG_HEADS, HG_DK),
                               jnp.log(f).reshape(B, T, HG_HEADS, HG_DK),
                               hi.reshape(B, T, HG_HEADS, HG_DK), s0)
    o_h = rmsnorm(o_h.astype(x.dtype), hg_norm_l).reshape(B, T, HG_WIDTH) * jax.nn.silu(hg)
    k_d = dk.reshape(B, T, DA_HEADS, 2, DA_DQK)
    v_d = dv.reshape(B, T, DA_HEADS, DA_DV)
    o_d = da_fn(dq.reshape(B, T, DA_HEADS, 2, DA_DQK), k_d, v_d, lam)
    o_d = (rmsnorm(o_d, da_norm_l) * (1.0 - lam_init)).reshape(B, T, DA_WIDTH) * jax.nn.silu(dg)
    o_m = mem_attend(mq.reshape(B, T, MX_HEADS, MX_DH), mk, mv).reshape(B, T, MX_WIDTH) * jax.nn.silu(mg)
    y = jnp.concatenate([o_h, o_d, o_m], axis=-1) @ w_out_l
    x_new = x + rmsnorm(y, norm_post_l)
    return x_new, s_new, k_d.reshape(B, T, DA_HEADS, 2 * DA_DQK), v_d


def setup_inputs(seed: int = 0) -> dict:
    key = jax.random.key(seed)
    ks = jax.random.split(key, 24)
    f32 = jnp.float32
    n_pages = PAST_LEN // PAGE_SIZE
    n_used = DEC_BATCH * n_pages
    n_phys = n_used + max(1, n_used // 4)

    def nrm(k, shape, scale=1.0):
        return jax.random.normal(k, shape, f32) * scale

    return {
        'x_prompt': nrm(ks[0], (BATCH, SEQ, D_MODEL)),
        'x_sample': nrm(ks[1], (DEC_BATCH, DEC_SEQ, D_MODEL)),
        'mem_prompt': nrm(ks[2], (BATCH, N_MEM, D_MODEL)),
        'cache_da_k': nrm(ks[3], (DEPTH, n_phys, PAGE_SIZE, DA_HEADS, 2 * DA_DQK)),
        'cache_da_v': nrm(ks[4], (DEPTH, n_phys, PAGE_SIZE, DA_HEADS, DA_DV)),
        'cache_mem_k': nrm(ks[5], (DEPTH, DEC_BATCH, N_MEM, MX_HEADS, MX_DH)),
        'cache_mem_v': nrm(ks[6], (DEPTH, DEC_BATCH, N_MEM, MX_HEADS, MX_DH)),
        'state_hgrn': nrm(ks[7], (DEPTH, DEC_BATCH, HG_HEADS, HG_DK, HG_DK), 0.5),
        'page_table': jax.random.permutation(ks[8], n_phys)[:n_used].reshape(DEC_BATCH, n_pages).astype(jnp.int32),
        'w_in': nrm(ks[9], (DEPTH, D_MODEL, D_IN), D_MODEL ** -0.5),
        'w_out': nrm(ks[10], (DEPTH, D_MIX, D_MODEL), D_MIX ** -0.5),
        'w_mem_kv': nrm(ks[11], (DEPTH, D_MODEL, 2 * MX_WIDTH), D_MODEL ** -0.5),
        'norm_pre': 1.0 + nrm(ks[12], (DEPTH, D_MODEL), 0.02),
        'norm_post': 1.0 + nrm(ks[13], (DEPTH, D_MODEL), 0.02),
        'mem_norm': 1.0 + nrm(ks[14], (DEPTH, D_MODEL), 0.02),
        'hg_norm': 1.0 + nrm(ks[15], (DEPTH, HG_DK), 0.02),
        'da_norm': 1.0 + nrm(ks[16], (DEPTH, DA_DV), 0.02),
        'hg_lb': nrm(ks[17], (DEPTH, HG_WIDTH), 1.0),
        'da_lq1': nrm(ks[18], (DEPTH, DA_DQK), 0.1),
        'da_lk1': nrm(ks[19], (DEPTH, DA_DQK), 0.1),
        'da_lq2': nrm(ks[20], (DEPTH, DA_DQK), 0.1),
        'da_lk2': nrm(ks[21], (DEPTH, DA_DQK), 0.1),
        'rel_bias': nrm(ks[22], (N_BUCKETS, DA_HEADS), 0.5),
    }


def reference(x_prompt, x_sample, mem_prompt, cache_da_k, cache_da_v, cache_mem_k,
              cache_mem_v, state_hgrn, page_table, w_in, w_out, w_mem_kv, norm_pre,
              norm_post, mem_norm, hg_norm, da_norm, hg_lb, da_lq1, da_lk1, da_lq2,
              da_lk2, rel_bias):
    f32 = jnp.float32
    lb_all = jnp.cumsum(jax.nn.softmax(hg_lb.astype(f32), axis=0), axis=0)
    lb_all = lb_all - lb_all[0:1]
    xp, xs = x_prompt, x_sample
    B = xp.shape[0]
    n_mem = mem_prompt.shape[1]
    da_p = functools.partial(da_prompt, table=rel_bias)
    kp_l, vp_l, sp_l, mkp_l, mvp_l, ks_l, vs_l, ss_l = [], [], [], [], [], [], [], []
    for l in range(DEPTH):
        lam_init = 0.8 - 0.6 * math.exp(-0.3 * l)
        lam = (jnp.exp(jnp.sum(da_lq1[l].astype(f32) * da_lk1[l].astype(f32)))
               - jnp.exp(jnp.sum(da_lq2[l].astype(f32) * da_lk2[l].astype(f32))) + lam_init)
        common = (lb_all[l], lam, lam_init, norm_pre[l], w_in[l], hg_norm[l], da_norm[l],
                  w_out[l], norm_post[l])
        mk_p, mv_p = jnp.split(rmsnorm(mem_prompt, mem_norm[l]) @ w_mem_kv[l], 2, axis=-1)
        mk_p = mk_p.reshape(B, n_mem, MX_HEADS, MX_DH)
        mv_p = mv_p.reshape(B, n_mem, MX_HEADS, MX_DH)
        s0_p = jnp.zeros((B, HG_HEADS, HG_DK, HG_DK), f32)
        xp, s_p, k_p, v_p = mixer_sublayer(xp, s0_p, mk_p, mv_p, da_p, *common)
        da_s = functools.partial(da_sample, ck=cache_da_k[l], cv=cache_da_v[l],
                                 page_table=page_table, table=rel_bias)
        xs, s_s, k_s, v_s = mixer_sublayer(xs, state_hgrn[l], cache_mem_k[l], cache_mem_v[l],
                                           da_s, *common)
        kp_l.append(k_p); vp_l.append(v_p); sp_l.append(s_p.astype(x_prompt.dtype))
        mkp_l.append(mk_p); mvp_l.append(mv_p)
        ks_l.append(k_s); vs_l.append(v_s); ss_l.append(s_s.astype(state_hgrn.dtype))
    return (xp, xs, jnp.stack(kp_l), jnp.stack(vp_l), jnp.stack(sp_l), jnp.stack(mkp_l),
            jnp.stack(mvp_l), jnp.stack(ks_l), jnp.stack(vs_l), jnp.stack(ss_l))
```

```python
import functools
import math

import jax
import jax.numpy as jnp
from jax import lax
from jax.experimental import pallas as pl
from jax.experimental.pallas import tpu as pltpu

F32 = jnp.float32
BF16 = jnp.bfloat16

EPS = 1e-6
MAX_DIST = 128
HG_CHUNK = 64
HG_SUB = 16
DA_TQ = 256
DA_TK = 256
PAGES_PER_STEP = 8
NEG_BIG = -1e30
VMEM_LIMIT_BYTES = 56 * 1024 * 1024


def _cparams(sem):
    return pltpu.CompilerParams(dimension_semantics=sem, vmem_limit_bytes=VMEM_LIMIT_BYTES)


def _sigmoid(x):
    return 1.0 / (1.0 + jnp.exp(-x))


def _silu(x):
    return x * _sigmoid(x)


def _rms(x, w):
    return x * lax.rsqrt(jnp.mean(x * x, axis=-1, keepdims=True) + EPS) * w


def _dot(a, b):
    return jnp.dot(a, b, preferred_element_type=F32)


def _dot_nt(a, b):
    return lax.dot_general(a, b, (((1,), (1,)), ((), ())), preferred_element_type=F32)


def _dot_tn(a, b):
    return lax.dot_general(a, b, (((0,), (0,)), ((), ())), preferred_element_type=F32)


def _lam(lq1_ref, lk1_ref, lq2_ref, lk2_ref, lam_init):
    a = jnp.sum(lq1_ref[...] * lk1_ref[...], axis=-1, keepdims=True)
    b = jnp.sum(lq2_ref[...] * lk2_ref[...], axis=-1, keepdims=True)
    return jnp.exp(a) - jnp.exp(b) + lam_init


def _bias_kernel(tab_ref, out_ref, *, n_buckets, heads, tk, tq):
    kk = lax.broadcasted_iota(jnp.int32, (tk, tq), 0)
    qq = lax.broadcasted_iota(jnp.int32, (tk, tq), 1)
    max_exact = n_buckets // 2
    for jj in range(2):
        d = qq - kk + (tk if jj == 0 else 0)
        n = jnp.maximum(d, 0)
        nf = jnp.maximum(n, 1).astype(F32)
        large = max_exact + (jnp.log(nf / max_exact) / math.log(MAX_DIST / max_exact)
                             * (n_buckets - max_exact)).astype(jnp.int32)
        large = jnp.minimum(large, n_buckets - 1)
        bucket = jnp.where(n < max_exact, n, large)
        for h in range(heads):
            val = jnp.zeros((tk, tq), F32)
            for b in range(n_buckets):
                val = jnp.where(bucket == b, tab_ref[b, h], val)
            val = val - tab_ref[n_buckets - 1, h]
            out_ref[h, jj] = jnp.where(d < 0, NEG_BIG, val)


def _bias_tiles(rel_bias, heads):
    n_buckets = rel_bias.shape[0]
    return pl.pallas_call(
        functools.partial(_bias_kernel, n_buckets=n_buckets, heads=heads, tk=DA_TK, tq=DA_TQ),
        out_shape=jax.ShapeDtypeStruct((heads, 2, DA_TK, DA_TQ), F32),
        in_specs=[pl.BlockSpec(memory_space=pltpu.SMEM)],
        out_specs=pl.BlockSpec(memory_space=pltpu.VMEM),
        name="t5_bias",
    )(rel_bias)


def _memkv_kernel(mem_ref, nw_ref, w_ref, k_ref, v_ref, *, width):
    xn = _rms(mem_ref[0], nw_ref[0]).astype(BF16)
    y = _dot(xn, w_ref[0])
    k_ref[0, 0] = y[:, :width]
    v_ref[0, 0] = y[:, width:]


def _mem_kv(mem_prompt, mem_norm, w_mem_kv_bf):
    depth, d_model, two_w = w_mem_kv_bf.shape
    width = two_w // 2
    b, n_mem, _ = mem_prompt.shape
    out = jax.ShapeDtypeStruct((depth, b, n_mem, width), F32)
    return pl.pallas_call(
        functools.partial(_memkv_kernel, width=width),
        grid=(depth, b),
        out_shape=(out, out),
        in_specs=[
            pl.BlockSpec((1, n_mem, d_model), lambda l, i: (i, 0, 0)),
            pl.BlockSpec((1, 1, d_model), lambda l, i: (l, 0, 0)),
            pl.BlockSpec((1, d_model, two_w), lambda l, i: (l, 0, 0)),
        ],
        out_specs=(pl.BlockSpec((1, 1, n_mem, width), lambda l, i: (l, i, 0, 0)),
                   pl.BlockSpec((1, 1, n_mem, width), lambda l, i: (l, i, 0, 0))),
        compiler_params=_cparams(("arbitrary", "arbitrary")),
        name="mem_kv",
    )(mem_prompt, mem_norm.reshape(depth, 1, d_model), w_mem_kv_bf)


def _inproj_kernel(x_ref, nw_ref, w_ref, lbp_ref, *out_refs, layer, hgw, daw, mxw, da_heads,
                   tkb, emit_vt, col_chunk):
    if emit_vt:
        (qh_ref, kh_ref, vh_ref, lf_ref, gh_ref, dq_ref, dkf_ref, dkb_ref, dvf_ref, dvt_ref,
         dg_ref, mq_ref, mg_ref) = out_refs
    else:
        (qh_ref, kh_ref, vh_ref, lf_ref, gh_ref, dq_ref, dkf_ref, dkb_ref, dvf_ref,
         dg_ref, mq_ref, mg_ref) = out_refs
        dvt_ref = None
    xn = _rms(x_ref[0], nw_ref[0]).astype(BF16)

    p = lbp_ref[...]
    e = jnp.exp(p - jnp.max(p, axis=0, keepdims=True))
    sm = e / jnp.sum(e, axis=0, keepdims=True)
    lb = jnp.zeros((1, hgw), F32)
    for r in range(1, layer + 1):
        lb = lb + sm[r:r + 1, :]

    def seg(c0, width, fn):
        for c in range(0, width, col_chunk):
            w = min(col_chunk, width - c)
            fn(c, w, _dot(xn, w_ref[0, :, c0 + c:c0 + c + w]))

    def st(ref, fn):
        def go(c, w, y):
            ref[0, :, c:c + w] = fn(y).astype(ref.dtype)
        return go

    def forget(c, w, y):
        f = lb[:, c:c + w] + (1.0 - lb[:, c:c + w]) * _sigmoid(y)
        lf_ref[0, :, c:c + w] = jnp.log(f)
        kh_ref[0, :, c:c + w] = (1.0 - f).astype(BF16)

    def da_k(c, w, y):
        dkf_ref[0, :, c:c + w] = y
        dkb_ref[0, :, c:c + w] = y.astype(BF16)

    def da_v(c, w, y):
        dvf_ref[0, :, c:c + w] = y
        if dvt_ref is not None:
            dv = daw // da_heads
            tm = y.shape[0]
            for hh in range(c // dv, (c + w) // dv):
                for kb in range(tm // tkb):
                    blk = y[kb * tkb:(kb + 1) * tkb, hh * dv - c:(hh + 1) * dv - c]
                    dvt_ref[0, hh, kb] = blk.T.astype(BF16)

    o = 0
    seg(o, hgw, st(qh_ref, _silu)); o += hgw
    seg(o, hgw, forget); o += hgw
    seg(o, hgw, st(vh_ref, lambda y: y)); o += hgw
    seg(o, hgw, st(gh_ref, _silu)); o += hgw
    dqk = daw // da_heads // 2
    seg(o, daw, st(dq_ref, lambda y: y * (dqk ** -0.5))); o += daw
    seg(o, daw, da_k); o += daw
    seg(o, daw, da_v); o += daw
    seg(o, daw, st(dg_ref, _silu)); o += daw
    seg(o, mxw, st(mq_ref, lambda y: y)); o += mxw
    seg(o, mxw, st(mg_ref, _silu)); o += mxw


def _in_proj(x, layer, norm_pre, w_in_bf, hg_lb, *, hgw, daw, mxw, da_heads, tm, emit_vt):
    b, t, d_model = x.shape
    depth, _, d_in = w_in_bf.shape
    tkb = DA_TK
    dv = daw // da_heads

    def act(width, dtype):
        return (jax.ShapeDtypeStruct((b, t, width), dtype),
                pl.BlockSpec((1, tm, width), lambda i, j: (i, j, 0)))

    outs = [act(hgw, BF16), act(hgw, BF16), act(hgw, BF16), act(hgw, F32), act(hgw, BF16),
            act(daw, BF16), act(daw, F32), act(daw, BF16), act(daw, F32)]
    if emit_vt:
        outs.append((jax.ShapeDtypeStruct((b, da_heads, t // tkb, dv, tkb), BF16),
                     pl.BlockSpec((1, da_heads, tm // tkb, dv, tkb), lambda i, j: (i, 0, j, 0, 0))))
    outs += [act(daw, BF16), act(mxw, BF16), act(mxw, BF16)]
    kern = functools.partial(_inproj_kernel, layer=layer, hgw=hgw, daw=daw, mxw=mxw,
                             da_heads=da_heads, tkb=tkb, emit_vt=emit_vt, col_chunk=512)
    res = pl.pallas_call(
        kern,
        grid=(b, t // tm),
        out_shape=tuple(o[0] for o in outs),
        in_specs=[
            pl.BlockSpec((1, tm, d_model), lambda i, j: (i, j, 0)),
            pl.BlockSpec((1, 1, d_model), lambda i, j: (layer, 0, 0)),
            pl.BlockSpec((1, d_model, d_in), lambda i, j: (layer, 0, 0),
                         pipeline_mode=pl.Buffered(1)),
            pl.BlockSpec((depth, hgw), lambda i, j: (0, 0)),
        ],
        out_specs=tuple(o[1] for o in outs),
        compiler_params=_cparams(("arbitrary", "arbitrary")),
        name="in_proj_vt" if emit_vt else "in_proj",
    )(x, norm_pre.reshape(depth, 1, d_model), w_in_bf, hg_lb)
    names = ["qh", "kh", "vh", "lf", "gh", "dq", "dkf", "dkb", "dvf"]
    names += ["dvt"] if emit_vt else []
    names += ["dg", "mq", "mg"]
    return dict(zip(names, res))


def _hgrn_kernel(q_ref, k_ref, v_ref, g_ref, gate_ref, nw_ref, o_ref, s_ref, st_ref, *,
                 chunk, sub, n_chunks):
    j = pl.program_id(2)
    dk = q_ref.shape[-1]
    nsub = chunk // sub

    @pl.when(j == 0)
    def _():
        st_ref[...] = jnp.zeros_like(st_ref)

    r_i = lax.broadcasted_iota(jnp.int32, (chunk, chunk), 0)
    c_i = lax.broadcasted_iota(jnp.int32, (chunk, chunk), 1)
    tri = jnp.where(c_i <= r_i, 1.0, 0.0).astype(BF16)

    n_off_rows = chunk - sub
    n_off_cols = sub * (nsub * (nsub - 1) // 2)
    n_off_pad = -(-n_off_cols // 128) * 128
    ro = lax.broadcasted_iota(jnp.int32, (n_off_rows, n_off_pad), 0) // sub + 1
    co = lax.broadcasted_iota(jnp.int32, (n_off_rows, n_off_pad), 1)
    cblk = jnp.zeros_like(co)
    start = 0
    for i in range(1, nsub):
        cblk = jnp.where((co >= start) & (co < start + i * sub), i, cblk)
        start += i * sub
    off_mask = ro == cblk

    rd = lax.broadcasted_iota(jnp.int32, (nsub * sub * sub, chunk), 0)
    cd = lax.broadcasted_iota(jnp.int32, (nsub * sub * sub, chunk), 1)
    r_t = rd % sub
    r_is = rd // sub
    diag_mask = (r_is == cd) & (r_t >= r_is % sub)

    def body(ci, carry):
        r0 = pl.multiple_of(ci * chunk, chunk)
        q = q_ref[0, pl.ds(r0, chunk), :].astype(F32)
        k = k_ref[0, pl.ds(r0, chunk), :].astype(F32)
        v = v_ref[0, pl.ds(r0, chunk), :]
        g = g_ref[0, pl.ds(r0, chunk), :]

        g1 = g.astype(BF16)
        rem = g - g1.astype(F32)
        g2 = rem.astype(BF16)
        g3 = (rem - g2.astype(F32)).astype(BF16)
        cum = _dot(tri, g1) + _dot(tri, g2) + _dot(tri, g3)

        s_prev = st_ref[...]
        last = cum[chunk - 1:chunk, :]
        o = _dot_nt((q * jnp.exp(cum)).astype(BF16), s_prev.astype(BF16))
        kl = (k * jnp.exp(last - cum)).astype(BF16)
        st_ref[...] = s_prev * jnp.exp(last) + _dot_tn(v, kl)

        q_parts, k_parts, v_parts = [], [], []
        for i in range(1, nsub):
            b_i = cum[i * sub - 1:i * sub, :]
            lo, hi = i * sub, (i + 1) * sub
            q_parts.append((q[lo:hi] * jnp.exp(cum[lo:hi] - b_i)).astype(BF16))
            k_parts.append((k[:lo] * jnp.exp(b_i - cum[:lo])).astype(BF16))
            v_parts.append(v[:lo])
        if n_off_pad > n_off_cols:
            k_parts.append(jnp.zeros((n_off_pad - n_off_cols, dk), BF16))
            v_parts.append(jnp.zeros((n_off_pad - n_off_cols, v.shape[-1]), BF16))
        a_off = _dot_nt(jnp.concatenate(q_parts, axis=0), jnp.concatenate(k_parts, axis=0))
        a_off = jnp.where(off_mask, a_off, 0.0).astype(BF16)
        o_off = _dot(a_off, jnp.concatenate(v_parts, axis=0))

        n_parts = []
        for i in range(nsub):
            lo, hi = i * sub, (i + 1) * sub
            c_blk = cum[lo:hi]
            q_blk = q[lo:hi]
            for s in range(sub):
                dec = jnp.exp(jnp.minimum(c_blk - c_blk[s:s + 1, :], 0.0))
                n_parts.append((q_blk * dec).astype(BF16))
        r = _dot_nt(jnp.concatenate(n_parts, axis=0), k.astype(BF16))
        r = jnp.where(diag_mask, r, 0.0)
        a_parts = []
        for i in range(nsub):
            base = i * sub * sub
            acc = r[base:base + sub]
            for s in range(1, sub):
                acc = acc + r[base + s * sub:base + (s + 1) * sub]
            a_parts.append(acc)
        a_diag = jnp.concatenate(a_parts, axis=0).astype(BF16)
        o = o + _dot(a_diag, v)
        o = o + jnp.concatenate([jnp.zeros((sub, o.shape[-1]), F32), o_off], axis=0)

        gate = gate_ref[0, pl.ds(r0, chunk), :].astype(F32)
        o_ref[0, pl.ds(r0, chunk), :] = (_rms(o, nw_ref[...]) * gate).astype(o_ref.dtype)
        return carry

    lax.fori_loop(0, n_chunks, body, 0)

    @pl.when(j == pl.num_programs(2) - 1)
    def _():
        s_ref[0, 0] = st_ref[...].T


def _hgrn_prompt(a, hg_norm_l, *, heads, dk, tt):
    b, t, _ = a["qh"].shape
    spec = pl.BlockSpec((1, tt, dk), lambda i, h, j: (i, j, h))
    kern = functools.partial(_hgrn_kernel, chunk=HG_CHUNK, sub=HG_SUB, n_chunks=tt // HG_CHUNK)
    return pl.pallas_call(
        kern,
        grid=(b, heads, t // tt),
        out_shape=(jax.ShapeDtypeStruct((b, t, heads * dk), BF16),
                   jax.ShapeDtypeStruct((b, heads, dk, dk), F32)),
        in_specs=[spec, spec, spec, spec, spec, pl.BlockSpec((1, dk), lambda i, h, j: (0, 0))],
        out_specs=(spec, pl.BlockSpec((1, 1, dk, dk), lambda i, h, j: (i, h, 0, 0))),
        scratch_shapes=[pltpu.VMEM((dk, dk), F32)],
        compiler_params=_cparams(("arbitrary", "arbitrary", "arbitrary")),
        name="hgrn_prompt",
    )(a["qh"], a["kh"], a["vh"], a["lf"], a["gh"], hg_norm_l.reshape(1, dk))


def _hgrn_step_kernel(q_ref, lf_ref, v_ref, gate_ref, nw_ref, s_ref, o_ref, sn_ref, *, heads, dk):
    def col(row):
        return jnp.broadcast_to(row, (dk, dk)).T

    for h in range(heads):
        sl = slice(h * dk, (h + 1) * dk)
        f = jnp.exp(lf_ref[0, :, sl])
        s_new = col(f) * s_ref[0, h] + col(1.0 - f) * v_ref[0, :, sl].astype(F32)
        sn_ref[0, h] = s_new
        o = jnp.sum(col(q_ref[0, :, sl].astype(F32)) * s_new, axis=0, keepdims=True)
        o_ref[0, :, sl] = (_rms(o, nw_ref[...]) * gate_ref[0, :, sl].astype(F32)).astype(o_ref.dtype)


def _hgrn_step(a, hg_norm_l, state_hgrn, layer, *, heads, dk):
    db = a["qh"].shape[0]
    w = heads * dk
    row = pl.BlockSpec((1, 1, w), lambda i: (i, 0, 0))
    return pl.pallas_call(
        functools.partial(_hgrn_step_kernel, heads=heads, dk=dk),
        grid=(db,),
        out_shape=(jax.ShapeDtypeStruct((db, 1, w), BF16),
                   jax.ShapeDtypeStruct((db, heads, dk, dk), F32)),
        in_specs=[row, row, row, row, pl.BlockSpec((1, dk), lambda i: (0, 0)),
                  pl.BlockSpec((None, 1, heads, dk, dk), lambda i: (layer, i, 0, 0, 0))],
        out_specs=(row, pl.BlockSpec((1, heads, dk, dk), lambda i: (i, 0, 0, 0))),
        compiler_params=_cparams(("arbitrary",)),
        name="hgrn_step",
    )(a["qh"], a["lf"], a["vh"], a["gh"], hg_norm_l.reshape(1, dk), state_hgrn)


def _da_prompt_kernel(q_ref, k_ref, vt_ref, bias_ref, gate_ref, nw_ref, lq1_ref, lk1_ref,
                      lq2_ref, lk2_ref, o_ref, acc_ref, *, tq, tk, lam_init):
    i = pl.program_id(2)
    dv = q_ref.shape[-1]
    half = dv // 2

    qt = q_ref[0].astype(F32).T
    row = lax.broadcasted_iota(jnp.int32, (dv, tq), 0)
    qbd = jnp.concatenate([jnp.where(row < half, qt, 0.0), jnp.where(row >= half, qt, 0.0)],
                          axis=1).astype(BF16)

    acc_ref[...] = jnp.zeros_like(acc_ref)

    def block(jb, m, l, bias):
        k_blk = k_ref[0, pl.ds(pl.multiple_of(jb * tk, tk), tk), :]
        s = _dot(k_blk, qbd)
        if bias is not None:
            s = s + jnp.concatenate([bias, bias], axis=1)
        m_new = jnp.maximum(m, jnp.max(s, axis=0, keepdims=True))
        alpha = jnp.exp(m - m_new)
        p = jnp.exp(s - m_new)
        l_new = alpha * l + jnp.sum(p, axis=0, keepdims=True)
        acc_ref[...] = alpha * acc_ref[...] + _dot(vt_ref[0, 0, jb], p.astype(BF16))
        return m_new, l_new

    m0 = jnp.full((1, 2 * tq), 0.1 * NEG_BIG, F32)
    l0 = jnp.zeros((1, 2 * tq), F32)
    m, l = lax.fori_loop(0, jnp.maximum(i - 1, 0), lambda jb, c: block(jb, c[0], c[1], None),
                         (m0, l0))
    prev_bias = jnp.where(i > 0, bias_ref[0, 0], NEG_BIG)
    m, l = block(jnp.maximum(i - 1, 0), m, l, prev_bias)
    m, l = block(i, m, l, bias_ref[0, 1])

    lam = _lam(lq1_ref, lk1_ref, lq2_ref, lk2_ref, lam_init)
    acc = acc_ref[...]
    inv = 1.0 / l
    out_t = acc[:, :tq] * inv[:, :tq] - lam * (acc[:, tq:] * inv[:, tq:])
    out = out_t.T
    out = _rms(out, nw_ref[...]) * (1.0 - lam_init)
    o_ref[0] = (out * gate_ref[0].astype(F32)).astype(o_ref.dtype)


def _da_prompt(a, bias, da_norm_l, lam_rows, lam_init, *, heads, dv):
    b, t, _ = a["dq"].shape
    tq, tk = DA_TQ, DA_TK
    qspec = pl.BlockSpec((1, tq, dv), lambda i, h, j: (i, j, h))
    vec = pl.BlockSpec((1, dv // 2), lambda i, h, j: (0, 0))
    kern = functools.partial(_da_prompt_kernel, tq=tq, tk=tk, lam_init=lam_init)
    return pl.pallas_call(
        kern,
        grid=(b, heads, t // tq),
        out_shape=jax.ShapeDtypeStruct((b, t, heads * dv), BF16),
        in_specs=[
            qspec,
            pl.BlockSpec((1, t, dv), lambda i, h, j: (i, 0, h)),
            pl.BlockSpec((1, 1, t // tk, dv, tk), lambda i, h, j: (i, h, 0, 0, 0)),
            pl.BlockSpec((1, 2, tk, tq), lambda i, h, j: (h, 0, 0, 0)),
            qspec,
            pl.BlockSpec((1, dv), lambda i, h, j: (0, 0)),
            vec, vec, vec, vec,
        ],
        out_specs=qspec,
        scratch_shapes=[pltpu.VMEM((dv, 2 * tq), F32)],
        compiler_params=_cparams(("arbitrary", "arbitrary", "arbitrary")),
        name="da_prompt",
    )(a["dq"], a["dkb"], a["dvt"], bias, a["dg"], da_norm_l.reshape(1, dv), *lam_rows)


def _da_sample_kernel(pt_ref, q_ref, kn_ref, vn_ref, gate_ref, bias_ref, bias0_ref, nw_ref,
                      lq1_ref, lk1_ref, lq2_ref, lk2_ref, *refs, heads, dv, n_pages, lam_init):
    del pt_ref
    k_refs = refs[:n_pages]
    v_refs = refs[n_pages:2 * n_pages]
    o_ref, m_ref, l_ref, acc_ref = refs[2 * n_pages:]
    j = pl.program_id(1)
    half = dv // 2
    rows = 2 * heads
    width = heads * dv

    @pl.when(j == 0)
    def _():
        m_ref[...] = jnp.full_like(m_ref, NEG_BIG)
        l_ref[...] = jnp.zeros_like(l_ref)
        acc_ref[...] = jnp.zeros_like(acc_ref)

    q = q_ref[0].astype(F32)
    r_i = lax.broadcasted_iota(jnp.int32, (rows, width), 0)
    c_i = lax.broadcasted_iota(jnp.int32, (rows, width), 1)
    own = c_i // half == r_i
    qbd = jnp.where(own, jnp.broadcast_to(q, (rows, width)), 0.0)

    s = jnp.concatenate([_dot_nt(qbd.astype(BF16), k_refs[g][0, 0].astype(BF16))
                         for g in range(n_pages)], axis=1)
    s = s + bias_ref[...]
    m_old = m_ref[...]
    m_new = jnp.maximum(m_old, jnp.max(s, axis=-1, keepdims=True))
    alpha = jnp.exp(m_old - m_new)
    p = jnp.exp(s - m_new)
    l_ref[...] = alpha * l_ref[...] + jnp.sum(p, axis=-1, keepdims=True)
    m_ref[...] = m_new
    page = k_refs[0].shape[2]
    pv = jnp.zeros((rows, width), F32)
    for g in range(n_pages):
        pv = pv + _dot(p[:, g * page:(g + 1) * page].astype(BF16), v_refs[g][0, 0].astype(BF16))
    acc_ref[...] = alpha * acc_ref[...] + pv

    @pl.when(j == pl.num_programs(1) - 1)
    def _():
        kn = kn_ref[0].astype(BF16).astype(F32)
        s_n = jnp.sum(qbd.astype(BF16).astype(F32) * kn, axis=-1, keepdims=True) + bias0_ref[...]
        m_o = m_ref[...]
        m_f = jnp.maximum(m_o, s_n)
        al = jnp.exp(m_o - m_f)
        p_n = jnp.exp(s_n - m_f)
        l_f = al * l_ref[...] + p_n
        vn = vn_ref[0].astype(BF16).astype(F32)
        acc = al * acc_ref[...] + p_n.astype(BF16).astype(F32) * vn
        acc = jnp.where(c_i // dv == r_i // 2, acc, 0.0) / l_f
        lam = _lam(lq1_ref, lk1_ref, lq2_ref, lk2_ref, lam_init)
        sign = jnp.where(r_i % 2 == 0, 1.0, -lam)
        out = jnp.sum(acc * sign, axis=0, keepdims=True)
        gate = gate_ref[0].astype(F32)
        for h in range(heads):
            sl = slice(h * dv, (h + 1) * dv)
            o_h = _rms(out[:, sl], nw_ref[...]) * (1.0 - lam_init)
            o_ref[0, :, sl] = (o_h * gate[:, sl]).astype(o_ref.dtype)


def _da_sample(a, cache_k, cache_v, page_table, layer, bias_rows, bias0, da_norm_l, lam_rows,
               lam_init, *, heads, dv):
    db = a["dq"].shape[0]
    width = heads * dv
    n_pages_seq = page_table.shape[1]
    page = cache_k.shape[2]
    g_n = PAGES_PER_STEP
    steps = n_pages_seq // g_n
    rows = 2 * heads
    row = pl.BlockSpec((1, 1, width), lambda i, j, pt: (i, 0, 0))
    vec = pl.BlockSpec((1, dv // 2), lambda i, j, pt: (0, 0))

    def page_spec(g):
        return pl.BlockSpec((1, 1, page, width),
                            lambda i, j, pt, g=g: (layer, pt[i, j * g_n + g], 0, 0))

    kern = functools.partial(_da_sample_kernel, heads=heads, dv=dv, n_pages=g_n, lam_init=lam_init)
    grid_spec = pltpu.PrefetchScalarGridSpec(
        num_scalar_prefetch=1,
        grid=(db, steps),
        in_specs=[row, row, row, row,
                  pl.BlockSpec((rows, g_n * page), lambda i, j, pt: (0, j)),
                  pl.BlockSpec((rows, 1), lambda i, j, pt: (0, 0)),
                  pl.BlockSpec((1, dv), lambda i, j, pt: (0, 0)),
                  vec, vec, vec, vec]
                 + [page_spec(g) for g in range(g_n)] * 2,
        out_specs=row,
        scratch_shapes=[pltpu.VMEM((rows, 1), F32), pltpu.VMEM((rows, 1), F32),
                        pltpu.VMEM((rows, width), F32)],
    )
    ck = cache_k.reshape(cache_k.shape[0], cache_k.shape[1], page, width)
    cv = cache_v.reshape(cache_v.shape[0], cache_v.shape[1], page, width)
    return pl.pallas_call(
        kern,
        grid_spec=grid_spec,
        out_shape=jax.ShapeDtypeStruct((db, 1, width), BF16),
        compiler_params=_cparams(("arbitrary", "arbitrary")),
        name="da_sample",
    )(page_table, a["dq"], a["dkf"], a["dvf"], a["dg"], bias_rows, bias0,
      da_norm_l.reshape(1, dv), *lam_rows, *([ck] * g_n), *([cv] * g_n))


def _out_kernel(x_ref, oh_ref, od_ref, mq_ref, mg_ref, mk_ref, mv_ref, w_ref, nw_ref, y_ref, *,
                heads, dh, hgw, daw):
    mq = mq_ref[0]
    mg = mg_ref[0].astype(F32)
    parts = []
    for h in range(heads):
        sl = slice(h * dh, (h + 1) * dh)
        s = _dot_nt(mq[:, sl], mk_ref[0, :, sl].astype(BF16)) * (dh ** -0.5)
        p = jnp.exp(s - jnp.max(s, axis=-1, keepdims=True))
        o = _dot(p.astype(BF16), mv_ref[0, :, sl].astype(BF16))
        o = o / jnp.sum(p, axis=-1, keepdims=True)
        parts.append((o * mg[:, sl]).astype(BF16))
    om = jnp.concatenate(parts, axis=-1)
    y = _dot(oh_ref[0], w_ref[0, :hgw, :])
    y = y + _dot(od_ref[0], w_ref[0, hgw:hgw + daw, :])
    y = y + _dot(om, w_ref[0, hgw + daw:, :])
    y_ref[0] = x_ref[0] + _rms(y, nw_ref[0])


def _out_proj(x, oh, od, a, mk, mv, mlayer, layer, w_out_bf, norm_post, *, heads, dh, tm):
    b, t, d_model = x.shape
    depth, d_mix, _ = w_out_bf.shape
    hgw = oh.shape[-1]
    daw = od.shape[-1]
    mxw = heads * dh
    n_mem = mk.shape[2]

    def act(width):
        return pl.BlockSpec((1, tm, width), lambda i, j: (i, j, 0))

    mem = pl.BlockSpec((None, 1, n_mem, mxw), lambda i, j: (mlayer, i, 0, 0))
    return pl.pallas_call(
        functools.partial(_out_kernel, heads=heads, dh=dh, hgw=hgw, daw=daw),
        grid=(b, t // tm),
        out_shape=jax.ShapeDtypeStruct((b, t, d_model), F32),
        in_specs=[act(d_model), act(hgw), act(daw), act(mxw), act(mxw), mem, mem,
                  pl.BlockSpec((1, d_mix, d_model), lambda i, j: (layer, 0, 0),
                               pipeline_mode=pl.Buffered(1)),
                  pl.BlockSpec((1, 1, d_model), lambda i, j: (layer, 0, 0))],
        out_specs=act(d_model),
        compiler_params=_cparams(("arbitrary", "arbitrary")),
        name="out_proj",
    )(x, oh, od, a["mq"], a["mg"], mk, mv, w_out_bf, norm_post.reshape(depth, 1, d_model))


def kernel(x_prompt, x_sample, mem_prompt, cache_da_k, cache_da_v, cache_mem_k, cache_mem_v,
           state_hgrn, page_table, w_in, w_out, w_mem_kv, norm_pre, norm_post, mem_norm, hg_norm,
           da_norm, hg_lb, da_lq1, da_lk1, da_lq2, da_lk2, rel_bias):
    depth = w_in.shape[0]
    b, t, d_model = x_prompt.shape
    db = x_sample.shape[0]
    hg_heads, hg_dk = state_hgrn.shape[2], state_hgrn.shape[3]
    hgw = hg_heads * hg_dk
    da_heads, dv = cache_da_v.shape[3], cache_da_v.shape[4]
    daw = da_heads * dv
    mx_heads, mx_dh = cache_mem_k.shape[3], cache_mem_k.shape[4]
    mxw = mx_heads * mx_dh
    n_mem = mem_prompt.shape[1]
    page = cache_da_k.shape[2]
    past = page_table.shape[1] * page
    assert w_in.shape[2] == 4 * hgw + 4 * daw + 2 * mxw
    assert x_sample.shape[1] == 1 and MAX_DIST <= page <= DA_TK
    assert t % max(DA_TQ, 512) == 0 and page_table.shape[1] % PAGES_PER_STEP == 0

    w_in_bf = w_in.astype(BF16)
    w_out_bf = w_out.astype(BF16)
    w_mem_bf = w_mem_kv.astype(BF16)

    bias = _bias_tiles(rel_bias, da_heads)
    near = bias[:, 0, DA_TK - page:, 0]
    bias_rows = jnp.concatenate([jnp.zeros((da_heads, past - page), F32), near], axis=1)
    bias_rows = jnp.repeat(bias_rows, 2, axis=0)
    bias0 = jnp.repeat(bias[:, 1, 0, 0], 2).reshape(2 * da_heads, 1)

    mk_p, mv_p = _mem_kv(mem_prompt, mem_norm, w_mem_bf)
    cmk = cache_mem_k.reshape(depth, db, n_mem, mxw)
    cmv = cache_mem_v.reshape(depth, db, n_mem, mxw)

    xp, xs = x_prompt, x_sample.reshape(1, db, d_model)
    outs = {n: [] for n in ("kp", "vp", "sp", "ks", "vs", "ss")}
    dims = dict(hgw=hgw, daw=daw, mxw=mxw, da_heads=da_heads)
    for l in range(depth):
        lam_init = 0.8 - 0.6 * math.exp(-0.3 * l)
        lam_rows = [v[l].reshape(1, dv // 2) for v in (da_lq1, da_lk1, da_lq2, da_lk2)]

        a = _in_proj(xp, l, norm_pre, w_in_bf, hg_lb, tm=512, emit_vt=True, **dims)
        oh, s_p = _hgrn_prompt(a, hg_norm[l], heads=hg_heads, dk=hg_dk, tt=512)
        od = _da_prompt(a, bias, da_norm[l], lam_rows, lam_init, heads=da_heads, dv=dv)
        xp = _out_proj(xp, oh, od, a, mk_p, mv_p, l, l, w_out_bf, norm_post,
                       heads=mx_heads, dh=mx_dh, tm=512)
        outs["kp"].append(a["dkf"].reshape(b, t, da_heads, dv))
        outs["vp"].append(a["dvf"].reshape(b, t, da_heads, dv))
        outs["sp"].append(s_p)

        a = _in_proj(xs, l, norm_pre, w_in_bf, hg_lb, tm=db, emit_vt=False, **dims)
        a = {n: v.reshape(db, 1, v.shape[-1]) for n, v in a.items()}
        oh, s_s = _hgrn_step(a, hg_norm[l], state_hgrn, l, heads=hg_heads, dk=hg_dk)
        od = _da_sample(a, cache_da_k, cache_da_v, page_table, l, bias_rows, bias0, da_norm[l],
                        lam_rows, lam_init, heads=da_heads, dv=dv)
        xs3 = _out_proj(xs.reshape(db, 1, d_model), oh, od, a, cmk, cmv, l, l, w_out_bf,
                        norm_post, heads=mx_heads, dh=mx_dh, tm=1)
        xs = xs3.reshape(1, db, d_model)
        outs["ks"].append(a["dkf"].reshape(db, 1, da_heads, dv))
        outs["vs"].append(a["dvf"].reshape(db, 1, da_heads, dv))
        outs["ss"].append(s_s)

    return (xp, xs.reshape(db, 1, d_model), jnp.stack(outs["kp"]), jnp.stack(outs["vp"]),
            jnp.stack(outs["sp"]), mk_p.reshape(depth, b, n_mem, mx_heads, mx_dh),
            mv_p.reshape(depth, b, n_mem, mx_heads, mx_dh), jnp.stack(outs["ks"]),
            jnp.stack(outs["vs"]), jnp.stack(outs["ss"]))
```

```python
import functools
import math

import jax
import jax.numpy as jnp
from jax import lax
from jax.experimental import pallas as pl
from jax.experimental.pallas import tpu as pltpu

F32 = jnp.float32
BF16 = jnp.bfloat16

EPS = 1e-6
MAX_DIST = 128
HG_CHUNK = 64
HG_SUB = 16
HG_UNROLL = 2
HG_HEADS_PER_STEP = 2
DA_HEADS_PER_STEP = 2
LOG2E = math.log2(math.e)
ONES_ROWS = 16
DA_TQ = 256
DA_TK = 256
PAGES_PER_STEP = 8
NEG_BIG = -1e30
VMEM_LIMIT_BYTES = 56 * 1024 * 1024


def _cparams(sem):
    return pltpu.CompilerParams(dimension_semantics=sem, vmem_limit_bytes=VMEM_LIMIT_BYTES)


def _sigmoid(x):
    return 1.0 / (1.0 + jnp.exp(-x))


def _silu(x):
    return x * _sigmoid(x)


def _rms(x, w):
    return x * lax.rsqrt(jnp.mean(x * x, axis=-1, keepdims=True) + EPS) * w


def _dot(a, b):
    return jnp.dot(a, b, preferred_element_type=F32)


def _dot_nt(a, b):
    return lax.dot_general(a, b, (((1,), (1,)), ((), ())), preferred_element_type=F32)


def _dot_tn(a, b):
    return lax.dot_general(a, b, (((0,), (0,)), ((), ())), preferred_element_type=F32)


def _lam(lq1_ref, lk1_ref, lq2_ref, lk2_ref, lam_init):
    a = jnp.sum(lq1_ref[...] * lk1_ref[...], axis=-1, keepdims=True)
    b = jnp.sum(lq2_ref[...] * lk2_ref[...], axis=-1, keepdims=True)
    return jnp.exp(a) - jnp.exp(b) + lam_init


def _bias_kernel(tab_ref, out_ref, *, n_buckets, heads, tk, tq):
    kk = lax.broadcasted_iota(jnp.int32, (tk, tq), 0)
    qq = lax.broadcasted_iota(jnp.int32, (tk, tq), 1)
    max_exact = n_buckets // 2
    for jj in range(2):
        d = qq - kk + (tk if jj == 0 else 0)
        n = jnp.maximum(d, 0)
        nf = jnp.maximum(n, 1).astype(F32)
        large = max_exact + (jnp.log(nf / max_exact) / math.log(MAX_DIST / max_exact)
                             * (n_buckets - max_exact)).astype(jnp.int32)
        large = jnp.minimum(large, n_buckets - 1)
        bucket = jnp.where(n < max_exact, n, large)
        for h in range(heads):
            val = jnp.zeros((tk, tq), F32)
            for b in range(n_buckets):
                val = jnp.where(bucket == b, tab_ref[b, h], val)
            val = (val - tab_ref[n_buckets - 1, h]) * LOG2E
            out_ref[h, jj] = jnp.where(d < 0, NEG_BIG, val)


def _bias_tiles(rel_bias, heads):
    n_buckets = rel_bias.shape[0]
    return pl.pallas_call(
        functools.partial(_bias_kernel, n_buckets=n_buckets, heads=heads, tk=DA_TK, tq=DA_TQ),
        out_shape=jax.ShapeDtypeStruct((heads, 2, DA_TK, DA_TQ), F32),
        in_specs=[pl.BlockSpec(memory_space=pltpu.SMEM)],
        out_specs=pl.BlockSpec(memory_space=pltpu.VMEM),
        name="t5_bias",
    )(rel_bias)


def _memkv_kernel(mem_ref, nw_ref, w_ref, k_ref, v_ref, *, width):
    xn = _rms(mem_ref[0], nw_ref[0]).astype(BF16)
    y = _dot(xn, w_ref[0])
    k_ref[0, 0] = y[:, :width]
    v_ref[0, 0] = y[:, width:]


def _mem_kv(mem_prompt, mem_norm, w_mem_kv_bf):
    depth, d_model, two_w = w_mem_kv_bf.shape
    width = two_w // 2
    b, n_mem, _ = mem_prompt.shape
    out = jax.ShapeDtypeStruct((depth, b, n_mem, width), F32)
    return pl.pallas_call(
        functools.partial(_memkv_kernel, width=width),
        grid=(depth, b),
        out_shape=(out, out),
        in_specs=[
            pl.BlockSpec((1, n_mem, d_model), lambda l, i: (i, 0, 0)),
            pl.BlockSpec((1, 1, d_model), lambda l, i: (l, 0, 0)),
            pl.BlockSpec((1, d_model, two_w), lambda l, i: (l, 0, 0)),
        ],
        out_specs=(pl.BlockSpec((1, 1, n_mem, width), lambda l, i: (l, i, 0, 0)),
                   pl.BlockSpec((1, 1, n_mem, width), lambda l, i: (l, i, 0, 0))),
        compiler_params=_cparams(("arbitrary", "arbitrary")),
        name="mem_kv",
    )(mem_prompt, mem_norm.reshape(depth, 1, d_model), w_mem_kv_bf)


def _inproj_kernel(x_ref, nw_ref, w_ref, lbp_ref, *out_refs, layer, hgw, daw, mxw, da_heads,
                   tkb, emit_vt, col_chunk):
    if emit_vt:
        (qh_ref, kh_ref, vh_ref, lf_ref, gh_ref, dq_ref, dkf_ref, dkb_ref, dvf_ref, dvt_ref,
         dg_ref, mq_ref, mg_ref) = out_refs
    else:
        (qh_ref, kh_ref, vh_ref, lf_ref, gh_ref, dq_ref, dkf_ref, dkb_ref, dvf_ref,
         dg_ref, mq_ref, mg_ref) = out_refs
        dvt_ref = None
    xn = _rms(x_ref[0], nw_ref[0]).astype(BF16)

    p = lbp_ref[...]
    e = jnp.exp(p - jnp.max(p, axis=0, keepdims=True))
    sm = e / jnp.sum(e, axis=0, keepdims=True)
    lb = jnp.zeros((1, hgw), F32)
    for r in range(1, layer + 1):
        lb = lb + sm[r:r + 1, :]

    def seg(c0, width, fn):
        for c in range(0, width, col_chunk):
            w = min(col_chunk, width - c)
            fn(c, w, _dot(xn, w_ref[0, :, c0 + c:c0 + c + w]))

    def st(ref, fn):
        def go(c, w, y):
            ref[0, :, c:c + w] = fn(y).astype(ref.dtype)
        return go

    def forget(c, w, y):
        f = lb[:, c:c + w] + (1.0 - lb[:, c:c + w]) * _sigmoid(y)
        lf_ref[0, :, c:c + w] = jnp.log2(f)
        kh_ref[0, :, c:c + w] = (1.0 - f).astype(BF16)

    def da_k(c, w, y):
        dkf_ref[0, :, c:c + w] = y
        dkb_ref[0, :, c:c + w] = y.astype(BF16)

    def da_v(c, w, y):
        dvf_ref[0, :, c:c + w] = y
        if dvt_ref is not None:
            dv = daw // da_heads
            tm = y.shape[0]
            for hh in range(c // dv, (c + w) // dv):
                for kb in range(tm // tkb):
                    blk = y[kb * tkb:(kb + 1) * tkb, hh * dv - c:(hh + 1) * dv - c]
                    dvt_ref[0, hh, kb] = blk.T.astype(BF16)

    o = 0
    seg(o, hgw, st(qh_ref, _silu)); o += hgw
    seg(o, hgw, forget); o += hgw
    seg(o, hgw, st(vh_ref, lambda y: y)); o += hgw
    seg(o, hgw, st(gh_ref, _silu)); o += hgw
    dqk = daw // da_heads // 2
    seg(o, daw, st(dq_ref, lambda y: y * (dqk ** -0.5 * LOG2E))); o += daw
    seg(o, daw, da_k); o += daw
    seg(o, daw, da_v); o += daw
    seg(o, daw, st(dg_ref, _silu)); o += daw
    seg(o, mxw, st(mq_ref, lambda y: y)); o += mxw
    seg(o, mxw, st(mg_ref, _silu)); o += mxw


def _in_proj(x, layer, norm_pre, w_in_bf, hg_lb, *, hgw, daw, mxw, da_heads, tm, emit_vt):
    b, t, d_model = x.shape
    depth, _, d_in = w_in_bf.shape
    tkb = DA_TK
    dv = daw // da_heads

    def act(width, dtype):
        return (jax.ShapeDtypeStruct((b, t, width), dtype),
                pl.BlockSpec((1, tm, width), lambda i, j: (i, j, 0)))

    outs = [act(hgw, BF16), act(hgw, BF16), act(hgw, BF16), act(hgw, F32), act(hgw, BF16),
            act(daw, BF16), act(daw, F32), act(daw, BF16), act(daw, F32)]
    if emit_vt:
        outs.append((jax.ShapeDtypeStruct((b, da_heads, t // tkb, dv, tkb), BF16),
                     pl.BlockSpec((1, da_heads, tm // tkb, dv, tkb), lambda i, j: (i, 0, j, 0, 0))))
    outs += [act(daw, BF16), act(mxw, BF16), act(mxw, BF16)]
    kern = functools.partial(_inproj_kernel, layer=layer, hgw=hgw, daw=daw, mxw=mxw,
                             da_heads=da_heads, tkb=tkb, emit_vt=emit_vt, col_chunk=512)
    res = pl.pallas_call(
        kern,
        grid=(b, t // tm),
        out_shape=tuple(o[0] for o in outs),
        in_specs=[
            pl.BlockSpec((1, tm, d_model), lambda i, j: (i, j, 0)),
            pl.BlockSpec((1, 1, d_model), lambda i, j: (layer, 0, 0)),
            pl.BlockSpec((1, d_model, d_in), lambda i, j: (layer, 0, 0),
                         pipeline_mode=pl.Buffered(1)),
            pl.BlockSpec((depth, hgw), lambda i, j: (0, 0)),
        ],
        out_specs=tuple(o[1] for o in outs),
        compiler_params=_cparams(("arbitrary", "arbitrary")),
        name="in_proj_vt" if emit_vt else "in_proj",
    )(x, norm_pre.reshape(depth, 1, d_model), w_in_bf, hg_lb)
    names = ["qh", "kh", "vh", "lf", "gh", "dq", "dkf", "dkb", "dvf"]
    names += ["dvt"] if emit_vt else []
    names += ["dg", "mq", "mg"]
    return dict(zip(names, res))


def _hgrn_off_shape(chunk, sub):
    nsub = chunk // sub
    n_off_cols = sub * (nsub * (nsub - 1) // 2)
    return chunk - sub, n_off_cols, -(-n_off_cols // 128) * 128


def _hgrn_kernel(q_ref, k_ref, v_ref, g_ref, gate_ref, nw_ref, o_ref, s_ref, st_ref, tri_ref,
                 offm_ref, diagm_ref, cum_ref, *, chunk, sub, n_chunks, unroll, hp, dk):
    j = pl.program_id(2)
    nsub = chunk // sub
    n_off_rows, n_off_cols, n_off_pad = _hgrn_off_shape(chunk, sub)

    @pl.when(j == 0)
    def _():
        st_ref[...] = jnp.zeros_like(st_ref)
        r_i = lax.broadcasted_iota(jnp.int32, (chunk, chunk), 0)
        c_i = lax.broadcasted_iota(jnp.int32, (chunk, chunk), 1)
        tri_ref[...] = jnp.where(c_i <= r_i, 1.0, 0.0).astype(BF16)

        ro = lax.broadcasted_iota(jnp.int32, (n_off_rows, n_off_pad), 0) // sub + 1
        co = lax.broadcasted_iota(jnp.int32, (n_off_rows, n_off_pad), 1)
        cblk = jnp.zeros_like(co)
        start = 0
        for i in range(1, nsub):
            cblk = jnp.where((co >= start) & (co < start + i * sub), i, cblk)
            start += i * sub
        offm_ref[...] = jnp.where(ro == cblk, 1.0, 0.0)

        rd = lax.broadcasted_iota(jnp.int32, (nsub * sub * sub, chunk), 0)
        cd = lax.broadcasted_iota(jnp.int32, (nsub * sub * sub, chunk), 1)
        r_is = rd // sub
        diagm_ref[...] = jnp.where((r_is == cd) & (rd % sub >= r_is % sub), 1.0, 0.0)

    g = jnp.concatenate([g_ref[0, c * chunk:(c + 1) * chunk, :] for c in range(n_chunks)], axis=1)
    g1 = g.astype(BF16)
    rem = g - g1.astype(F32)
    g2 = rem.astype(BF16)
    g3 = (rem - g2.astype(F32)).astype(BF16)
    tri = tri_ref[...]
    cum_all = _dot(tri, g1) + _dot(tri, g2) + _dot(tri, g3)
    for c in range(n_chunks * hp):
        cum_ref[c] = cum_all[:, c * dk:(c + 1) * dk]

    def head_chunk(ci, hh):
        r0 = pl.multiple_of(ci * chunk, chunk)
        cols = slice(hh * dk, (hh + 1) * dk)
        q = q_ref[0, pl.ds(r0, chunk), cols].astype(F32)
        k_bf = k_ref[0, pl.ds(r0, chunk), cols]
        k = k_bf.astype(F32)
        v = v_ref[0, pl.ds(r0, chunk), cols]
        cum = cum_ref[ci * hp + hh]

        s_prev = st_ref[hh]
        last = cum[chunk - 1:chunk, :]
        o = _dot_nt((q * jnp.exp2(cum)).astype(BF16), s_prev.astype(BF16))
        kl = (k * jnp.exp2(last - cum)).astype(BF16)
        st_ref[hh] = s_prev * jnp.exp2(last) + _dot_tn(v, kl)

        q_parts, k_parts, v_parts = [], [], []
        v32 = v.astype(F32)
        for i in range(1, nsub):
            b_i = cum[i * sub - 1:i * sub, :]
            lo, hi = i * sub, (i + 1) * sub
            q_parts.append(q[lo:hi] * jnp.exp2(cum[lo:hi] - b_i))
            k_parts.append(k[:lo] * jnp.exp2(b_i - cum[:lo]))
            v_parts.append(v32[:lo])
        if n_off_pad > n_off_cols:
            k_parts.append(jnp.zeros((n_off_pad - n_off_cols, dk), F32))
            v_parts.append(jnp.zeros((n_off_pad - n_off_cols, v.shape[-1]), F32))
        a_off = _dot_nt(jnp.concatenate(q_parts, axis=0).astype(BF16),
                        jnp.concatenate(k_parts, axis=0).astype(BF16))
        a_off = (a_off * offm_ref[...]).astype(BF16)
        o_off = _dot(a_off, jnp.concatenate(v_parts, axis=0).astype(BF16))

        n_parts = []
        for i in range(nsub):
            lo, hi = i * sub, (i + 1) * sub
            c_blk = cum[lo:hi]
            q_blk = q[lo:hi]
            for s in range(sub):
                dec = jnp.exp2(jnp.minimum(c_blk - c_blk[s:s + 1, :], 0.0))
                n_parts.append(q_blk * dec)
        r = _dot_nt(jnp.concatenate(n_parts, axis=0).astype(BF16), k_bf)
        r = r * diagm_ref[...]
        a_parts = []
        for i in range(nsub):
            base = i * sub * sub
            acc = r[base:base + sub]
            for s in range(1, sub):
                acc = acc + r[base + s * sub:base + (s + 1) * sub]
            a_parts.append(acc)
        a_diag = jnp.concatenate(a_parts, axis=0).astype(BF16)
        o = o + _dot(a_diag, v)
        o = o + jnp.concatenate([jnp.zeros((sub, o.shape[-1]), F32), o_off], axis=0)

        gate = gate_ref[0, pl.ds(r0, chunk), cols].astype(F32)
        o_ref[0, pl.ds(r0, chunk), cols] = (_rms(o, nw_ref[...]) * gate).astype(o_ref.dtype)

    def body(ci, carry):
        for hh in range(hp):
            head_chunk(ci, hh)
        return carry

    lax.fori_loop(0, n_chunks, body, 0, unroll=unroll)

    @pl.when(j == pl.num_programs(2) - 1)
    def _():
        for hh in range(hp):
            s_ref[0, hh] = st_ref[hh].T


def _hgrn_prompt(a, hg_norm_l, *, heads, dk, tt):
    b, t, _ = a["qh"].shape
    hp = HG_HEADS_PER_STEP
    spec = pl.BlockSpec((1, tt, hp * dk), lambda i, h, j: (i, j, h))
    chunk, sub = HG_CHUNK, HG_SUB
    n_chunks = tt // chunk
    n_off_rows, _, n_off_pad = _hgrn_off_shape(chunk, sub)
    kern = functools.partial(_hgrn_kernel, chunk=chunk, sub=sub, n_chunks=n_chunks,
                             unroll=HG_UNROLL, hp=hp, dk=dk)
    return pl.pallas_call(
        kern,
        grid=(b, heads // hp, t // tt),
        out_shape=(jax.ShapeDtypeStruct((b, t, heads * dk), BF16),
                   jax.ShapeDtypeStruct((b, heads, dk, dk), F32)),
        in_specs=[spec, spec, spec, spec, spec, pl.BlockSpec((1, dk), lambda i, h, j: (0, 0))],
        out_specs=(spec, pl.BlockSpec((1, hp, dk, dk), lambda i, h, j: (i, h, 0, 0))),
        scratch_shapes=[pltpu.VMEM((hp, dk, dk), F32), pltpu.VMEM((chunk, chunk), BF16),
                        pltpu.VMEM((n_off_rows, n_off_pad), F32),
                        pltpu.VMEM((chunk * sub, chunk), F32),
                        pltpu.VMEM((n_chunks * hp, chunk, dk), F32)],
        compiler_params=_cparams(("arbitrary", "arbitrary", "arbitrary")),
        name="hgrn_prompt",
    )(a["qh"], a["kh"], a["vh"], a["lf"], a["gh"], hg_norm_l.reshape(1, dk))


def _hgrn_step_kernel(q_ref, lf_ref, v_ref, gate_ref, nw_ref, s_ref, o_ref, sn_ref, *, heads, dk):
    def col(row):
        return jnp.broadcast_to(row, (dk, dk)).T

    for h in range(heads):
        sl = slice(h * dk, (h + 1) * dk)
        f = jnp.exp2(lf_ref[0, :, sl])
        s_new = col(f) * s_ref[0, h] + col(1.0 - f) * v_ref[0, :, sl].astype(F32)
        sn_ref[0, h] = s_new
        o = jnp.sum(col(q_ref[0, :, sl].astype(F32)) * s_new, axis=0, keepdims=True)
        o_ref[0, :, sl] = (_rms(o, nw_ref[...]) * gate_ref[0, :, sl].astype(F32)).astype(o_ref.dtype)


def _hgrn_step(a, hg_norm_l, state_hgrn, layer, *, heads, dk):
    db = a["qh"].shape[0]
    w = heads * dk
    row = pl.BlockSpec((1, 1, w), lambda i: (i, 0, 0))
    return pl.pallas_call(
        functools.partial(_hgrn_step_kernel, heads=heads, dk=dk),
        grid=(db,),
        out_shape=(jax.ShapeDtypeStruct((db, 1, w), BF16),
                   jax.ShapeDtypeStruct((db, heads, dk, dk), F32)),
        in_specs=[row, row, row, row, pl.BlockSpec((1, dk), lambda i: (0, 0)),
                  pl.BlockSpec((None, 1, heads, dk, dk), lambda i: (layer, i, 0, 0, 0))],
        out_specs=(row, pl.BlockSpec((1, heads, dk, dk), lambda i: (i, 0, 0, 0))),
        compiler_params=_cparams(("arbitrary",)),
        name="hgrn_step",
    )(a["qh"], a["lf"], a["vh"], a["gh"], hg_norm_l.reshape(1, dk), state_hgrn)


def _da_prompt_kernel(q_ref, k_ref, vt_ref, bias_ref, gate_ref, nw_ref, lq1_ref, lk1_ref,
                      lq2_ref, lk2_ref, o_ref, acc_ref, s0_ref, s1_ref, p0_ref, p1_ref, al0_ref,
                      al1_ref, *, tq, tk, lam_init, hp, dv):
    i = pl.program_id(2)
    half = dv // 2
    row = lax.broadcasted_iota(jnp.int32, (dv, tq), 0)

    def qbd_of(hh):
        qt = q_ref[0, :, hh * dv:(hh + 1) * dv].astype(F32).T
        return jnp.concatenate([jnp.where(row < half, qt, 0.0), jnp.where(row >= half, qt, 0.0)],
                               axis=1).astype(BF16)

    qbd = [qbd_of(hh) for hh in range(hp)]
    s_refs, p_refs, al_refs = (s0_ref, s1_ref), (p0_ref, p1_ref), (al0_ref, al1_ref)
    acc_ref[...] = jnp.zeros_like(acc_ref)
    p1_ref[...] = jnp.zeros_like(p1_ref)
    al1_ref[...] = jnp.ones_like(al1_ref)

    def scores(slot, blk):
        for hh in range(hp):
            k_blk = k_ref[0, pl.ds(pl.multiple_of(blk * tk, tk), tk), hh * dv:(hh + 1) * dv]
            s_refs[slot][hh] = _dot(k_blk, qbd[hh])

    def probs(slot, ms, bias_idx):
        out = []
        for hh in range(hp):
            s = s_refs[slot][hh]
            if bias_idx is not None:
                bias = bias_ref[hh, bias_idx]
                s = s + jnp.concatenate([bias, bias], axis=1)
            m_new = jnp.maximum(ms[hh], jnp.max(s, axis=0, keepdims=True))
            al_refs[slot][hh] = jnp.exp2(ms[hh] - m_new)
            p_refs[slot][hh] = jnp.exp2(s - m_new).astype(BF16)
            out.append(m_new)
        return tuple(out)

    def accumulate(slot, blk):
        kb = jnp.maximum(blk, 0)
        for hh in range(hp):
            vt = jnp.concatenate([vt_ref[0, hh, kb], jnp.ones((ONES_ROWS, tk), BF16)], axis=0)
            acc_ref[hh] = al_refs[slot][hh] * acc_ref[hh] + _dot(vt, p_refs[slot][hh])

    def far_pair(u, ms):
        e = 2 * u
        accumulate(1, e - 1)
        ms = probs(0, ms, None)
        scores(1, e + 1)
        accumulate(0, e)
        ms = probs(1, ms, None)
        scores(0, e + 2)
        return ms

    def tail_odd(ms):
        accumulate(1, i - 2)
        ms = probs(0, ms, 0)
        scores(1, i)
        accumulate(0, i - 1)
        ms = probs(1, ms, 1)
        accumulate(1, i)
        return ms

    def tail_even(ms):
        accumulate(1, i - 3)
        ms = probs(0, ms, None)
        scores(1, i - 1)
        accumulate(0, i - 2)
        ms = probs(1, ms, 0)
        scores(0, i)
        accumulate(1, i - 1)
        ms = probs(0, ms, 1)
        accumulate(0, i)
        return ms

    def tail_zero(ms):
        ms = probs(0, ms, 1)
        accumulate(0, i)
        return ms

    m = tuple(jnp.full((1, 2 * tq), 0.1 * NEG_BIG, F32) for _ in range(hp))
    scores(0, 0)
    m = lax.fori_loop(0, jnp.right_shift(jnp.maximum(i - 1, 0), 1), far_pair, m)
    lax.cond(i == 0, tail_zero,
             lambda ms: lax.cond((i & 1) == 1, tail_odd, tail_even, ms), m)

    lam = _lam(lq1_ref, lk1_ref, lq2_ref, lk2_ref, lam_init)
    for hh in range(hp):
        acc = acc_ref[hh, :dv, :]
        inv = 1.0 / acc_ref[hh, dv:dv + 1, :]
        out_t = acc[:, :tq] * inv[:, :tq] - lam * (acc[:, tq:] * inv[:, tq:])
        out = _rms(out_t.T, nw_ref[...]) * (1.0 - lam_init)
        cols = slice(hh * dv, (hh + 1) * dv)
        o_ref[0, :, cols] = (out * gate_ref[0, :, cols].astype(F32)).astype(o_ref.dtype)


def _da_prompt(a, bias, da_norm_l, lam_rows, lam_init, *, heads, dv):
    b, t, _ = a["dq"].shape
    tq, tk = DA_TQ, DA_TK
    hp = DA_HEADS_PER_STEP
    qspec = pl.BlockSpec((1, tq, hp * dv), lambda i, h, j: (i, j, h))
    vec = pl.BlockSpec((1, dv // 2), lambda i, h, j: (0, 0))
    kern = functools.partial(_da_prompt_kernel, tq=tq, tk=tk, lam_init=lam_init, hp=hp, dv=dv)
    return pl.pallas_call(
        kern,
        grid=(b, heads // hp, t // tq),
        out_shape=jax.ShapeDtypeStruct((b, t, heads * dv), BF16),
        in_specs=[
            qspec,
            pl.BlockSpec((1, t, hp * dv), lambda i, h, j: (i, 0, h)),
            pl.BlockSpec((1, hp, t // tk, dv, tk), lambda i, h, j: (i, h, 0, 0, 0)),
            pl.BlockSpec((hp, 2, tk, tq), lambda i, h, j: (h, 0, 0, 0)),
            qspec,
            pl.BlockSpec((1, dv), lambda i, h, j: (0, 0)),
            vec, vec, vec, vec,
        ],
        out_specs=qspec,
        scratch_shapes=[pltpu.VMEM((hp, dv + ONES_ROWS, 2 * tq), F32),
                        pltpu.VMEM((hp, tk, 2 * tq), F32), pltpu.VMEM((hp, tk, 2 * tq), F32),
                        pltpu.VMEM((hp, tk, 2 * tq), BF16), pltpu.VMEM((hp, tk, 2 * tq), BF16),
                        pltpu.VMEM((hp, 1, 2 * tq), F32), pltpu.VMEM((hp, 1, 2 * tq), F32)],
        compiler_params=_cparams(("arbitrary", "arbitrary", "arbitrary")),
        name="da_prompt",
    )(a["dq"], a["dkb"], a["dvt"], bias, a["dg"], da_norm_l.reshape(1, dv), *lam_rows)


def _da_sample_kernel(pt_ref, q_ref, kn_ref, vn_ref, gate_ref, bias_ref, bias0_ref, nw_ref,
                      lq1_ref, lk1_ref, lq2_ref, lk2_ref, *refs, heads, dv, n_pages, lam_init):
    del pt_ref
    k_refs = refs[:n_pages]
    v_refs = refs[n_pages:2 * n_pages]
    o_ref, m_ref, l_ref, acc_ref = refs[2 * n_pages:]
    j = pl.program_id(1)
    half = dv // 2
    rows = 2 * heads

    @pl.when(j == 0)
    def _():
        m_ref[...] = jnp.full_like(m_ref, 0.1 * NEG_BIG)
        l_ref[...] = jnp.zeros_like(l_ref)
        acc_ref[...] = jnp.zeros_like(acc_ref)

    def per_map(x):
        return jnp.concatenate([x[:, h * dv:(h + 1) * dv] for h in range(heads) for _ in range(2)],
                               axis=0)

    r_i = lax.broadcasted_iota(jnp.int32, (rows, dv), 0)
    c_i = lax.broadcasted_iota(jnp.int32, (rows, dv), 1)
    q8 = jnp.where(c_i // half == r_i % 2, per_map(q_ref[0].astype(F32)), 0.0).astype(BF16)

    s = jnp.concatenate([_dot_nt(q8, k_refs[g][0, 0].astype(BF16)) for g in range(n_pages)],
                        axis=1)
    s = s + bias_ref[...]
    m_old = m_ref[...]
    m_new = jnp.maximum(m_old, jnp.max(s, axis=-1, keepdims=True))
    alpha = jnp.exp2(m_old - m_new)
    p = jnp.exp2(s - m_new)
    l_ref[...] = alpha * l_ref[...] + jnp.sum(p, axis=-1, keepdims=True)
    m_ref[...] = m_new
    n_cols = k_refs[0].shape[2]
    pv = jnp.zeros((rows, dv), F32)
    for g in range(n_pages):
        pv = pv + _dot(p[:, g * n_cols:(g + 1) * n_cols].astype(BF16),
                       v_refs[g][0, 0].astype(BF16))
    acc_ref[...] = alpha * acc_ref[...] + pv

    @pl.when(j == pl.num_programs(1) - 1)
    def _():
        kn = per_map(kn_ref[0].astype(BF16).astype(F32))
        s_n = jnp.sum(q8.astype(F32) * kn, axis=-1, keepdims=True) + bias0_ref[...]
        m_o = m_ref[...]
        m_f = jnp.maximum(m_o, s_n)
        al = jnp.exp2(m_o - m_f)
        p_n = jnp.exp2(s_n - m_f)
        l_f = al * l_ref[...] + p_n
        vn = per_map(vn_ref[0].astype(BF16).astype(F32))
        z = (al * acc_ref[...] + p_n.astype(BF16).astype(F32) * vn) / l_f
        lam = _lam(lq1_ref, lk1_ref, lq2_ref, lk2_ref, lam_init)
        gate = gate_ref[0].astype(F32)
        for h in range(heads):
            sl = slice(h * dv, (h + 1) * dv)
            o_h = z[2 * h:2 * h + 1] - lam * z[2 * h + 1:2 * h + 2]
            o_h = _rms(o_h, nw_ref[...]) * (1.0 - lam_init)
            o_ref[0, :, sl] = (o_h * gate[:, sl]).astype(o_ref.dtype)


def _da_sample(a, cache_k, cache_v, page_table, layer, bias_rows, bias0, da_norm_l, lam_rows,
               lam_init, *, heads, dv):
    db = a["dq"].shape[0]
    width = heads * dv
    n_pages_seq = page_table.shape[1]
    page = cache_k.shape[2]
    g_n = PAGES_PER_STEP
    steps = n_pages_seq // g_n
    rows = 2 * heads
    row = pl.BlockSpec((1, 1, width), lambda i, j, pt: (i, 0, 0))
    vec = pl.BlockSpec((1, dv // 2), lambda i, j, pt: (0, 0))

    def page_spec(g):
        return pl.BlockSpec((1, 1, page * heads, dv),
                            lambda i, j, pt, g=g: (layer, pt[i, j * g_n + g], 0, 0))

    kern = functools.partial(_da_sample_kernel, heads=heads, dv=dv, n_pages=g_n, lam_init=lam_init)
    grid_spec = pltpu.PrefetchScalarGridSpec(
        num_scalar_prefetch=1,
        grid=(db, steps),
        in_specs=[row, row, row, row,
                  pl.BlockSpec((rows, g_n * page * heads), lambda i, j, pt: (0, j)),
                  pl.BlockSpec((rows, 1), lambda i, j, pt: (0, 0)),
                  pl.BlockSpec((1, dv), lambda i, j, pt: (0, 0)),
                  vec, vec, vec, vec]
                 + [page_spec(g) for g in range(g_n)] * 2,
        out_specs=row,
        scratch_shapes=[pltpu.VMEM((rows, 1), F32), pltpu.VMEM((rows, 1), F32),
                        pltpu.VMEM((rows, dv), F32)],
    )
    ck = cache_k.reshape(cache_k.shape[0], cache_k.shape[1], page * heads, dv)
    cv = cache_v.reshape(cache_v.shape[0], cache_v.shape[1], page * heads, dv)
    return pl.pallas_call(
        kern,
        grid_spec=grid_spec,
        out_shape=jax.ShapeDtypeStruct((db, 1, width), BF16),
        compiler_params=_cparams(("arbitrary", "arbitrary")),
        name="da_sample",
    )(page_table, a["dq"], a["dkf"], a["dvf"], a["dg"], bias_rows, bias0,
      da_norm_l.reshape(1, dv), *lam_rows, *([ck] * g_n), *([cv] * g_n))


def _out_kernel(x_ref, oh_ref, od_ref, mq_ref, mg_ref, mk_ref, mv_ref, w_ref, nw_ref, y_ref, *,
                heads, dh, hgw, daw):
    mq = mq_ref[0]
    mg = mg_ref[0].astype(F32)
    parts = []
    for h in range(heads):
        sl = slice(h * dh, (h + 1) * dh)
        s = _dot_nt(mq[:, sl], mk_ref[0, :, sl].astype(BF16)) * (dh ** -0.5)
        p = jnp.exp(s - jnp.max(s, axis=-1, keepdims=True))
        o = _dot(p.astype(BF16), mv_ref[0, :, sl].astype(BF16))
        o = o / jnp.sum(p, axis=-1, keepdims=True)
        parts.append((o * mg[:, sl]).astype(BF16))
    om = jnp.concatenate(parts, axis=-1)
    y = _dot(oh_ref[0], w_ref[0, :hgw, :])
    y = y + _dot(od_ref[0], w_ref[0, hgw:hgw + daw, :])
    y = y + _dot(om, w_ref[0, hgw + daw:, :])
    y_ref[0] = x_ref[0] + _rms(y, nw_ref[0])


def _out_proj(x, oh, od, a, mk, mv, mlayer, layer, w_out_bf, norm_post, *, heads, dh, tm):
    b, t, d_model = x.shape
    depth, d_mix, _ = w_out_bf.shape
    hgw = oh.shape[-1]
    daw = od.shape[-1]
    mxw = heads * dh
    n_mem = mk.shape[2]

    def act(width):
        return pl.BlockSpec((1, tm, width), lambda i, j: (i, j, 0))

    mem = pl.BlockSpec((None, 1, n_mem, mxw), lambda i, j: (mlayer, i, 0, 0))
    return pl.pallas_call(
        functools.partial(_out_kernel, heads=heads, dh=dh, hgw=hgw, daw=daw),
        grid=(b, t // tm),
        out_shape=jax.ShapeDtypeStruct((b, t, d_model), F32),
        in_specs=[act(d_model), act(hgw), act(daw), act(mxw), act(mxw), mem, mem,
                  pl.BlockSpec((1, d_mix, d_model), lambda i, j: (layer, 0, 0),
                               pipeline_mode=pl.Buffered(1)),
                  pl.BlockSpec((1, 1, d_model), lambda i, j: (layer, 0, 0))],
        out_specs=act(d_model),
        compiler_params=_cparams(("arbitrary", "arbitrary")),
        name="out_proj",
    )(x, oh, od, a["mq"], a["mg"], mk, mv, w_out_bf, norm_post.reshape(depth, 1, d_model))


def kernel(x_prompt, x_sample, mem_prompt, cache_da_k, cache_da_v, cache_mem_k, cache_mem_v,
           state_hgrn, page_table, w_in, w_out, w_mem_kv, norm_pre, norm_post, mem_norm, hg_norm,
           da_norm, hg_lb, da_lq1, da_lk1, da_lq2, da_lk2, rel_bias):
    depth = w_in.shape[0]
    b, t, d_model = x_prompt.shape
    db = x_sample.shape[0]
    hg_heads, hg_dk = state_hgrn.shape[2], state_hgrn.shape[3]
    hgw = hg_heads * hg_dk
    da_heads, dv = cache_da_v.shape[3], cache_da_v.shape[4]
    daw = da_heads * dv
    mx_heads, mx_dh = cache_mem_k.shape[3], cache_mem_k.shape[4]
    mxw = mx_heads * mx_dh
    n_mem = mem_prompt.shape[1]
    page = cache_da_k.shape[2]
    past = page_table.shape[1] * page
    assert w_in.shape[2] == 4 * hgw + 4 * daw + 2 * mxw
    assert x_sample.shape[1] == 1 and MAX_DIST <= page <= DA_TK
    assert t % max(DA_TQ, 512) == 0 and page_table.shape[1] % PAGES_PER_STEP == 0

    w_in_bf = w_in.astype(BF16)
    w_out_bf = w_out.astype(BF16)
    w_mem_bf = w_mem_kv.astype(BF16)

    bias = _bias_tiles(rel_bias, da_heads)
    near = bias[:, 0, DA_TK - page:, 0]
    bias_rows = jnp.concatenate([jnp.zeros((da_heads, past - page), F32), near], axis=1)
    bias_rows = jnp.repeat(bias_rows, 2, axis=0)
    col_head = jnp.arange(past * da_heads, dtype=jnp.int32) % da_heads
    row_head = jnp.arange(2 * da_heads, dtype=jnp.int32) // 2
    bias_rows = jnp.where(col_head[None, :] == row_head[:, None],
                          jnp.repeat(bias_rows, da_heads, axis=1), NEG_BIG)
    bias0 = jnp.repeat(bias[:, 1, 0, 0], 2).reshape(2 * da_heads, 1)

    mk_p, mv_p = _mem_kv(mem_prompt, mem_norm, w_mem_bf)
    cmk = cache_mem_k.reshape(depth, db, n_mem, mxw)
    cmv = cache_mem_v.reshape(depth, db, n_mem, mxw)

    xp, xs = x_prompt, x_sample.reshape(1, db, d_model)
    outs = {n: [] for n in ("kp", "vp", "sp", "ks", "vs", "ss")}
    dims = dict(hgw=hgw, daw=daw, mxw=mxw, da_heads=da_heads)
    for l in range(depth):
        lam_init = 0.8 - 0.6 * math.exp(-0.3 * l)
        lam_rows = [v[l].reshape(1, dv // 2) for v in (da_lq1, da_lk1, da_lq2, da_lk2)]

        a = _in_proj(xp, l, norm_pre, w_in_bf, hg_lb, tm=512, emit_vt=True, **dims)
        oh, s_p = _hgrn_prompt(a, hg_norm[l], heads=hg_heads, dk=hg_dk, tt=512)
        od = _da_prompt(a, bias, da_norm[l], lam_rows, lam_init, heads=da_heads, dv=dv)
        xp = _out_proj(xp, oh, od, a, mk_p, mv_p, l, l, w_out_bf, norm_post,
                       heads=mx_heads, dh=mx_dh, tm=512)
        outs["kp"].append(a["dkf"].reshape(b, t, da_heads, dv))
        outs["vp"].append(a["dvf"].reshape(b, t, da_heads, dv))
        outs["sp"].append(s_p)

        a = _in_proj(xs, l, norm_pre, w_in_bf, hg_lb, tm=db, emit_vt=False, **dims)
        a = {n: v.reshape(db, 1, v.shape[-1]) for n, v in a.items()}
        oh, s_s = _hgrn_step(a, hg_norm[l], state_hgrn, l, heads=hg_heads, dk=hg_dk)
        od = _da_sample(a, cache_da_k, cache_da_v, page_table, l, bias_rows, bias0, da_norm[l],
                        lam_rows, lam_init, heads=da_heads, dv=dv)
        xs3 = _out_proj(xs.reshape(db, 1, d_model), oh, od, a, cmk, cmv, l, l, w_out_bf,
                        norm_post, heads=mx_heads, dh=mx_dh, tm=1)
        xs = xs3.reshape(1, db, d_model)
        outs["ks"].append(a["dkf"].reshape(db, 1, da_heads, dv))
        outs["vs"].append(a["dvf"].reshape(db, 1, da_heads, dv))
        outs["ss"].append(s_s)

    return (xp, xs.reshape(db, 1, d_model), jnp.stack(outs["kp"]), jnp.stack(outs["vp"]),
            jnp.stack(outs["sp"]), mk_p.reshape(depth, b, n_mem, mx_heads, mx_dh),
            mv_p.reshape(depth, b, n_mem, mx_heads, mx_dh), jnp.stack(outs["ks"]),
            jnp.stack(outs["vs"]), jnp.stack(outs["ss"]))
```

```python
import functools
import math

import jax
import jax.numpy as jnp
from jax import lax
from jax.experimental import pallas as pl
from jax.experimental.pallas import tpu as pltpu

F32 = jnp.float32
BF16 = jnp.bfloat16

EPS = 1e-6
MAX_DIST = 128
HG_CHUNK = 64
HG_SUB = 8
HG_UNROLL = 2
HG_GROUP = 2
HG_HEADS_PER_STEP = 2
DA_HEADS_PER_STEP = 2
LOG2E = math.log2(math.e)
ONES_ROWS = 16
DA_TQ = 256
DA_TK = 256
PAGES_PER_STEP = 16
NEG_BIG = -1e30
VMEM_LIMIT_BYTES = 56 * 1024 * 1024


def _cparams(sem):
    return pltpu.CompilerParams(dimension_semantics=sem, vmem_limit_bytes=VMEM_LIMIT_BYTES)


def _sigmoid(x):
    return 1.0 / (1.0 + jnp.exp(-x))


def _silu(x):
    return x * _sigmoid(x)


def _rms(x, w):
    return x * lax.rsqrt(jnp.mean(x * x, axis=-1, keepdims=True) + EPS) * w


def _dot(a, b):
    return jnp.dot(a, b, preferred_element_type=F32)


def _dot_nt(a, b):
    return lax.dot_general(a, b, (((1,), (1,)), ((), ())), preferred_element_type=F32)


def _dot_tn(a, b):
    return lax.dot_general(a, b, (((0,), (0,)), ((), ())), preferred_element_type=F32)


def _lam(lq1_ref, lk1_ref, lq2_ref, lk2_ref, lam_init):
    a = jnp.sum(lq1_ref[...] * lk1_ref[...], axis=-1, keepdims=True)
    b = jnp.sum(lq2_ref[...] * lk2_ref[...], axis=-1, keepdims=True)
    return jnp.exp(a) - jnp.exp(b) + lam_init


def _bias_kernel(tab_ref, out_ref, *, n_buckets, heads, tk, tq):
    kk = lax.broadcasted_iota(jnp.int32, (tk, tq), 0)
    qq = lax.broadcasted_iota(jnp.int32, (tk, tq), 1)
    max_exact = n_buckets // 2
    for jj in range(2):
        d = qq - kk + (tk if jj == 0 else 0)
        n = jnp.maximum(d, 0)
        nf = jnp.maximum(n, 1).astype(F32)
        large = max_exact + (jnp.log(nf / max_exact) / math.log(MAX_DIST / max_exact)
                             * (n_buckets - max_exact)).astype(jnp.int32)
        large = jnp.minimum(large, n_buckets - 1)
        bucket = jnp.where(n < max_exact, n, large)
        for h in range(heads):
            val = jnp.zeros((tk, tq), F32)
            for b in range(n_buckets):
                val = jnp.where(bucket == b, tab_ref[b, h], val)
            val = (val - tab_ref[n_buckets - 1, h]) * LOG2E
            out_ref[h, jj] = jnp.where(d < 0, NEG_BIG, val)


def _bias_tiles(rel_bias, heads):
    n_buckets = rel_bias.shape[0]
    return pl.pallas_call(
        functools.partial(_bias_kernel, n_buckets=n_buckets, heads=heads, tk=DA_TK, tq=DA_TQ),
        out_shape=jax.ShapeDtypeStruct((heads, 2, DA_TK, DA_TQ), F32),
        in_specs=[pl.BlockSpec(memory_space=pltpu.SMEM)],
        out_specs=pl.BlockSpec(memory_space=pltpu.VMEM),
        name="t5_bias",
    )(rel_bias)


def _memkv_kernel(mem_ref, nw_ref, w_ref, k_ref, v_ref, *, width):
    xn = _rms(mem_ref[0], nw_ref[0]).astype(BF16)
    y = _dot(xn, w_ref[0])
    k_ref[0, 0] = y[:, :width]
    v_ref[0, 0] = y[:, width:]


def _mem_kv(mem_prompt, mem_norm, w_mem_kv_bf):
    depth, d_model, two_w = w_mem_kv_bf.shape
    width = two_w // 2
    b, n_mem, _ = mem_prompt.shape
    out = jax.ShapeDtypeStruct((depth, b, n_mem, width), F32)
    return pl.pallas_call(
        functools.partial(_memkv_kernel, width=width),
        grid=(depth, b),
        out_shape=(out, out),
        in_specs=[
            pl.BlockSpec((1, n_mem, d_model), lambda l, i: (i, 0, 0)),
            pl.BlockSpec((1, 1, d_model), lambda l, i: (l, 0, 0)),
            pl.BlockSpec((1, d_model, two_w), lambda l, i: (l, 0, 0)),
        ],
        out_specs=(pl.BlockSpec((1, 1, n_mem, width), lambda l, i: (l, i, 0, 0)),
                   pl.BlockSpec((1, 1, n_mem, width), lambda l, i: (l, i, 0, 0))),
        compiler_params=_cparams(("arbitrary", "arbitrary")),
        name="mem_kv",
    )(mem_prompt, mem_norm.reshape(depth, 1, d_model), w_mem_kv_bf)


def _inproj_kernel(x_ref, nw_ref, w_ref, lbp_ref, *out_refs, layer, hgw, daw, mxw, da_heads,
                   tkb, emit_vt, col_chunk):
    if emit_vt:
        (qh_ref, kh_ref, vh_ref, lf_ref, gh_ref, dq_ref, dkf_ref, dkb_ref, dvf_ref, dvt_ref,
         dg_ref, mq_ref, mg_ref) = out_refs
    else:
        (qh_ref, kh_ref, vh_ref, lf_ref, gh_ref, dq_ref, dkf_ref, dkb_ref, dvf_ref,
         dg_ref, mq_ref, mg_ref) = out_refs
        dvt_ref = None
    xn = _rms(x_ref[0], nw_ref[0]).astype(BF16)

    p = lbp_ref[...]
    e = jnp.exp(p - jnp.max(p, axis=0, keepdims=True))
    sm = e / jnp.sum(e, axis=0, keepdims=True)
    lb = jnp.zeros((1, hgw), F32)
    for r in range(1, layer + 1):
        lb = lb + sm[r:r + 1, :]

    def seg(c0, width, fn):
        for c in range(0, width, col_chunk):
            w = min(col_chunk, width - c)
            fn(c, w, _dot(xn, w_ref[0, :, c0 + c:c0 + c + w]))

    def st(ref, fn):
        def go(c, w, y):
            ref[0, :, c:c + w] = fn(y).astype(ref.dtype)
        return go

    def forget(c, w, y):
        f = lb[:, c:c + w] + (1.0 - lb[:, c:c + w]) * _sigmoid(y)
        lf_ref[0, :, c:c + w] = jnp.log2(f)
        kh_ref[0, :, c:c + w] = (1.0 - f).astype(BF16)

    dv = daw // da_heads

    def st_f32(ref, c, w, y):
        if emit_vt:
            for hh in range(c // dv, (c + w) // dv):
                ref[0, :, hh, :] = y[:, hh * dv - c:(hh + 1) * dv - c]
        else:
            ref[0, :, c:c + w] = y

    def da_k(c, w, y):
        st_f32(dkf_ref, c, w, y)
        dkb_ref[0, :, c:c + w] = y.astype(BF16)

    def da_v(c, w, y):
        st_f32(dvf_ref, c, w, y)
        if dvt_ref is not None:
            tm = y.shape[0]
            for hh in range(c // dv, (c + w) // dv):
                for kb in range(tm // tkb):
                    blk = y[kb * tkb:(kb + 1) * tkb, hh * dv - c:(hh + 1) * dv - c]
                    dvt_ref[0, hh, kb] = blk.T.astype(BF16)

    o = 0
    seg(o, hgw, st(qh_ref, _silu)); o += hgw
    seg(o, hgw, forget); o += hgw
    seg(o, hgw, st(vh_ref, lambda y: y)); o += hgw
    seg(o, hgw, st(gh_ref, _silu)); o += hgw
    dqk = daw // da_heads // 2
    seg(o, daw, st(dq_ref, lambda y: y * (dqk ** -0.5 * LOG2E))); o += daw
    seg(o, daw, da_k); o += daw
    seg(o, daw, da_v); o += daw
    seg(o, daw, st(dg_ref, _silu)); o += daw
    seg(o, mxw, st(mq_ref, lambda y: y)); o += mxw
    seg(o, mxw, st(mg_ref, _silu)); o += mxw


def _in_proj(x, layer, norm_pre, w_in_bf, hg_lb, *, hgw, daw, mxw, da_heads, tm, emit_vt):
    b, t, d_model = x.shape
    depth, _, d_in = w_in_bf.shape
    tkb = DA_TK
    dv = daw // da_heads

    def act(width, dtype):
        return (jax.ShapeDtypeStruct((b, t, width), dtype),
                pl.BlockSpec((1, tm, width), lambda i, j: (i, j, 0)))

    def rows_f32():
        if not emit_vt:
            return act(daw, F32)
        return (jax.ShapeDtypeStruct((b, t, da_heads, dv), F32),
                pl.BlockSpec((1, tm, da_heads, dv), lambda i, j: (i, j, 0, 0)))

    outs = [act(hgw, BF16), act(hgw, BF16), act(hgw, BF16), act(hgw, F32), act(hgw, BF16),
            act(daw, BF16), rows_f32(), act(daw, BF16), rows_f32()]
    if emit_vt:
        outs.append((jax.ShapeDtypeStruct((b, da_heads, t // tkb, dv, tkb), BF16),
                     pl.BlockSpec((1, da_heads, tm // tkb, dv, tkb), lambda i, j: (i, 0, j, 0, 0))))
    outs += [act(daw, BF16), act(mxw, BF16), act(mxw, BF16)]
    kern = functools.partial(_inproj_kernel, layer=layer, hgw=hgw, daw=daw, mxw=mxw,
                             da_heads=da_heads, tkb=tkb, emit_vt=emit_vt, col_chunk=512)
    res = pl.pallas_call(
        kern,
        grid=(b, t // tm),
        out_shape=tuple(o[0] for o in outs),
        in_specs=[
            pl.BlockSpec((1, tm, d_model), lambda i, j: (i, j, 0)),
            pl.BlockSpec((1, 1, d_model), lambda i, j: (layer, 0, 0)),
            pl.BlockSpec((1, d_model, d_in), lambda i, j: (layer, 0, 0),
                         pipeline_mode=pl.Buffered(1)),
            pl.BlockSpec((depth, hgw), lambda i, j: (0, 0)),
        ],
        out_specs=tuple(o[1] for o in outs),
        compiler_params=_cparams(("arbitrary", "arbitrary")),
        name="in_proj_vt" if emit_vt else "in_proj",
    )(x, norm_pre.reshape(depth, 1, d_model), w_in_bf, hg_lb)
    names = ["qh", "kh", "vh", "lf", "gh", "dq", "dkf", "dkb", "dvf"]
    names += ["dvt"] if emit_vt else []
    names += ["dg", "mq", "mg"]
    return dict(zip(names, res))


def _hgrn_off_shape(chunk, sub):
    nsub = chunk // sub
    n_off_cols = sub * (nsub * (nsub - 1) // 2)
    return chunk - sub, n_off_cols, -(-n_off_cols // 128) * 128


def _hgrn_kernel(q_ref, k_ref, v_ref, g_ref, gate_ref, nw_ref, o_ref, s_ref, st_ref, tri_ref,
                 offm_ref, diagm_ref, cum_ref, qe_ref, dec_ref, kv_ref, oi_ref, *, chunk, sub,
                 n_chunks, unroll, group, hp, dk):
    j = pl.program_id(2)
    nsub = chunk // sub
    n_off_rows, n_off_cols, n_off_pad = _hgrn_off_shape(chunk, sub)

    @pl.when(j == 0)
    def _():
        st_ref[...] = jnp.zeros_like(st_ref)
        r_i = lax.broadcasted_iota(jnp.int32, (chunk, chunk), 0)
        c_i = lax.broadcasted_iota(jnp.int32, (chunk, chunk), 1)
        tri_ref[...] = jnp.where(c_i <= r_i, 1.0, 0.0).astype(BF16)

        ro = lax.broadcasted_iota(jnp.int32, (n_off_rows, n_off_pad), 0) // sub + 1
        co = lax.broadcasted_iota(jnp.int32, (n_off_rows, n_off_pad), 1)
        cblk = jnp.zeros_like(co)
        start = 0
        for i in range(1, nsub):
            cblk = jnp.where((co >= start) & (co < start + i * sub), i, cblk)
            start += i * sub
        offm_ref[...] = jnp.where(ro == cblk, 1.0, 0.0)

        rd = lax.broadcasted_iota(jnp.int32, (nsub * sub * sub, chunk), 0)
        cd = lax.broadcasted_iota(jnp.int32, (nsub * sub * sub, chunk), 1)
        r_is = rd // sub
        diagm_ref[...] = jnp.where((r_is == cd) & (rd % sub >= r_is % sub), 1.0, 0.0)

    g = jnp.concatenate([g_ref[0, c * chunk:(c + 1) * chunk, :] for c in range(n_chunks)], axis=1)
    g1 = g.astype(BF16)
    rem = g - g1.astype(F32)
    g2 = rem.astype(BF16)
    g3 = (rem - g2.astype(F32)).astype(BF16)
    tri = tri_ref[...]
    cum_all = _dot(tri, g1) + _dot(tri, g2) + _dot(tri, g3)
    for c in range(n_chunks * hp):
        cum_ref[c] = cum_all[:, c * dk:(c + 1) * dk]

    def stage_a(ci, hh):
        r0 = pl.multiple_of(ci * chunk, chunk)
        cols = slice(hh * dk, (hh + 1) * dk)
        item = ci * hp + hh
        q = q_ref[0, pl.ds(r0, chunk), cols].astype(F32)
        k_bf = k_ref[0, pl.ds(r0, chunk), cols]
        k = k_bf.astype(F32)
        v = v_ref[0, pl.ds(r0, chunk), cols]
        cum = cum_ref[item]

        last = cum[chunk - 1:chunk, :]
        qe_ref[item] = (q * jnp.exp2(cum)).astype(BF16)
        dec_ref[item] = jnp.exp2(last)
        kv = _dot_tn(v, (k * jnp.exp2(last - cum)).astype(BF16))

        q_parts, k_parts, v_parts = [], [], []
        v32 = v.astype(F32)
        for i in range(1, nsub):
            b_i = cum[i * sub - 1:i * sub, :]
            lo, hi = i * sub, (i + 1) * sub
            q_parts.append(q[lo:hi] * jnp.exp2(cum[lo:hi] - b_i))
            k_parts.append(k[:lo] * jnp.exp2(b_i - cum[:lo]))
            v_parts.append(v32[:lo])
        if n_off_pad > n_off_cols:
            k_parts.append(jnp.zeros((n_off_pad - n_off_cols, dk), F32))
            v_parts.append(jnp.zeros((n_off_pad - n_off_cols, v.shape[-1]), F32))
        a_off = _dot_nt(jnp.concatenate(q_parts, axis=0).astype(BF16),
                        jnp.concatenate(k_parts, axis=0).astype(BF16))
        v_off = jnp.concatenate(v_parts, axis=0).astype(BF16)

        n_parts = []
        for i in range(nsub):
            lo, hi = i * sub, (i + 1) * sub
            c_blk = cum[lo:hi]
            q_blk = q[lo:hi]
            for s in range(sub):
                dec = jnp.exp2(jnp.minimum(c_blk - c_blk[s:s + 1, :], 0.0))
                n_parts.append(q_blk * dec)
        r = _dot_nt(jnp.concatenate(n_parts, axis=0).astype(BF16), k_bf)
        return item, kv, a_off, v_off, r, v

    def stage_b(item, kv, a_off, v_off, r, v):
        kv_ref[item] = kv
        a_off = (a_off * offm_ref[...]).astype(BF16)
        o_off = _dot(a_off, v_off)
        r = r * diagm_ref[...]
        a_parts = []
        for i in range(nsub):
            base = i * sub * sub
            acc = r[base:base + sub]
            for s in range(1, sub):
                acc = acc + r[base + s * sub:base + (s + 1) * sub]
            a_parts.append(acc)
        a_diag = jnp.concatenate(a_parts, axis=0).astype(BF16)
        return item, o_off, _dot(a_diag, v)

    def stage_c(item, o_off, o_diag):
        oi_ref[item] = o_diag + jnp.concatenate([jnp.zeros((sub, dk), F32), o_off], axis=0)

    def intra(gi, carry):
        items = [(gi * group + u // hp, u % hp) for u in range(group * hp)]
        mid = [stage_a(ci, hh) for ci, hh in items]
        for x in [stage_b(*x) for x in mid]:
            stage_c(*x)
        return carry

    lax.fori_loop(0, n_chunks // group, intra, 0)

    def inter(ci, carry):
        r0 = pl.multiple_of(ci * chunk, chunk)
        for hh in range(hp):
            cols = slice(hh * dk, (hh + 1) * dk)
            item = ci * hp + hh
            s_prev = st_ref[hh]
            o = oi_ref[item] + _dot_nt(qe_ref[item], s_prev.astype(BF16))
            st_ref[hh] = s_prev * dec_ref[item] + kv_ref[item]
            gate = gate_ref[0, pl.ds(r0, chunk), cols].astype(F32)
            o_ref[0, pl.ds(r0, chunk), cols] = (_rms(o, nw_ref[...]) * gate).astype(o_ref.dtype)
        return carry

    lax.fori_loop(0, n_chunks, inter, 0, unroll=unroll)

    @pl.when(j == pl.num_programs(2) - 1)
    def _():
        for hh in range(hp):
            s_ref[0, hh] = st_ref[hh].T


def _hgrn_prompt(a, hg_norm_l, *, heads, dk, tt):
    b, t, _ = a["qh"].shape
    hp = HG_HEADS_PER_STEP
    spec = pl.BlockSpec((1, tt, hp * dk), lambda i, h, j: (i, j, h))
    chunk, sub = HG_CHUNK, HG_SUB
    n_chunks = tt // chunk
    n_off_rows, _, n_off_pad = _hgrn_off_shape(chunk, sub)
    kern = functools.partial(_hgrn_kernel, chunk=chunk, sub=sub, n_chunks=n_chunks,
                             unroll=HG_UNROLL, group=HG_GROUP, hp=hp, dk=dk)
    n_items = n_chunks * hp
    return pl.pallas_call(
        kern,
        grid=(b, heads // hp, t // tt),
        out_shape=(jax.ShapeDtypeStruct((b, t, heads * dk), BF16),
                   jax.ShapeDtypeStruct((b, heads, dk, dk), F32)),
        in_specs=[spec, spec, spec, spec, spec, pl.BlockSpec((1, dk), lambda i, h, j: (0, 0))],
        out_specs=(spec, pl.BlockSpec((1, hp, dk, dk), lambda i, h, j: (i, h, 0, 0))),
        scratch_shapes=[pltpu.VMEM((hp, dk, dk), F32), pltpu.VMEM((chunk, chunk), BF16),
                        pltpu.VMEM((n_off_rows, n_off_pad), F32),
                        pltpu.VMEM((chunk * sub, chunk), F32),
                        pltpu.VMEM((n_items, chunk, dk), F32),
                        pltpu.VMEM((n_items, chunk, dk), BF16),
                        pltpu.VMEM((n_items, 1, dk), F32),
                        pltpu.VMEM((n_items, dk, dk), F32),
                        pltpu.VMEM((n_items, chunk, dk), F32)],
        compiler_params=_cparams(("arbitrary", "arbitrary", "arbitrary")),
        name="hgrn_prompt",
    )(a["qh"], a["kh"], a["vh"], a["lf"], a["gh"], hg_norm_l.reshape(1, dk))


def _hgrn_step_kernel(q_ref, lf_ref, v_ref, gate_ref, nw_ref, s_ref, o_ref, sn_ref, *, heads, dk):
    def col(row):
        return jnp.broadcast_to(row, (dk, dk)).T

    for h in range(heads):
        sl = slice(h * dk, (h + 1) * dk)
        f = jnp.exp2(lf_ref[0, :, sl])
        s_new = col(f) * s_ref[0, h] + col(1.0 - f) * v_ref[0, :, sl].astype(F32)
        sn_ref[0, h] = s_new
        o = jnp.sum(col(q_ref[0, :, sl].astype(F32)) * s_new, axis=0, keepdims=True)
        o_ref[0, :, sl] = (_rms(o, nw_ref[...]) * gate_ref[0, :, sl].astype(F32)).astype(o_ref.dtype)


def _hgrn_step(a, hg_norm_l, state_hgrn, layer, *, heads, dk):
    db = a["qh"].shape[0]
    w = heads * dk
    row = pl.BlockSpec((1, 1, w), lambda i: (i, 0, 0))
    return pl.pallas_call(
        functools.partial(_hgrn_step_kernel, heads=heads, dk=dk),
        grid=(db,),
        out_shape=(jax.ShapeDtypeStruct((db, 1, w), BF16),
                   jax.ShapeDtypeStruct((db, heads, dk, dk), F32)),
        in_specs=[row, row, row, row, pl.BlockSpec((1, dk), lambda i: (0, 0)),
                  pl.BlockSpec((None, 1, heads, dk, dk), lambda i: (layer, i, 0, 0, 0))],
        out_specs=(row, pl.BlockSpec((1, heads, dk, dk), lambda i: (i, 0, 0, 0))),
        compiler_params=_cparams(("arbitrary",)),
        name="hgrn_step",
    )(a["qh"], a["lf"], a["vh"], a["gh"], hg_norm_l.reshape(1, dk), state_hgrn)


def _da_prompt_kernel(q_ref, k_ref, vt_ref, bias_ref, gate_ref, nw_ref, lq1_ref, lk1_ref,
                      lq2_ref, lk2_ref, o_ref, acc_ref, s0_ref, s1_ref, p0_ref, p1_ref, al0_ref,
                      al1_ref, *, tq, tk, lam_init, hp, dv):
    i = pl.program_id(2)
    half = dv // 2
    row = lax.broadcasted_iota(jnp.int32, (dv, tq), 0)

    def qbd_of(hh):
        qt = q_ref[0, :, hh * dv:(hh + 1) * dv].astype(F32).T
        return jnp.concatenate([jnp.where(row < half, qt, 0.0), jnp.where(row >= half, qt, 0.0)],
                               axis=1).astype(BF16)

    qbd = [qbd_of(hh) for hh in range(hp)]
    s_refs, p_refs, al_refs = (s0_ref, s1_ref), (p0_ref, p1_ref), (al0_ref, al1_ref)
    acc_ref[...] = jnp.zeros_like(acc_ref)
    p1_ref[...] = jnp.zeros_like(p1_ref)
    al1_ref[...] = jnp.ones_like(al1_ref)

    def scores(slot, blk):
        for hh in range(hp):
            k_blk = k_ref[0, pl.ds(pl.multiple_of(blk * tk, tk), tk), hh * dv:(hh + 1) * dv]
            s_refs[slot][hh] = _dot(k_blk, qbd[hh])

    def probs(slot, ms, bias_idx):
        out = []
        for hh in range(hp):
            s = s_refs[slot][hh]
            if bias_idx is not None:
                bias = bias_ref[hh, bias_idx]
                s = s + jnp.concatenate([bias, bias], axis=1)
            m_new = jnp.maximum(ms[hh], jnp.max(s, axis=0, keepdims=True))
            al_refs[slot][hh] = jnp.exp2(ms[hh] - m_new)
            p_refs[slot][hh] = jnp.exp2(s - m_new).astype(BF16)
            out.append(m_new)
        return tuple(out)

    def accumulate(slot, blk):
        kb = jnp.maximum(blk, 0)
        for hh in range(hp):
            vt = jnp.concatenate([vt_ref[0, hh, kb], jnp.ones((ONES_ROWS, tk), BF16)], axis=0)
            acc_ref[hh] = al_refs[slot][hh] * acc_ref[hh] + _dot(vt, p_refs[slot][hh])

    def far_pair(u, ms):
        e = 2 * u
        accumulate(1, e - 1)
        ms = probs(0, ms, None)
        scores(1, e + 1)
        accumulate(0, e)
        ms = probs(1, ms, None)
        scores(0, e + 2)
        return ms

    def tail_odd(ms):
        accumulate(1, i - 2)
        ms = probs(0, ms, 0)
        scores(1, i)
        accumulate(0, i - 1)
        ms = probs(1, ms, 1)
        accumulate(1, i)
        return ms

    def tail_even(ms):
        accumulate(1, i - 3)
        ms = probs(0, ms, None)
        scores(1, i - 1)
        accumulate(0, i - 2)
        ms = probs(1, ms, 0)
        scores(0, i)
        accumulate(1, i - 1)
        ms = probs(0, ms, 1)
        accumulate(0, i)
        return ms

    def tail_zero(ms):
        ms = probs(0, ms, 1)
        accumulate(0, i)
        return ms

    m = tuple(jnp.full((1, 2 * tq), 0.1 * NEG_BIG, F32) for _ in range(hp))
    scores(0, 0)
    m = lax.fori_loop(0, jnp.right_shift(jnp.maximum(i - 1, 0), 1), far_pair, m)
    lax.cond(i == 0, tail_zero,
             lambda ms: lax.cond((i & 1) == 1, tail_odd, tail_even, ms), m)

    lam = _lam(lq1_ref, lk1_ref, lq2_ref, lk2_ref, lam_init)
    for hh in range(hp):
        acc = acc_ref[hh, :dv, :]
        inv = 1.0 / acc_ref[hh, dv:dv + 1, :]
        out_t = acc[:, :tq] * inv[:, :tq] - lam * (acc[:, tq:] * inv[:, tq:])
        out = _rms(out_t.T, nw_ref[...]) * (1.0 - lam_init)
        cols = slice(hh * dv, (hh + 1) * dv)
        o_ref[0, :, cols] = (out * gate_ref[0, :, cols].astype(F32)).astype(o_ref.dtype)


def _da_prompt(a, bias, da_norm_l, lam_rows, lam_init, *, heads, dv):
    b, t, _ = a["dq"].shape
    tq, tk = DA_TQ, DA_TK
    hp = DA_HEADS_PER_STEP
    qspec = pl.BlockSpec((1, tq, hp * dv), lambda i, h, j: (i, j, h))
    vec = pl.BlockSpec((1, dv // 2), lambda i, h, j: (0, 0))
    kern = functools.partial(_da_prompt_kernel, tq=tq, tk=tk, lam_init=lam_init, hp=hp, dv=dv)
    return pl.pallas_call(
        kern,
        grid=(b, heads // hp, t // tq),
        out_shape=jax.ShapeDtypeStruct((b, t, heads * dv), BF16),
        in_specs=[
            qspec,
            pl.BlockSpec((1, t, hp * dv), lambda i, h, j: (i, 0, h)),
            pl.BlockSpec((1, hp, t // tk, dv, tk), lambda i, h, j: (i, h, 0, 0, 0)),
            pl.BlockSpec((hp, 2, tk, tq), lambda i, h, j: (h, 0, 0, 0)),
            qspec,
            pl.BlockSpec((1, dv), lambda i, h, j: (0, 0)),
            vec, vec, vec, vec,
        ],
        out_specs=qspec,
        scratch_shapes=[pltpu.VMEM((hp, dv + ONES_ROWS, 2 * tq), F32),
                        pltpu.VMEM((hp, tk, 2 * tq), F32), pltpu.VMEM((hp, tk, 2 * tq), F32),
                        pltpu.VMEM((hp, tk, 2 * tq), BF16), pltpu.VMEM((hp, tk, 2 * tq), BF16),
                        pltpu.VMEM((hp, 1, 2 * tq), F32), pltpu.VMEM((hp, 1, 2 * tq), F32)],
        compiler_params=_cparams(("arbitrary", "arbitrary", "arbitrary")),
        name="da_prompt",
    )(a["dq"], a["dkb"], a["dvt"], bias, a["dg"], da_norm_l.reshape(1, dv), *lam_rows)


def _da_sample_kernel(pt_ref, q_ref, kn_ref, vn_ref, gate_ref, bias_ref, bias0_ref, nw_ref,
                      lq1_ref, lk1_ref, lq2_ref, lk2_ref, *refs, heads, dv, n_pages, lam_init):
    del pt_ref
    k_refs = refs[:n_pages]
    v_refs = refs[n_pages:2 * n_pages]
    o_ref, m_ref, l_ref, acc_ref = refs[2 * n_pages:]
    j = pl.program_id(1)
    half = dv // 2
    rows = 2 * heads

    @pl.when(j == 0)
    def _():
        m_ref[...] = jnp.full_like(m_ref, 0.1 * NEG_BIG)
        l_ref[...] = jnp.zeros_like(l_ref)
        acc_ref[...] = jnp.zeros_like(acc_ref)

    def per_map(x):
        return jnp.concatenate([x[:, h * dv:(h + 1) * dv] for h in range(heads) for _ in range(2)],
                               axis=0)

    r_i = lax.broadcasted_iota(jnp.int32, (rows, dv), 0)
    c_i = lax.broadcasted_iota(jnp.int32, (rows, dv), 1)
    q8 = jnp.where(c_i // half == r_i % 2, per_map(q_ref[0].astype(F32)), 0.0).astype(BF16)

    s = jnp.concatenate([_dot_nt(q8, k_refs[g][0, 0].astype(BF16)) for g in range(n_pages)],
                        axis=1)
    s = s + bias_ref[...]
    m_old = m_ref[...]
    m_new = jnp.maximum(m_old, jnp.max(s, axis=-1, keepdims=True))
    alpha = jnp.exp2(m_old - m_new)
    p = jnp.exp2(s - m_new)
    l_ref[...] = alpha * l_ref[...] + jnp.sum(p, axis=-1, keepdims=True)
    m_ref[...] = m_new
    n_cols = k_refs[0].shape[2]
    pv = jnp.zeros((rows, dv), F32)
    for g in range(n_pages):
        pv = pv + _dot(p[:, g * n_cols:(g + 1) * n_cols].astype(BF16),
                       v_refs[g][0, 0].astype(BF16))
    acc_ref[...] = alpha * acc_ref[...] + pv

    @pl.when(j == pl.num_programs(1) - 1)
    def _():
        kn = per_map(kn_ref[0].astype(BF16).astype(F32))
        s_n = jnp.sum(q8.astype(F32) * kn, axis=-1, keepdims=True) + bias0_ref[...]
        m_o = m_ref[...]
        m_f = jnp.maximum(m_o, s_n)
        al = jnp.exp2(m_o - m_f)
        p_n = jnp.exp2(s_n - m_f)
        l_f = al * l_ref[...] + p_n
        vn = per_map(vn_ref[0].astype(BF16).astype(F32))
        z = (al * acc_ref[...] + p_n.astype(BF16).astype(F32) * vn) / l_f
        lam = _lam(lq1_ref, lk1_ref, lq2_ref, lk2_ref, lam_init)
        gate = gate_ref[0].astype(F32)
        for h in range(heads):
            sl = slice(h * dv, (h + 1) * dv)
            o_h = z[2 * h:2 * h + 1] - lam * z[2 * h + 1:2 * h + 2]
            o_h = _rms(o_h, nw_ref[...]) * (1.0 - lam_init)
            o_ref[0, :, sl] = (o_h * gate[:, sl]).astype(o_ref.dtype)


def _da_sample(a, cache_k, cache_v, page_table, layer, bias_rows, bias0, da_norm_l, lam_rows,
               lam_init, *, heads, dv):
    db = a["dq"].shape[0]
    width = heads * dv
    n_pages_seq = page_table.shape[1]
    page = cache_k.shape[2]
    g_n = PAGES_PER_STEP
    steps = n_pages_seq // g_n
    rows = 2 * heads
    row = pl.BlockSpec((1, 1, width), lambda i, j, pt: (i, 0, 0))
    vec = pl.BlockSpec((1, dv // 2), lambda i, j, pt: (0, 0))

    def page_spec(g):
        return pl.BlockSpec((1, 1, page * heads, dv),
                            lambda i, j, pt, g=g: (layer, pt[i, j * g_n + g], 0, 0))

    kern = functools.partial(_da_sample_kernel, heads=heads, dv=dv, n_pages=g_n, lam_init=lam_init)
    grid_spec = pltpu.PrefetchScalarGridSpec(
        num_scalar_prefetch=1,
        grid=(db, steps),
        in_specs=[row, row, row, row,
                  pl.BlockSpec((rows, g_n * page * heads), lambda i, j, pt: (0, j)),
                  pl.BlockSpec((rows, 1), lambda i, j, pt: (0, 0)),
                  pl.BlockSpec((1, dv), lambda i, j, pt: (0, 0)),
                  vec, vec, vec, vec]
                 + [page_spec(g) for g in range(g_n)] * 2,
        out_specs=row,
        scratch_shapes=[pltpu.VMEM((rows, 1), F32), pltpu.VMEM((rows, 1), F32),
                        pltpu.VMEM((rows, dv), F32)],
    )
    ck = cache_k.reshape(cache_k.shape[0], cache_k.shape[1], page * heads, dv)
    cv = cache_v.reshape(cache_v.shape[0], cache_v.shape[1], page * heads, dv)
    return pl.pallas_call(
        kern,
        grid_spec=grid_spec,
        out_shape=jax.ShapeDtypeStruct((db, 1, width), BF16),
        compiler_params=_cparams(("arbitrary", "arbitrary")),
        name="da_sample",
    )(page_table, a["dq"], a["dkf"], a["dvf"], a["dg"], bias_rows, bias0,
      da_norm_l.reshape(1, dv), *lam_rows, *([ck] * g_n), *([cv] * g_n))


def _out_kernel(x_ref, oh_ref, od_ref, *refs, heads, dh, hgw, daw, fused_mem):
    if fused_mem:
        mq_ref, mg_ref, mk_ref, mv_ref, w_ref, nw_ref, y_ref = refs
        mq = mq_ref[0]
        mg = mg_ref[0].astype(F32)
        parts = []
        for h in range(heads):
            sl = slice(h * dh, (h + 1) * dh)
            s = _dot_nt(mq[:, sl], mk_ref[0, :, sl].astype(BF16)) * (dh ** -0.5)
            p = jnp.exp(s - jnp.max(s, axis=-1, keepdims=True))
            o = _dot(p.astype(BF16), mv_ref[0, :, sl].astype(BF16))
            o = o / jnp.sum(p, axis=-1, keepdims=True)
            parts.append((o * mg[:, sl]).astype(BF16))
        om = jnp.concatenate(parts, axis=-1)
    else:
        om_ref, w_ref, nw_ref, y_ref = refs
        om = om_ref[0]
    y = _dot(oh_ref[0], w_ref[0, :hgw, :])
    y = y + _dot(od_ref[0], w_ref[0, hgw:hgw + daw, :])
    y = y + _dot(om, w_ref[0, hgw + daw:, :])
    y_ref[0] = x_ref[0] + _rms(y, nw_ref[0])


def _out_proj(x, oh, od, mem_args, layer, w_out_bf, norm_post, *, heads, dh, tm, fused_mem):
    b, t, d_model = x.shape
    depth, d_mix, _ = w_out_bf.shape
    hgw = oh.shape[-1]
    daw = od.shape[-1]
    mxw = heads * dh

    def act(width):
        return pl.BlockSpec((1, tm, width), lambda i, j: (i, j, 0))

    if fused_mem:
        mem = pl.BlockSpec((None, 1, mem_args[2].shape[2], mxw), lambda i, j: (layer, i, 0, 0))
        mem_specs = [act(mxw), act(mxw), mem, mem]
    else:
        mem_specs = [act(mxw)]
    return pl.pallas_call(
        functools.partial(_out_kernel, heads=heads, dh=dh, hgw=hgw, daw=daw, fused_mem=fused_mem),
        grid=(b, t // tm),
        out_shape=jax.ShapeDtypeStruct((b, t, d_model), F32),
        in_specs=[act(d_model), act(hgw), act(daw)] + mem_specs
                 + [pl.BlockSpec((1, d_mix, d_model), lambda i, j: (layer, 0, 0),
                                 pipeline_mode=pl.Buffered(1)),
                    pl.BlockSpec((1, 1, d_model), lambda i, j: (layer, 0, 0))],
        out_specs=act(d_model),
        compiler_params=_cparams(("arbitrary", "arbitrary")),
        name="out_proj" if fused_mem else "out_proj_rows",
    )(x, oh, od, *mem_args, w_out_bf, norm_post.reshape(depth, 1, d_model))


def _mem_sample_kernel(mq_ref, mg_ref, mk_ref, mv_ref, om_ref, *, heads, dh):
    pad = 8
    q = mq_ref[0]
    q_rows = jnp.concatenate([q[:, h * dh:(h + 1) * dh] for h in range(heads)]
                             + [jnp.zeros((pad - heads, dh), BF16)], axis=0)
    s = _dot_nt(q_rows, mk_ref[0].astype(BF16)) * (dh ** -0.5)
    r_i = lax.broadcasted_iota(jnp.int32, s.shape, 0)
    c_i = lax.broadcasted_iota(jnp.int32, s.shape, 1)
    s = jnp.where(c_i % heads == r_i, s, NEG_BIG)
    p = jnp.exp(s - jnp.max(s, axis=-1, keepdims=True))
    o = _dot(p.astype(BF16), mv_ref[0].astype(BF16)) / jnp.sum(p, axis=-1, keepdims=True)
    mg = mg_ref[0].astype(F32)
    for h in range(heads):
        sl = slice(h * dh, (h + 1) * dh)
        om_ref[0, :, sl] = (o[h:h + 1] * mg[:, sl]).astype(om_ref.dtype)


def _mem_sample(a, cmk, cmv, layer, *, heads, dh):
    db = a["mq"].shape[0]
    width = heads * dh
    rows = cmk.shape[2]
    row = pl.BlockSpec((1, 1, width), lambda i: (i, 0, 0))
    mem = pl.BlockSpec((None, 1, rows, dh), lambda i: (layer, i, 0, 0))
    return pl.pallas_call(
        functools.partial(_mem_sample_kernel, heads=heads, dh=dh),
        grid=(db,),
        out_shape=jax.ShapeDtypeStruct((db, 1, width), BF16),
        in_specs=[row, row, mem, mem],
        out_specs=row,
        compiler_params=_cparams(("arbitrary",)),
        name="mem_sample",
    )(a["mq"], a["mg"], cmk, cmv)


def kernel(x_prompt, x_sample, mem_prompt, cache_da_k, cache_da_v, cache_mem_k, cache_mem_v,
           state_hgrn, page_table, w_in, w_out, w_mem_kv, norm_pre, norm_post, mem_norm, hg_norm,
           da_norm, hg_lb, da_lq1, da_lk1, da_lq2, da_lk2, rel_bias):
    depth = w_in.shape[0]
    b, t, d_model = x_prompt.shape
    db = x_sample.shape[0]
    hg_heads, hg_dk = state_hgrn.shape[2], state_hgrn.shape[3]
    hgw = hg_heads * hg_dk
    da_heads, dv = cache_da_v.shape[3], cache_da_v.shape[4]
    daw = da_heads * dv
    mx_heads, mx_dh = cache_mem_k.shape[3], cache_mem_k.shape[4]
    mxw = mx_heads * mx_dh
    n_mem = mem_prompt.shape[1]
    page = cache_da_k.shape[2]
    past = page_table.shape[1] * page
    assert w_in.shape[2] == 4 * hgw + 4 * daw + 2 * mxw
    assert x_sample.shape[1] == 1 and MAX_DIST <= page <= DA_TK
    assert t % max(DA_TQ, 512) == 0 and page_table.shape[1] % PAGES_PER_STEP == 0

    w_in_bf = w_in.astype(BF16)
    w_out_bf = w_out.astype(BF16)
    w_mem_bf = w_mem_kv.astype(BF16)

    bias = _bias_tiles(rel_bias, da_heads)
    near = bias[:, 0, DA_TK - page:, 0]
    bias_rows = jnp.concatenate([jnp.zeros((da_heads, past - page), F32), near], axis=1)
    bias_rows = jnp.repeat(bias_rows, 2, axis=0)
    col_head = jnp.arange(past * da_heads, dtype=jnp.int32) % da_heads
    row_head = jnp.arange(2 * da_heads, dtype=jnp.int32) // 2
    bias_rows = jnp.where(col_head[None, :] == row_head[:, None],
                          jnp.repeat(bias_rows, da_heads, axis=1), NEG_BIG)
    bias0 = jnp.repeat(bias[:, 1, 0, 0], 2).reshape(2 * da_heads, 1)

    mk_p, mv_p = _mem_kv(mem_prompt, mem_norm, w_mem_bf)
    cmk = cache_mem_k.reshape(depth, db, n_mem * mx_heads, mx_dh)
    cmv = cache_mem_v.reshape(depth, db, n_mem * mx_heads, mx_dh)

    xp, xs = x_prompt, x_sample.reshape(1, db, d_model)
    outs = {n: [] for n in ("kp", "vp", "sp", "ks", "vs", "ss")}
    dims = dict(hgw=hgw, daw=daw, mxw=mxw, da_heads=da_heads)
    for l in range(depth):
        lam_init = 0.8 - 0.6 * math.exp(-0.3 * l)
        lam_rows = [v[l].reshape(1, dv // 2) for v in (da_lq1, da_lk1, da_lq2, da_lk2)]

        a = _in_proj(xp, l, norm_pre, w_in_bf, hg_lb, tm=512, emit_vt=True, **dims)
        oh, s_p = _hgrn_prompt(a, hg_norm[l], heads=hg_heads, dk=hg_dk, tt=512)
        od = _da_prompt(a, bias, da_norm[l], lam_rows, lam_init, heads=da_heads, dv=dv)
        xp = _out_proj(xp, oh, od, (a["mq"], a["mg"], mk_p, mv_p), l, w_out_bf, norm_post,
                       heads=mx_heads, dh=mx_dh, tm=512, fused_mem=True)
        outs["kp"].append(a["dkf"])
        outs["vp"].append(a["dvf"])
        outs["sp"].append(s_p)

        a = _in_proj(xs, l, norm_pre, w_in_bf, hg_lb, tm=db, emit_vt=False, **dims)
        a = {n: v.reshape(db, 1, v.shape[-1]) for n, v in a.items()}
        oh, s_s = _hgrn_step(a, hg_norm[l], state_hgrn, l, heads=hg_heads, dk=hg_dk)
        od = _da_sample(a, cache_da_k, cache_da_v, page_table, l, bias_rows, bias0, da_norm[l],
                        lam_rows, lam_init, heads=da_heads, dv=dv)
        om = _mem_sample(a, cmk, cmv, l, heads=mx_heads, dh=mx_dh)
        as_rows = lambda v: v.reshape(1, db, v.shape[-1])
        xs = _out_proj(xs, as_rows(oh), as_rows(od), (as_rows(om),), l, w_out_bf, norm_post,
                       heads=mx_heads, dh=mx_dh, tm=db, fused_mem=False)
        outs["ks"].append(a["dkf"].reshape(db, 1, da_heads, dv))
        outs["vs"].append(a["dvf"].reshape(db, 1, da_heads, dv))
        outs["ss"].append(s_s)

    return (xp, xs.reshape(db, 1, d_model), jnp.stack(outs["kp"]), jnp.stack(outs["vp"]),
            jnp.stack(outs["sp"]), mk_p.reshape(depth, b, n_mem, mx_heads, mx_dh),
            mv_p.reshape(depth, b, n_mem, mx_heads, mx_dh), jnp.stack(outs["ks"]),
            jnp.stack(outs["vs"]), jnp.stack(outs["ss"]))
```

```python
import functools
import math

import jax
import jax.numpy as jnp
from jax import lax
from jax.experimental import pallas as pl
from jax.experimental.pallas import tpu as pltpu

F32 = jnp.float32
BF16 = jnp.bfloat16

EPS = 1e-6
MAX_DIST = 128
HG_CHUNK = 64
HG_SUB = 8
HG_UNROLL = 2
HG_GROUP = 2
HG_HEADS_PER_STEP = 2
DA_HEADS_PER_STEP = 2
LOG2E = math.log2(math.e)
ONES_ROWS = 16
DA_TQ = 256
DA_TK = 256
PAGES_PER_STEP = 16
NEG_BIG = -1e30
VMEM_LIMIT_BYTES = 56 * 1024 * 1024


def _cparams(sem):
    return pltpu.CompilerParams(dimension_semantics=sem, vmem_limit_bytes=VMEM_LIMIT_BYTES)


def _sigmoid(x):
    return 1.0 / (1.0 + jnp.exp(-x))


def _silu(x):
    return x * _sigmoid(x)


def _rms(x, w):
    return x * lax.rsqrt(jnp.mean(x * x, axis=-1, keepdims=True) + EPS) * w


def _dot(a, b):
    return jnp.dot(a, b, preferred_element_type=F32)


def _dot_nt(a, b):
    return lax.dot_general(a, b, (((1,), (1,)), ((), ())), preferred_element_type=F32)


def _dot_tn(a, b):
    return lax.dot_general(a, b, (((0,), (0,)), ((), ())), preferred_element_type=F32)


def _lam(lq1_ref, lk1_ref, lq2_ref, lk2_ref, lam_init):
    a = jnp.sum(lq1_ref[...] * lk1_ref[...], axis=-1, keepdims=True)
    b = jnp.sum(lq2_ref[...] * lk2_ref[...], axis=-1, keepdims=True)
    return jnp.exp(a) - jnp.exp(b) + lam_init


def _bias_kernel(tab_ref, out_ref, *, n_buckets, heads, tk, tq):
    kk = lax.broadcasted_iota(jnp.int32, (tk, tq), 0)
    qq = lax.broadcasted_iota(jnp.int32, (tk, tq), 1)
    max_exact = n_buckets // 2
    for jj in range(2):
        d = qq - kk + (tk if jj == 0 else 0)
        n = jnp.maximum(d, 0)
        nf = jnp.maximum(n, 1).astype(F32)
        large = max_exact + (jnp.log(nf / max_exact) / math.log(MAX_DIST / max_exact)
                             * (n_buckets - max_exact)).astype(jnp.int32)
        large = jnp.minimum(large, n_buckets - 1)
        bucket = jnp.where(n < max_exact, n, large)
        for h in range(heads):
            val = jnp.zeros((tk, tq), F32)
            for b in range(n_buckets):
                val = jnp.where(bucket == b, tab_ref[b, h], val)
            val = (val - tab_ref[n_buckets - 1, h]) * LOG2E
            out_ref[h, jj] = jnp.where(d < 0, NEG_BIG, val)


def _bias_tiles(rel_bias, heads):
    n_buckets = rel_bias.shape[0]
    return pl.pallas_call(
        functools.partial(_bias_kernel, n_buckets=n_buckets, heads=heads, tk=DA_TK, tq=DA_TQ),
        out_shape=jax.ShapeDtypeStruct((heads, 2, DA_TK, DA_TQ), F32),
        in_specs=[pl.BlockSpec(memory_space=pltpu.SMEM)],
        out_specs=pl.BlockSpec(memory_space=pltpu.VMEM),
        name="t5_bias",
    )(rel_bias)


def _memkv_kernel(mem_ref, nw_ref, w_ref, k_ref, v_ref, *, width):
    xn = _rms(mem_ref[0], nw_ref[0]).astype(BF16)
    y = _dot(xn, w_ref[0])
    k_ref[0, 0] = y[:, :width]
    v_ref[0, 0] = y[:, width:]


def _mem_kv(mem_prompt, mem_norm, w_mem_kv_bf):
    depth, d_model, two_w = w_mem_kv_bf.shape
    width = two_w // 2
    b, n_mem, _ = mem_prompt.shape
    out = jax.ShapeDtypeStruct((depth, b, n_mem, width), F32)
    return pl.pallas_call(
        functools.partial(_memkv_kernel, width=width),
        grid=(depth, b),
        out_shape=(out, out),
        in_specs=[
            pl.BlockSpec((1, n_mem, d_model), lambda l, i: (i, 0, 0)),
            pl.BlockSpec((1, 1, d_model), lambda l, i: (l, 0, 0)),
            pl.BlockSpec((1, d_model, two_w), lambda l, i: (l, 0, 0)),
        ],
        out_specs=(pl.BlockSpec((1, 1, n_mem, width), lambda l, i: (l, i, 0, 0)),
                   pl.BlockSpec((1, 1, n_mem, width), lambda l, i: (l, i, 0, 0))),
        compiler_params=_cparams(("arbitrary", "arbitrary")),
        name="mem_kv",
    )(mem_prompt, mem_norm.reshape(depth, 1, d_model), w_mem_kv_bf)


def _inproj_kernel(x_ref, nw_ref, w_ref, lbp_ref, *out_refs, layer, hgw, daw, mxw, da_heads,
                   tkb, emit_vt, col_chunk):
    if emit_vt:
        (qh_ref, kh_ref, vh_ref, lf_ref, gh_ref, dq_ref, dkf_ref, dkb_ref, dvf_ref, dvt_ref,
         dg_ref, mq_ref, mg_ref) = out_refs
    else:
        (qh_ref, kh_ref, vh_ref, lf_ref, gh_ref, dq_ref, dkf_ref, dkb_ref, dvf_ref,
         dg_ref, mq_ref, mg_ref) = out_refs
        dvt_ref = None
    xn = _rms(x_ref[0], nw_ref[0]).astype(BF16)

    p = lbp_ref[...]
    e = jnp.exp(p - jnp.max(p, axis=0, keepdims=True))
    sm = e / jnp.sum(e, axis=0, keepdims=True)
    lb = jnp.zeros((1, hgw), F32)
    for r in range(1, layer + 1):
        lb = lb + sm[r:r + 1, :]

    def seg(c0, width, fn):
        for c in range(0, width, col_chunk):
            w = min(col_chunk, width - c)
            fn(c, w, _dot(xn, w_ref[0, :, c0 + c:c0 + c + w]))

    def st(ref, fn):
        def go(c, w, y):
            ref[0, :, c:c + w] = fn(y).astype(ref.dtype)
        return go

    def forget(c, w, y):
        f = lb[:, c:c + w] + (1.0 - lb[:, c:c + w]) * _sigmoid(y)
        lf_ref[0, :, c:c + w] = jnp.log2(f)
        kh_ref[0, :, c:c + w] = (1.0 - f).astype(BF16)

    dv = daw // da_heads

    def st_f32(ref, c, w, y):
        if emit_vt:
            for hh in range(c // dv, (c + w) // dv):
                ref[0, :, hh, :] = y[:, hh * dv - c:(hh + 1) * dv - c]
        else:
            ref[0, :, c:c + w] = y

    def da_k(c, w, y):
        st_f32(dkf_ref, c, w, y)
        dkb_ref[0, :, c:c + w] = y.astype(BF16)

    def da_v(c, w, y):
        st_f32(dvf_ref, c, w, y)
        if dvt_ref is not None:
            tm = y.shape[0]
            for hh in range(c // dv, (c + w) // dv):
                for kb in range(tm // tkb):
                    blk = y[kb * tkb:(kb + 1) * tkb, hh * dv - c:(hh + 1) * dv - c]
                    dvt_ref[0, hh, kb] = blk.T.astype(BF16)

    o = 0
    seg(o, hgw, st(qh_ref, _silu)); o += hgw
    seg(o, hgw, forget); o += hgw
    seg(o, hgw, st(vh_ref, lambda y: y)); o += hgw
    seg(o, hgw, st(gh_ref, _silu)); o += hgw
    dqk = daw // da_heads // 2
    seg(o, daw, st(dq_ref, lambda y: y * (dqk ** -0.5 * LOG2E))); o += daw
    seg(o, daw, da_k); o += daw
    seg(o, daw, da_v); o += daw
    seg(o, daw, st(dg_ref, _silu)); o += daw
    seg(o, mxw, st(mq_ref, lambda y: y)); o += mxw
    seg(o, mxw, st(mg_ref, _silu)); o += mxw


def _in_proj(x, layer, norm_pre, w_in_bf, hg_lb, *, hgw, daw, mxw, da_heads, tm, emit_vt):
    b, t, d_model = x.shape
    depth, _, d_in = w_in_bf.shape
    tkb = DA_TK
    dv = daw // da_heads

    def act(width, dtype):
        return (jax.ShapeDtypeStruct((b, t, width), dtype),
                pl.BlockSpec((1, tm, width), lambda i, j: (i, j, 0)))

    def rows_f32():
        if not emit_vt:
            return act(daw, F32)
        return (jax.ShapeDtypeStruct((b, t, da_heads, dv), F32),
                pl.BlockSpec((1, tm, da_heads, dv), lambda i, j: (i, j, 0, 0)))

    outs = [act(hgw, BF16), act(hgw, BF16), act(hgw, BF16), act(hgw, F32), act(hgw, BF16),
            act(daw, BF16), rows_f32(), act(daw, BF16), rows_f32()]
    if emit_vt:
        outs.append((jax.ShapeDtypeStruct((b, da_heads, t // tkb, dv, tkb), BF16),
                     pl.BlockSpec((1, da_heads, tm // tkb, dv, tkb), lambda i, j: (i, 0, j, 0, 0))))
    outs += [act(daw, BF16), act(mxw, BF16), act(mxw, BF16)]
    kern = functools.partial(_inproj_kernel, layer=layer, hgw=hgw, daw=daw, mxw=mxw,
                             da_heads=da_heads, tkb=tkb, emit_vt=emit_vt, col_chunk=512)
    res = pl.pallas_call(
        kern,
        grid=(b, t // tm),
        out_shape=tuple(o[0] for o in outs),
        in_specs=[
            pl.BlockSpec((1, tm, d_model), lambda i, j: (i, j, 0)),
            pl.BlockSpec((1, 1, d_model), lambda i, j: (layer, 0, 0)),
            pl.BlockSpec((1, d_model, d_in), lambda i, j: (layer, 0, 0),
                         pipeline_mode=pl.Buffered(1)),
            pl.BlockSpec((depth, hgw), lambda i, j: (0, 0)),
        ],
        out_specs=tuple(o[1] for o in outs),
        compiler_params=_cparams(("arbitrary", "arbitrary")),
        name="in_proj_vt" if emit_vt else "in_proj",
    )(x, norm_pre.reshape(depth, 1, d_model), w_in_bf, hg_lb)
    names = ["qh", "kh", "vh", "lf", "gh", "dq", "dkf", "dkb", "dvf"]
    names += ["dvt"] if emit_vt else []
    names += ["dg", "mq", "mg"]
    return dict(zip(names, res))


def _hgrn_off_shape(chunk, sub):
    nsub = chunk // sub
    n_off_cols = sub * (nsub * (nsub - 1) // 2)
    return chunk - sub, n_off_cols, -(-n_off_cols // 128) * 128


def _hgrn_kernel(q_ref, k_ref, v_ref, g_ref, gate_ref, nw_ref, o_ref, s_ref, st_ref, tri_ref,
                 offm_ref, diagm_ref, cum_ref, qe_ref, dec_ref, kv_ref, oi_ref, *, chunk, sub,
                 n_chunks, unroll, group, hp, dk):
    j = pl.program_id(2)
    nsub = chunk // sub
    n_off_rows, n_off_cols, n_off_pad = _hgrn_off_shape(chunk, sub)

    @pl.when(j == 0)
    def _():
        st_ref[...] = jnp.zeros_like(st_ref)
        r_i = lax.broadcasted_iota(jnp.int32, (chunk, chunk), 0)
        c_i = lax.broadcasted_iota(jnp.int32, (chunk, chunk), 1)
        tri_ref[...] = jnp.where(c_i <= r_i, 1.0, 0.0).astype(BF16)

        ro = lax.broadcasted_iota(jnp.int32, (n_off_rows, n_off_pad), 0) // sub + 1
        co = lax.broadcasted_iota(jnp.int32, (n_off_rows, n_off_pad), 1)
        cblk = jnp.zeros_like(co)
        start = 0
        for i in range(1, nsub):
            cblk = jnp.where((co >= start) & (co < start + i * sub), i, cblk)
            start += i * sub
        offm_ref[...] = jnp.where(ro == cblk, 1.0, 0.0)

        rd = lax.broadcasted_iota(jnp.int32, (nsub * sub * sub, chunk), 0)
        cd = lax.broadcasted_iota(jnp.int32, (nsub * sub * sub, chunk), 1)
        r_is = rd // sub
        diagm_ref[...] = jnp.where((r_is == cd) & (rd % sub >= r_is % sub), 1.0, 0.0)

    g = jnp.concatenate([g_ref[0, c * chunk:(c + 1) * chunk, :] for c in range(n_chunks)], axis=1)
    g1 = g.astype(BF16)
    rem = g - g1.astype(F32)
    g2 = rem.astype(BF16)
    g3 = (rem - g2.astype(F32)).astype(BF16)
    tri = tri_ref[...]
    cum_all = _dot(tri, g1) + _dot(tri, g2) + _dot(tri, g3)
    for c in range(n_chunks * hp):
        cum_ref[c] = cum_all[:, c * dk:(c + 1) * dk]

    def stage_a(ci, hh):
        r0 = pl.multiple_of(ci * chunk, chunk)
        cols = slice(hh * dk, (hh + 1) * dk)
        item = ci * hp + hh
        q = q_ref[0, pl.ds(r0, chunk), cols].astype(F32)
        k_bf = k_ref[0, pl.ds(r0, chunk), cols]
        k = k_bf.astype(F32)
        v = v_ref[0, pl.ds(r0, chunk), cols]
        cum = cum_ref[item]

        last = cum[chunk - 1:chunk, :]
        qe_ref[item] = (q * jnp.exp2(cum)).astype(BF16)
        dec_ref[item] = jnp.exp2(last)
        kv = _dot_tn(v, (k * jnp.exp2(last - cum)).astype(BF16))

        q_parts, k_parts, v_parts = [], [], []
        v32 = v.astype(F32)
        for i in range(1, nsub):
            b_i = cum[i * sub - 1:i * sub, :]
            lo, hi = i * sub, (i + 1) * sub
            q_parts.append(q[lo:hi] * jnp.exp2(cum[lo:hi] - b_i))
            k_parts.append(k[:lo] * jnp.exp2(b_i - cum[:lo]))
            v_parts.append(v32[:lo])
        if n_off_pad > n_off_cols:
            k_parts.append(jnp.zeros((n_off_pad - n_off_cols, dk), F32))
            v_parts.append(jnp.zeros((n_off_pad - n_off_cols, v.shape[-1]), F32))
        a_off = _dot_nt(jnp.concatenate(q_parts, axis=0).astype(BF16),
                        jnp.concatenate(k_parts, axis=0).astype(BF16))
        v_off = jnp.concatenate(v_parts, axis=0).astype(BF16)

        n_parts = []
        for i in range(nsub):
            lo, hi = i * sub, (i + 1) * sub
            c_blk = cum[lo:hi]
            q_blk = q[lo:hi]
            for s in range(sub):
                dec = jnp.exp2(jnp.minimum(c_blk - c_blk[s:s + 1, :], 0.0))
                n_parts.append(q_blk * dec)
        r = _dot_nt(jnp.concatenate(n_parts, axis=0).astype(BF16), k_bf)
        return item, kv, a_off, v_off, r, v

    def stage_b(item, kv, a_off, v_off, r, v):
        kv_ref[item] = kv
        a_off = (a_off * offm_ref[...]).astype(BF16)
        o_off = _dot(a_off, v_off)
        r = r * diagm_ref[...]
        a_parts = []
        for i in range(nsub):
            base = i * sub * sub
            acc = r[base:base + sub]
            for s in range(1, sub):
                acc = acc + r[base + s * sub:base + (s + 1) * sub]
            a_parts.append(acc)
        a_diag = jnp.concatenate(a_parts, axis=0).astype(BF16)
        return item, o_off, _dot(a_diag, v)

    def stage_c(item, o_off, o_diag):
        oi_ref[item] = o_diag + jnp.concatenate([jnp.zeros((sub, dk), F32), o_off], axis=0)

    def intra(gi, carry):
        items = [(gi * group + u // hp, u % hp) for u in range(group * hp)]
        mid = [stage_a(ci, hh) for ci, hh in items]
        for x in [stage_b(*x) for x in mid]:
            stage_c(*x)
        return carry

    lax.fori_loop(0, n_chunks // group, intra, 0)

    def inter(ci, carry):
        r0 = pl.multiple_of(ci * chunk, chunk)
        for hh in range(hp):
            cols = slice(hh * dk, (hh + 1) * dk)
            item = ci * hp + hh
            s_prev = st_ref[hh]
            o = oi_ref[item] + _dot_nt(qe_ref[item], s_prev.astype(BF16))
            st_ref[hh] = s_prev * dec_ref[item] + kv_ref[item]
            gate = gate_ref[0, pl.ds(r0, chunk), cols].astype(F32)
            o_ref[0, pl.ds(r0, chunk), cols] = (_rms(o, nw_ref[...]) * gate).astype(o_ref.dtype)
        return carry

    lax.fori_loop(0, n_chunks, inter, 0, unroll=unroll)

    @pl.when(j == pl.num_programs(2) - 1)
    def _():
        for hh in range(hp):
            s_ref[0, hh] = st_ref[hh].T


def _hgrn_prompt(a, hg_norm_l, *, heads, dk, tt, group):
    b, t, _ = a["qh"].shape
    hp = HG_HEADS_PER_STEP
    spec = pl.BlockSpec((1, tt, hp * dk), lambda i, h, j: (i, j, h))
    chunk, sub = HG_CHUNK, HG_SUB
    n_chunks = tt // chunk
    n_off_rows, _, n_off_pad = _hgrn_off_shape(chunk, sub)
    kern = functools.partial(_hgrn_kernel, chunk=chunk, sub=sub, n_chunks=n_chunks,
                             unroll=HG_UNROLL, group=group, hp=hp, dk=dk)
    n_items = n_chunks * hp
    return pl.pallas_call(
        kern,
        grid=(b, heads // hp, t // tt),
        out_shape=(jax.ShapeDtypeStruct((b, t, heads * dk), BF16),
                   jax.ShapeDtypeStruct((b, heads, dk, dk), F32)),
        in_specs=[spec, spec, spec, spec, spec, pl.BlockSpec((1, dk), lambda i, h, j: (0, 0))],
        out_specs=(spec, pl.BlockSpec((1, hp, dk, dk), lambda i, h, j: (i, h, 0, 0))),
        scratch_shapes=[pltpu.VMEM((hp, dk, dk), F32), pltpu.VMEM((chunk, chunk), BF16),
                        pltpu.VMEM((n_off_rows, n_off_pad), F32),
                        pltpu.VMEM((chunk * sub, chunk), F32),
                        pltpu.VMEM((n_items, chunk, dk), F32),
                        pltpu.VMEM((n_items, chunk, dk), BF16),
                        pltpu.VMEM((n_items, 1, dk), F32),
                        pltpu.VMEM((n_items, dk, dk), F32),
                        pltpu.VMEM((n_items, chunk, dk), F32)],
        compiler_params=_cparams(("arbitrary", "arbitrary", "arbitrary")),
        name="hgrn_prompt",
    )(a["qh"], a["kh"], a["vh"], a["lf"], a["gh"], hg_norm_l.reshape(1, dk))


def _hgrn_step_kernel(q_ref, lf_ref, v_ref, gate_ref, nw_ref, s_ref, o_ref, sn_ref, *, heads, dk):
    def col(row):
        return jnp.broadcast_to(row, (dk, dk)).T

    for h in range(heads):
        sl = slice(h * dk, (h + 1) * dk)
        f = jnp.exp2(lf_ref[0, :, sl])
        s_new = col(f) * s_ref[0, h] + col(1.0 - f) * v_ref[0, :, sl].astype(F32)
        sn_ref[0, h] = s_new
        o = jnp.sum(col(q_ref[0, :, sl].astype(F32)) * s_new, axis=0, keepdims=True)
        o_ref[0, :, sl] = (_rms(o, nw_ref[...]) * gate_ref[0, :, sl].astype(F32)).astype(o_ref.dtype)


def _hgrn_step(a, hg_norm_l, state_hgrn, layer, *, heads, dk):
    db = a["qh"].shape[0]
    w = heads * dk
    row = pl.BlockSpec((1, 1, w), lambda i: (i, 0, 0))
    return pl.pallas_call(
        functools.partial(_hgrn_step_kernel, heads=heads, dk=dk),
        grid=(db,),
        out_shape=(jax.ShapeDtypeStruct((db, 1, w), BF16),
                   jax.ShapeDtypeStruct((db, heads, dk, dk), F32)),
        in_specs=[row, row, row, row, pl.BlockSpec((1, dk), lambda i: (0, 0)),
                  pl.BlockSpec((None, 1, heads, dk, dk), lambda i: (layer, i, 0, 0, 0))],
        out_specs=(row, pl.BlockSpec((1, heads, dk, dk), lambda i: (i, 0, 0, 0))),
        compiler_params=_cparams(("arbitrary",)),
        name="hgrn_step",
    )(a["qh"], a["lf"], a["vh"], a["gh"], hg_norm_l.reshape(1, dk), state_hgrn)


def _da_prompt_kernel(q_ref, k_ref, vt_ref, bias_ref, gate_ref, nw_ref, lq1_ref, lk1_ref,
                      lq2_ref, lk2_ref, o_ref, acc_ref, s0_ref, s1_ref, p0_ref, p1_ref, al0_ref,
                      al1_ref, mb0_ref, mb1_ref, *, tq, tk, lam_init, hp, dv, merged_acc):
    mb_refs = (mb0_ref, mb1_ref)
    i = pl.program_id(2)
    half = dv // 2
    row = lax.broadcasted_iota(jnp.int32, (dv, tq), 0)

    def qbd_of(hh):
        qt = q_ref[0, :, hh * dv:(hh + 1) * dv].astype(F32).T
        return jnp.concatenate([jnp.where(row < half, qt, 0.0), jnp.where(row >= half, qt, 0.0)],
                               axis=1).astype(BF16)

    qbd = [qbd_of(hh) for hh in range(hp)]
    s_refs, p_refs, al_refs = (s0_ref, s1_ref), (p0_ref, p1_ref), (al0_ref, al1_ref)
    acc_ref[...] = jnp.zeros_like(acc_ref)
    p1_ref[...] = jnp.zeros_like(p1_ref)
    al1_ref[...] = jnp.ones_like(al1_ref)

    def scores(slot, blk):
        for hh in range(hp):
            k_blk = k_ref[0, pl.ds(pl.multiple_of(blk * tk, tk), tk), hh * dv:(hh + 1) * dv]
            s = _dot(k_blk, qbd[hh])
            s_refs[slot][hh] = s
            mb_refs[slot][hh] = jnp.max(s, axis=0, keepdims=True)

    def probs(slot, ms, bias_idx):
        out = []
        for hh in range(hp):
            s = s_refs[slot][hh]
            if bias_idx is None:
                m_blk = mb_refs[slot][hh]
            else:
                bias = bias_ref[hh, bias_idx]
                s = s + jnp.concatenate([bias, bias], axis=1)
                m_blk = jnp.max(s, axis=0, keepdims=True)
            m_new = jnp.maximum(ms[hh], m_blk)
            al_refs[slot][hh] = jnp.exp2(ms[hh] - m_new)
            p_refs[slot][hh] = jnp.exp2(s - m_new).astype(BF16)
            out.append(m_new)
        return tuple(out)

    def accumulate(slot, blk):
        kb = jnp.maximum(blk, 0)
        for hh in range(hp):
            vt = jnp.concatenate([vt_ref[0, hh, kb], jnp.ones((ONES_ROWS, tk), BF16)], axis=0)
            acc_ref[hh] = al_refs[slot][hh] * acc_ref[hh] + _dot(vt, p_refs[slot][hh])

    def accumulate_pair(blk_odd, blk_even):
        kb1, kb0 = jnp.maximum(blk_odd, 0), blk_even
        ones = jnp.ones((ONES_ROWS, tk), BF16)
        for hh in range(hp):
            pv1 = _dot(jnp.concatenate([vt_ref[0, hh, kb1], ones], axis=0), p_refs[1][hh])
            pv0 = _dot(jnp.concatenate([vt_ref[0, hh, kb0], ones], axis=0), p_refs[0][hh])
            a0 = al_refs[0][hh]
            acc_ref[hh] = (a0 * al_refs[1][hh]) * acc_ref[hh] + a0 * pv1 + pv0

    def far_pair(u, ms):
        e = 2 * u
        if merged_acc:
            ms = probs(0, ms, None)
            scores(1, e + 1)
            accumulate_pair(e - 1, e)
        else:
            accumulate(1, e - 1)
            ms = probs(0, ms, None)
            scores(1, e + 1)
            accumulate(0, e)
        ms = probs(1, ms, None)
        scores(0, e + 2)
        return ms

    def tail_odd(ms):
        accumulate(1, i - 2)
        ms = probs(0, ms, 0)
        scores(1, i)
        accumulate(0, i - 1)
        ms = probs(1, ms, 1)
        accumulate(1, i)
        return ms

    def tail_even(ms):
        accumulate(1, i - 3)
        ms = probs(0, ms, None)
        scores(1, i - 1)
        accumulate(0, i - 2)
        ms = probs(1, ms, 0)
        scores(0, i)
        accumulate(1, i - 1)
        ms = probs(0, ms, 1)
        accumulate(0, i)
        return ms

    def tail_zero(ms):
        ms = probs(0, ms, 1)
        accumulate(0, i)
        return ms

    m = tuple(jnp.full((1, 2 * tq), 0.1 * NEG_BIG, F32) for _ in range(hp))
    scores(0, 0)
    m = lax.fori_loop(0, jnp.right_shift(jnp.maximum(i - 1, 0), 1), far_pair, m)
    lax.cond(i == 0, tail_zero,
             lambda ms: lax.cond((i & 1) == 1, tail_odd, tail_even, ms), m)

    lam = _lam(lq1_ref, lk1_ref, lq2_ref, lk2_ref, lam_init)
    for hh in range(hp):
        acc = acc_ref[hh, :dv, :]
        inv = 1.0 / acc_ref[hh, dv:dv + 1, :]
        out_t = acc[:, :tq] * inv[:, :tq] - lam * (acc[:, tq:] * inv[:, tq:])
        out = _rms(out_t.T, nw_ref[...]) * (1.0 - lam_init)
        cols = slice(hh * dv, (hh + 1) * dv)
        o_ref[0, :, cols] = (out * gate_ref[0, :, cols].astype(F32)).astype(o_ref.dtype)


def _da_prompt(a, bias, da_norm_l, lam_rows, lam_init, *, heads, dv, merged_acc):
    b, t, _ = a["dq"].shape
    tq, tk = DA_TQ, DA_TK
    hp = DA_HEADS_PER_STEP
    qspec = pl.BlockSpec((1, tq, hp * dv), lambda i, h, j: (i, j, h))
    vec = pl.BlockSpec((1, dv // 2), lambda i, h, j: (0, 0))
    kern = functools.partial(_da_prompt_kernel, tq=tq, tk=tk, lam_init=lam_init, hp=hp, dv=dv,
                             merged_acc=merged_acc)
    return pl.pallas_call(
        kern,
        grid=(b, heads // hp, t // tq),
        out_shape=jax.ShapeDtypeStruct((b, t, heads * dv), BF16),
        in_specs=[
            qspec,
            pl.BlockSpec((1, t, hp * dv), lambda i, h, j: (i, 0, h)),
            pl.BlockSpec((1, hp, t // tk, dv, tk), lambda i, h, j: (i, h, 0, 0, 0)),
            pl.BlockSpec((hp, 2, tk, tq), lambda i, h, j: (h, 0, 0, 0)),
            qspec,
            pl.BlockSpec((1, dv), lambda i, h, j: (0, 0)),
            vec, vec, vec, vec,
        ],
        out_specs=qspec,
        scratch_shapes=[pltpu.VMEM((hp, dv + ONES_ROWS, 2 * tq), F32),
                        pltpu.VMEM((hp, tk, 2 * tq), F32), pltpu.VMEM((hp, tk, 2 * tq), F32),
                        pltpu.VMEM((hp, tk, 2 * tq), BF16), pltpu.VMEM((hp, tk, 2 * tq), BF16),
                        pltpu.VMEM((hp, 1, 2 * tq), F32), pltpu.VMEM((hp, 1, 2 * tq), F32),
                        pltpu.VMEM((hp, 1, 2 * tq), F32), pltpu.VMEM((hp, 1, 2 * tq), F32)],
        compiler_params=_cparams(("arbitrary", "arbitrary", "arbitrary")),
        name="da_prompt",
    )(a["dq"], a["dkb"], a["dvt"], bias, a["dg"], da_norm_l.reshape(1, dv), *lam_rows)


def _da_sample_kernel(pt_ref, q_ref, kn_ref, vn_ref, gate_ref, bias_ref, bias0_ref, nw_ref,
                      lq1_ref, lk1_ref, lq2_ref, lk2_ref, *refs, heads, dv, n_pages, lam_init):
    del pt_ref
    k_refs = refs[:n_pages]
    v_refs = refs[n_pages:2 * n_pages]
    o_ref, m_ref, l_ref, acc_ref = refs[2 * n_pages:]
    j = pl.program_id(1)
    half = dv // 2
    rows = 2 * heads

    @pl.when(j == 0)
    def _():
        m_ref[...] = jnp.full_like(m_ref, 0.1 * NEG_BIG)
        l_ref[...] = jnp.zeros_like(l_ref)
        acc_ref[...] = jnp.zeros_like(acc_ref)

    def per_map(x):
        return jnp.concatenate([x[:, h * dv:(h + 1) * dv] for h in range(heads) for _ in range(2)],
                               axis=0)

    r_i = lax.broadcasted_iota(jnp.int32, (rows, dv), 0)
    c_i = lax.broadcasted_iota(jnp.int32, (rows, dv), 1)
    q8 = jnp.where(c_i // half == r_i % 2, per_map(q_ref[0].astype(F32)), 0.0).astype(BF16)

    s = jnp.concatenate([_dot_nt(q8, k_refs[g][0, 0].astype(BF16)) for g in range(n_pages)],
                        axis=1)
    s = s + bias_ref[...]
    m_old = m_ref[...]
    m_new = jnp.maximum(m_old, jnp.max(s, axis=-1, keepdims=True))
    alpha = jnp.exp2(m_old - m_new)
    p = jnp.exp2(s - m_new)
    l_ref[...] = alpha * l_ref[...] + jnp.sum(p, axis=-1, keepdims=True)
    m_ref[...] = m_new
    n_cols = k_refs[0].shape[2]
    pv = jnp.zeros((rows, dv), F32)
    for g in range(n_pages):
        pv = pv + _dot(p[:, g * n_cols:(g + 1) * n_cols].astype(BF16),
                       v_refs[g][0, 0].astype(BF16))
    acc_ref[...] = alpha * acc_ref[...] + pv

    @pl.when(j == pl.num_programs(1) - 1)
    def _():
        kn = per_map(kn_ref[0].astype(BF16).astype(F32))
        s_n = jnp.sum(q8.astype(F32) * kn, axis=-1, keepdims=True) + bias0_ref[...]
        m_o = m_ref[...]
        m_f = jnp.maximum(m_o, s_n)
        al = jnp.exp2(m_o - m_f)
        p_n = jnp.exp2(s_n - m_f)
        l_f = al * l_ref[...] + p_n
        vn = per_map(vn_ref[0].astype(BF16).astype(F32))
        z = (al * acc_ref[...] + p_n.astype(BF16).astype(F32) * vn) / l_f
        lam = _lam(lq1_ref, lk1_ref, lq2_ref, lk2_ref, lam_init)
        gate = gate_ref[0].astype(F32)
        for h in range(heads):
            sl = slice(h * dv, (h + 1) * dv)
            o_h = z[2 * h:2 * h + 1] - lam * z[2 * h + 1:2 * h + 2]
            o_h = _rms(o_h, nw_ref[...]) * (1.0 - lam_init)
            o_ref[0, :, sl] = (o_h * gate[:, sl]).astype(o_ref.dtype)


def _da_sample(a, cache_k, cache_v, page_table, layer, bias_rows, bias0, da_norm_l, lam_rows,
               lam_init, *, heads, dv):
    db = a["dq"].shape[0]
    width = heads * dv
    n_pages_seq = page_table.shape[1]
    page = cache_k.shape[2]
    g_n = PAGES_PER_STEP
    steps = n_pages_seq // g_n
    rows = 2 * heads
    row = pl.BlockSpec((1, 1, width), lambda i, j, pt: (i, 0, 0))
    vec = pl.BlockSpec((1, dv // 2), lambda i, j, pt: (0, 0))

    def page_spec(g):
        return pl.BlockSpec((1, 1, page * heads, dv),
                            lambda i, j, pt, g=g: (layer, pt[i, j * g_n + g], 0, 0))

    kern = functools.partial(_da_sample_kernel, heads=heads, dv=dv, n_pages=g_n, lam_init=lam_init)
    grid_spec = pltpu.PrefetchScalarGridSpec(
        num_scalar_prefetch=1,
        grid=(db, steps),
        in_specs=[row, row, row, row,
                  pl.BlockSpec((rows, g_n * page * heads), lambda i, j, pt: (0, j)),
                  pl.BlockSpec((rows, 1), lambda i, j, pt: (0, 0)),
                  pl.BlockSpec((1, dv), lambda i, j, pt: (0, 0)),
                  vec, vec, vec, vec]
                 + [page_spec(g) for g in range(g_n)] * 2,
        out_specs=row,
        scratch_shapes=[pltpu.VMEM((rows, 1), F32), pltpu.VMEM((rows, 1), F32),
                        pltpu.VMEM((rows, dv), F32)],
    )
    ck = cache_k.reshape(cache_k.shape[0], cache_k.shape[1], page * heads, dv)
    cv = cache_v.reshape(cache_v.shape[0], cache_v.shape[1], page * heads, dv)
    return pl.pallas_call(
        kern,
        grid_spec=grid_spec,
        out_shape=jax.ShapeDtypeStruct((db, 1, width), BF16),
        compiler_params=_cparams(("arbitrary", "arbitrary")),
        name="da_sample",
    )(page_table, a["dq"], a["dkf"], a["dvf"], a["dg"], bias_rows, bias0,
      da_norm_l.reshape(1, dv), *lam_rows, *([ck] * g_n), *([cv] * g_n))


def _out_kernel(x_ref, oh_ref, od_ref, *refs, heads, dh, hgw, daw, fused_mem):
    if fused_mem:
        mq_ref, mg_ref, mk_ref, mv_ref, w_ref, nw_ref, y_ref = refs
        mq = mq_ref[0]
        mg = mg_ref[0].astype(F32)
        parts = []
        for h in range(heads):
            sl = slice(h * dh, (h + 1) * dh)
            s = _dot_nt(mq[:, sl], mk_ref[0, :, sl].astype(BF16)) * (dh ** -0.5)
            p = jnp.exp(s - jnp.max(s, axis=-1, keepdims=True))
            o = _dot(p.astype(BF16), mv_ref[0, :, sl].astype(BF16))
            o = o / jnp.sum(p, axis=-1, keepdims=True)
            parts.append((o * mg[:, sl]).astype(BF16))
        om = jnp.concatenate(parts, axis=-1)
    else:
        om_ref, w_ref, nw_ref, y_ref = refs
        om = om_ref[0]
    y = _dot(oh_ref[0], w_ref[0, :hgw, :])
    y = y + _dot(od_ref[0], w_ref[0, hgw:hgw + daw, :])
    y = y + _dot(om, w_ref[0, hgw + daw:, :])
    y_ref[0] = x_ref[0] + _rms(y, nw_ref[0])


def _out_proj(x, oh, od, mem_args, layer, w_out_bf, norm_post, *, heads, dh, tm, fused_mem):
    b, t, d_model = x.shape
    depth, d_mix, _ = w_out_bf.shape
    hgw = oh.shape[-1]
    daw = od.shape[-1]
    mxw = heads * dh

    def act(width):
        return pl.BlockSpec((1, tm, width), lambda i, j: (i, j, 0))

    if fused_mem:
        mem = pl.BlockSpec((None, 1, mem_args[2].shape[2], mxw), lambda i, j: (layer, i, 0, 0))
        mem_specs = [act(mxw), act(mxw), mem, mem]
    else:
        mem_specs = [act(mxw)]
    return pl.pallas_call(
        functools.partial(_out_kernel, heads=heads, dh=dh, hgw=hgw, daw=daw, fused_mem=fused_mem),
        grid=(b, t // tm),
        out_shape=jax.ShapeDtypeStruct((b, t, d_model), F32),
        in_specs=[act(d_model), act(hgw), act(daw)] + mem_specs
                 + [pl.BlockSpec((1, d_mix, d_model), lambda i, j: (layer, 0, 0),
                                 pipeline_mode=pl.Buffered(1)),
                    pl.BlockSpec((1, 1, d_model), lambda i, j: (layer, 0, 0))],
        out_specs=act(d_model),
        compiler_params=_cparams(("arbitrary", "arbitrary")),
        name="out_proj" if fused_mem else "out_proj_rows",
    )(x, oh, od, *mem_args, w_out_bf, norm_post.reshape(depth, 1, d_model))


def _mem_sample_kernel(mq_ref, mg_ref, mk_ref, mv_ref, om_ref, *, heads, dh):
    pad = 8
    q = mq_ref[0]
    q_rows = jnp.concatenate([q[:, h * dh:(h + 1) * dh] for h in range(heads)]
                             + [jnp.zeros((pad - heads, dh), BF16)], axis=0)
    s = _dot_nt(q_rows, mk_ref[0].astype(BF16)) * (dh ** -0.5)
    r_i = lax.broadcasted_iota(jnp.int32, s.shape, 0)
    c_i = lax.broadcasted_iota(jnp.int32, s.shape, 1)
    s = jnp.where(c_i % heads == r_i, s, NEG_BIG)
    p = jnp.exp(s - jnp.max(s, axis=-1, keepdims=True))
    o = _dot(p.astype(BF16), mv_ref[0].astype(BF16)) / jnp.sum(p, axis=-1, keepdims=True)
    mg = mg_ref[0].astype(F32)
    for h in range(heads):
        sl = slice(h * dh, (h + 1) * dh)
        om_ref[0, :, sl] = (o[h:h + 1] * mg[:, sl]).astype(om_ref.dtype)


def _mem_sample(a, cmk, cmv, layer, *, heads, dh):
    db = a["mq"].shape[0]
    width = heads * dh
    rows = cmk.shape[2]
    row = pl.BlockSpec((1, 1, width), lambda i: (i, 0, 0))
    mem = pl.BlockSpec((None, 1, rows, dh), lambda i: (layer, i, 0, 0))
    return pl.pallas_call(
        functools.partial(_mem_sample_kernel, heads=heads, dh=dh),
        grid=(db,),
        out_shape=jax.ShapeDtypeStruct((db, 1, width), BF16),
        in_specs=[row, row, mem, mem],
        out_specs=row,
        compiler_params=_cparams(("arbitrary",)),
        name="mem_sample",
    )(a["mq"], a["mg"], cmk, cmv)


def kernel(x_prompt, x_sample, mem_prompt, cache_da_k, cache_da_v, cache_mem_k, cache_mem_v,
           state_hgrn, page_table, w_in, w_out, w_mem_kv, norm_pre, norm_post, mem_norm, hg_norm,
           da_norm, hg_lb, da_lq1, da_lk1, da_lq2, da_lk2, rel_bias):
    depth = w_in.shape[0]
    b, t, d_model = x_prompt.shape
    db = x_sample.shape[0]
    hg_heads, hg_dk = state_hgrn.shape[2], state_hgrn.shape[3]
    hgw = hg_heads * hg_dk
    da_heads, dv = cache_da_v.shape[3], cache_da_v.shape[4]
    daw = da_heads * dv
    mx_heads, mx_dh = cache_mem_k.shape[3], cache_mem_k.shape[4]
    mxw = mx_heads * mx_dh
    n_mem = mem_prompt.shape[1]
    page = cache_da_k.shape[2]
    past = page_table.shape[1] * page
    assert w_in.shape[2] == 4 * hgw + 4 * daw + 2 * mxw
    assert x_sample.shape[1] == 1 and MAX_DIST <= page <= DA_TK
    assert t % max(DA_TQ, 512) == 0 and page_table.shape[1] % PAGES_PER_STEP == 0

    w_in_bf = w_in.astype(BF16)
    w_out_bf = w_out.astype(BF16)
    w_mem_bf = w_mem_kv.astype(BF16)

    bias = _bias_tiles(rel_bias, da_heads)
    near = bias[:, 0, DA_TK - page:, 0]
    bias_rows = jnp.concatenate([jnp.zeros((da_heads, past - page), F32), near], axis=1)
    bias_rows = jnp.repeat(bias_rows, 2, axis=0)
    col_head = jnp.arange(past * da_heads, dtype=jnp.int32) % da_heads
    row_head = jnp.arange(2 * da_heads, dtype=jnp.int32) // 2
    bias_rows = jnp.where(col_head[None, :] == row_head[:, None],
                          jnp.repeat(bias_rows, da_heads, axis=1), NEG_BIG)
    bias0 = jnp.repeat(bias[:, 1, 0, 0], 2).reshape(2 * da_heads, 1)

    mk_p, mv_p = _mem_kv(mem_prompt, mem_norm, w_mem_bf)
    cmk = cache_mem_k.reshape(depth, db, n_mem * mx_heads, mx_dh)
    cmv = cache_mem_v.reshape(depth, db, n_mem * mx_heads, mx_dh)

    xp, xs = x_prompt, x_sample.reshape(1, db, d_model)
    outs = {n: [] for n in ("kp", "vp", "sp", "ks", "vs", "ss")}
    dims = dict(hgw=hgw, daw=daw, mxw=mxw, da_heads=da_heads)
    for l in range(depth):
        lam_init = 0.8 - 0.6 * math.exp(-0.3 * l)
        lam_rows = [v[l].reshape(1, dv // 2) for v in (da_lq1, da_lk1, da_lq2, da_lk2)]

        a = _in_proj(xp, l, norm_pre, w_in_bf, hg_lb, tm=512, emit_vt=True, **dims)
        trial = l == 1
        oh, s_p = _hgrn_prompt(a, hg_norm[l], heads=hg_heads, dk=hg_dk, tt=512,
                               group=4 if trial else HG_GROUP)
        od = _da_prompt(a, bias, da_norm[l], lam_rows, lam_init, heads=da_heads, dv=dv,
                        merged_acc=trial)
        xp = _out_proj(xp, oh, od, (a["mq"], a["mg"], mk_p, mv_p), l, w_out_bf, norm_post,
                       heads=mx_heads, dh=mx_dh, tm=1024 if trial else 512, fused_mem=True)
        outs["kp"].append(a["dkf"])
        outs["vp"].append(a["dvf"])
        outs["sp"].append(s_p)

        a = _in_proj(xs, l, norm_pre, w_in_bf, hg_lb, tm=db, emit_vt=False, **dims)
        a = {n: v.reshape(db, 1, v.shape[-1]) for n, v in a.items()}
        oh, s_s = _hgrn_step(a, hg_norm[l], state_hgrn, l, heads=hg_heads, dk=hg_dk)
        od = _da_sample(a, cache_da_k, cache_da_v, page_table, l, bias_rows, bias0, da_norm[l],
                        lam_rows, lam_init, heads=da_heads, dv=dv)
        om = _mem_sample(a, cmk, cmv, l, heads=mx_heads, dh=mx_dh)
        as_rows = lambda v: v.reshape(1, db, v.shape[-1])
        xs = _out_proj(xs, as_rows(oh), as_rows(od), (as_rows(om),), l, w_out_bf, norm_post,
                       heads=mx_heads, dh=mx_dh, tm=db, fused_mem=False)
        outs["ks"].append(a["dkf"].reshape(db, 1, da_heads, dv))
        outs["vs"].append(a["dvf"].reshape(db, 1, da_heads, dv))
        outs["ss"].append(s_s)

    return (xp, xs.reshape(db, 1, d_model), jnp.stack(outs["kp"]), jnp.stack(outs["vp"]),
            jnp.stack(outs["sp"]), mk_p.reshape(depth, b, n_mem, mx_heads, mx_dh),
            mv_p.reshape(depth, b, n_mem, mx_heads, mx_dh), jnp.stack(outs["ks"]),
            jnp.stack(outs["vs"]), jnp.stack(outs["ss"]))
```

```python
import functools
import math

import jax
import jax.numpy as jnp
from jax import lax
from jax.experimental import pallas as pl
from jax.experimental.pallas import tpu as pltpu

F32 = jnp.float32
BF16 = jnp.bfloat16

EPS = 1e-6
MAX_DIST = 128
HG_CHUNK = 64
HG_SUB = 8
HG_UNROLL = 2
HG_GROUP = 4
HG_HEADS_PER_STEP = 2
DA_HEADS_PER_STEP = 2
LOG2E = math.log2(math.e)
ONES_ROWS = 16
DA_TQ = 256
DA_TK = 256
PAGES_PER_STEP = 16
NEG_BIG = -1e30
VMEM_LIMIT_BYTES = 56 * 1024 * 1024


def _cparams(sem, flags=None):
    return pltpu.CompilerParams(dimension_semantics=sem, vmem_limit_bytes=VMEM_LIMIT_BYTES,
                                flags=flags)


def _sigmoid(x):
    return 1.0 / (1.0 + jnp.exp(-x))


def _silu(x):
    return x * _sigmoid(x)


def _rms(x, w):
    return x * lax.rsqrt(jnp.mean(x * x, axis=-1, keepdims=True) + EPS) * w


def _dot(a, b):
    return jnp.dot(a, b, preferred_element_type=F32)


def _dot_nt(a, b):
    return lax.dot_general(a, b, (((1,), (1,)), ((), ())), preferred_element_type=F32)


def _dot_tn(a, b):
    return lax.dot_general(a, b, (((0,), (0,)), ((), ())), preferred_element_type=F32)


def _lam(lq1_ref, lk1_ref, lq2_ref, lk2_ref, lam_init):
    a = jnp.sum(lq1_ref[...] * lk1_ref[...], axis=-1, keepdims=True)
    b = jnp.sum(lq2_ref[...] * lk2_ref[...], axis=-1, keepdims=True)
    return jnp.exp(a) - jnp.exp(b) + lam_init


def _bias_kernel(tab_ref, out_ref, *, n_buckets, heads, tk, tq):
    kk = lax.broadcasted_iota(jnp.int32, (tk, tq), 0)
    qq = lax.broadcasted_iota(jnp.int32, (tk, tq), 1)
    max_exact = n_buckets // 2
    for jj in range(2):
        d = qq - kk + (tk if jj == 0 else 0)
        n = jnp.maximum(d, 0)
        nf = jnp.maximum(n, 1).astype(F32)
        large = max_exact + (jnp.log(nf / max_exact) / math.log(MAX_DIST / max_exact)
                             * (n_buckets - max_exact)).astype(jnp.int32)
        large = jnp.minimum(large, n_buckets - 1)
        bucket = jnp.where(n < max_exact, n, large)
        for h in range(heads):
            val = jnp.zeros((tk, tq), F32)
            for b in range(n_buckets):
                val = jnp.where(bucket == b, tab_ref[b, h], val)
            val = (val - tab_ref[n_buckets - 1, h]) * LOG2E
            out_ref[h, jj] = jnp.where(d < 0, NEG_BIG, val)


def _bias_tiles(rel_bias, heads):
    n_buckets = rel_bias.shape[0]
    return pl.pallas_call(
        functools.partial(_bias_kernel, n_buckets=n_buckets, heads=heads, tk=DA_TK, tq=DA_TQ),
        out_shape=jax.ShapeDtypeStruct((heads, 2, DA_TK, DA_TQ), F32),
        in_specs=[pl.BlockSpec(memory_space=pltpu.SMEM)],
        out_specs=pl.BlockSpec(memory_space=pltpu.VMEM),
        name="t5_bias",
    )(rel_bias)


def _memkv_kernel(mem_ref, nw_ref, w_ref, k_ref, v_ref, *, width):
    xn = _rms(mem_ref[0], nw_ref[0]).astype(BF16)
    y = _dot(xn, w_ref[0])
    k_ref[0, 0] = y[:, :width]
    v_ref[0, 0] = y[:, width:]


def _mem_kv(mem_prompt, mem_norm, w_mem_kv_bf):
    depth, d_model, two_w = w_mem_kv_bf.shape
    width = two_w // 2
    b, n_mem, _ = mem_prompt.shape
    out = jax.ShapeDtypeStruct((depth, b, n_mem, width), F32)
    return pl.pallas_call(
        functools.partial(_memkv_kernel, width=width),
        grid=(depth, b),
        out_shape=(out, out),
        in_specs=[
            pl.BlockSpec((1, n_mem, d_model), lambda l, i: (i, 0, 0)),
            pl.BlockSpec((1, 1, d_model), lambda l, i: (l, 0, 0)),
            pl.BlockSpec((1, d_model, two_w), lambda l, i: (l, 0, 0)),
        ],
        out_specs=(pl.BlockSpec((1, 1, n_mem, width), lambda l, i: (l, i, 0, 0)),
                   pl.BlockSpec((1, 1, n_mem, width), lambda l, i: (l, i, 0, 0))),
        compiler_params=_cparams(("arbitrary", "arbitrary")),
        name="mem_kv",
    )(mem_prompt, mem_norm.reshape(depth, 1, d_model), w_mem_kv_bf)


def _inproj_kernel(x_ref, nw_ref, w_ref, lbp_ref, *refs, names, n_carried, layer, hgw, daw, mxw,
                   da_heads, tkb, emit_vt, col_chunk):
    r = dict(zip(names, refs[n_carried:]))
    xn = _rms(x_ref[0], nw_ref[0]).astype(BF16)

    p = lbp_ref[...]
    e = jnp.exp(p - jnp.max(p, axis=0, keepdims=True))
    sm = e / jnp.sum(e, axis=0, keepdims=True)
    lb = jnp.zeros((1, hgw), F32)
    for row in range(1, layer + 1):
        lb = lb + sm[row:row + 1, :]

    def seg(c0, width, fn):
        for c in range(0, width, col_chunk):
            w = min(col_chunk, width - c)
            fn(c, w, _dot(xn, w_ref[0, :, c0 + c:c0 + c + w]))

    def st(ref, fn):
        def go(c, w, y):
            ref[0, :, c:c + w] = fn(y).astype(ref.dtype)
        return go

    def forget(c, w, y):
        f = lb[:, c:c + w] + (1.0 - lb[:, c:c + w]) * _sigmoid(y)
        r["lf"][0, :, c:c + w] = jnp.log2(f)
        r["kh"][0, :, c:c + w] = (1.0 - f).astype(BF16)

    dv = daw // da_heads

    def st_f32(ref, c, w, y):
        if emit_vt:
            for hh in range(c // dv, (c + w) // dv):
                ref[0, 0, :, hh, :] = y[:, hh * dv - c:(hh + 1) * dv - c]
        else:
            ref[0, :, c:c + w] = y

    def da_k(c, w, y):
        st_f32(r["dkf"], c, w, y)
        r["dkb"][0, :, c:c + w] = y.astype(BF16)

    def da_v(c, w, y):
        st_f32(r["dvf"], c, w, y)
        if emit_vt:
            tm = y.shape[0]
            for hh in range(c // dv, (c + w) // dv):
                for kb in range(tm // tkb):
                    blk = y[kb * tkb:(kb + 1) * tkb, hh * dv - c:(hh + 1) * dv - c]
                    r["dvt"][0, hh, kb] = blk.T.astype(BF16)

    o = 0
    seg(o, hgw, st(r["qh"], _silu)); o += hgw
    seg(o, hgw, forget); o += hgw
    seg(o, hgw, st(r["vh"], lambda y: y)); o += hgw
    seg(o, hgw, st(r["gh"], _silu)); o += hgw
    dqk = daw // da_heads // 2
    seg(o, daw, st(r["dq"], lambda y: y * (dqk ** -0.5 * LOG2E))); o += daw
    seg(o, daw, da_k); o += daw
    seg(o, daw, da_v); o += daw
    seg(o, daw, st(r["dg"], _silu)); o += daw
    seg(o, mxw, st(r["mq"], lambda y: y)); o += mxw
    seg(o, mxw, st(r["mg"], _silu)); o += mxw


def _in_proj(x, layer, norm_pre, w_bf, hg_lb, carried, *, hgw, daw, mxw, da_heads, tm, emit_vt):
    b, t, d_model = x.shape
    depth, _, d_in = w_bf.shape
    tkb = DA_TK
    dv = daw // da_heads

    def act(width, dtype):
        return (jax.ShapeDtypeStruct((b, t, width), dtype),
                pl.BlockSpec((1, tm, width), lambda i, j: (i, j, 0)))

    def rows_f32():
        if not emit_vt:
            return act(daw, F32)
        return (jax.ShapeDtypeStruct((depth, b, t, da_heads, dv), F32),
                pl.BlockSpec((1, 1, tm, da_heads, dv), lambda i, j: (layer, i, j, 0, 0)))

    outs = dict(qh=act(hgw, BF16), kh=act(hgw, BF16), vh=act(hgw, BF16), lf=act(hgw, F32),
                gh=act(hgw, BF16), dq=act(daw, BF16), dkf=rows_f32(), dkb=act(daw, BF16),
                dvf=rows_f32())
    if emit_vt:
        outs["dvt"] = (jax.ShapeDtypeStruct((b, da_heads, t // tkb, dv, tkb), BF16),
                       pl.BlockSpec((1, da_heads, tm // tkb, dv, tkb),
                                    lambda i, j: (i, 0, j, 0, 0)))
    outs.update(dg=act(daw, BF16), mq=act(mxw, BF16), mg=act(mxw, BF16))
    names = tuple(outs)
    in_specs = [
        pl.BlockSpec((1, tm, d_model), lambda i, j: (i, j, 0)),
        pl.BlockSpec((1, 1, d_model), lambda i, j: (layer, 0, 0)),
        pl.BlockSpec((1, d_model, d_in), lambda i, j: (layer, 0, 0), pipeline_mode=pl.Buffered(1)),
        pl.BlockSpec((depth, hgw), lambda i, j: (0, 0)),
    ] + [pl.BlockSpec(memory_space=pl.ANY)] * len(carried)
    aliases = {len(in_specs) - len(carried) + n: names.index(name)
               for n, name in enumerate(("dkf", "dvf")[:len(carried)])}
    kern = functools.partial(_inproj_kernel, names=names, n_carried=len(carried), layer=layer,
                             hgw=hgw, daw=daw, mxw=mxw, da_heads=da_heads, tkb=tkb,
                             emit_vt=emit_vt, col_chunk=512)
    res = pl.pallas_call(
        kern,
        grid=(b, t // tm),
        out_shape=tuple(outs[n][0] for n in names),
        in_specs=in_specs,
        out_specs=tuple(outs[n][1] for n in names),
        input_output_aliases=aliases,
        compiler_params=_cparams(("arbitrary", "arbitrary")),
        name="in_proj_vt" if emit_vt else "in_proj",
    )(x, norm_pre.reshape(depth, 1, d_model), w_bf, hg_lb, *carried)
    return dict(zip(names, res))


def _hgrn_off_shape(chunk, sub):
    nsub = chunk // sub
    n_off_cols = sub * (nsub * (nsub - 1) // 2)
    return chunk - sub, n_off_cols, -(-n_off_cols // 128) * 128


def _hgrn_kernel(q_ref, k_ref, v_ref, g_ref, gate_ref, nw_ref, o_ref, s_ref, st_ref, tri_ref,
                 offm_ref, diagm_ref, cum_ref, qe_ref, dec_ref, kv_ref, oi_ref, *, chunk, sub,
                 n_chunks, unroll, group, hp, dk):
    j = pl.program_id(2)
    nsub = chunk // sub
    n_off_rows, n_off_cols, n_off_pad = _hgrn_off_shape(chunk, sub)

    @pl.when(j == 0)
    def _():
        st_ref[...] = jnp.zeros_like(st_ref)
        r_i = lax.broadcasted_iota(jnp.int32, (chunk, chunk), 0)
        c_i = lax.broadcasted_iota(jnp.int32, (chunk, chunk), 1)
        tri_ref[...] = jnp.where(c_i <= r_i, 1.0, 0.0).astype(BF16)

        ro = lax.broadcasted_iota(jnp.int32, (n_off_rows, n_off_pad), 0) // sub + 1
        co = lax.broadcasted_iota(jnp.int32, (n_off_rows, n_off_pad), 1)
        cblk = jnp.zeros_like(co)
        start = 0
        for i in range(1, nsub):
            cblk = jnp.where((co >= start) & (co < start + i * sub), i, cblk)
            start += i * sub
        offm_ref[...] = jnp.where(ro == cblk, 1.0, 0.0)

        rd = lax.broadcasted_iota(jnp.int32, (nsub * sub * sub, chunk), 0)
        cd = lax.broadcasted_iota(jnp.int32, (nsub * sub * sub, chunk), 1)
        r_is = rd // sub
        diagm_ref[...] = jnp.where((r_is == cd) & (rd % sub >= r_is % sub), 1.0, 0.0)

    g = jnp.concatenate([g_ref[0, c * chunk:(c + 1) * chunk, :] for c in range(n_chunks)], axis=1)
    g1 = g.astype(BF16)
    rem = g - g1.astype(F32)
    g2 = rem.astype(BF16)
    g3 = (rem - g2.astype(F32)).astype(BF16)
    tri = tri_ref[...]
    cum_all = _dot(tri, g1) + _dot(tri, g2) + _dot(tri, g3)
    for c in range(n_chunks * hp):
        cum_ref[c] = cum_all[:, c * dk:(c + 1) * dk]

    def stage_a(ci, hh):
        r0 = pl.multiple_of(ci * chunk, chunk)
        cols = slice(hh * dk, (hh + 1) * dk)
        item = ci * hp + hh
        q = q_ref[0, pl.ds(r0, chunk), cols].astype(F32)
        k_bf = k_ref[0, pl.ds(r0, chunk), cols]
        k = k_bf.astype(F32)
        v = v_ref[0, pl.ds(r0, chunk), cols]
        cum = cum_ref[item]

        last = cum[chunk - 1:chunk, :]
        qe_ref[item] = (q * jnp.exp2(cum)).astype(BF16)
        dec_ref[item] = jnp.exp2(last)
        kv = _dot_tn(v, (k * jnp.exp2(last - cum)).astype(BF16))

        q_parts, k_parts, v_parts = [], [], []
        v32 = v.astype(F32)
        for i in range(1, nsub):
            b_i = cum[i * sub - 1:i * sub, :]
            lo, hi = i * sub, (i + 1) * sub
            q_parts.append(q[lo:hi] * jnp.exp2(cum[lo:hi] - b_i))
            k_parts.append(k[:lo] * jnp.exp2(b_i - cum[:lo]))
            v_parts.append(v32[:lo])
        if n_off_pad > n_off_cols:
            k_parts.append(jnp.zeros((n_off_pad - n_off_cols, dk), F32))
            v_parts.append(jnp.zeros((n_off_pad - n_off_cols, v.shape[-1]), F32))
        a_off = _dot_nt(jnp.concatenate(q_parts, axis=0).astype(BF16),
                        jnp.concatenate(k_parts, axis=0).astype(BF16))
        v_off = jnp.concatenate(v_parts, axis=0).astype(BF16)

        n_parts = []
        for i in range(nsub):
            lo, hi = i * sub, (i + 1) * sub
            c_blk = cum[lo:hi]
            q_blk = q[lo:hi]
            for s in range(sub):
                dec = jnp.exp2(jnp.minimum(c_blk - c_blk[s:s + 1, :], 0.0))
                n_parts.append(q_blk * dec)
        r = _dot_nt(jnp.concatenate(n_parts, axis=0).astype(BF16), k_bf)
        return item, kv, a_off, v_off, r, v

    def stage_b(item, kv, a_off, v_off, r, v):
        kv_ref[item] = kv
        a_off = (a_off * offm_ref[...]).astype(BF16)
        o_off = _dot(a_off, v_off)
        r = r * diagm_ref[...]
        a_parts = []
        for i in range(nsub):
            base = i * sub * sub
            acc = r[base:base + sub]
            for s in range(1, sub):
                acc = acc + r[base + s * sub:base + (s + 1) * sub]
            a_parts.append(acc)
        a_diag = jnp.concatenate(a_parts, axis=0).astype(BF16)
        return item, o_off, _dot(a_diag, v)

    def stage_c(item, o_off, o_diag):
        oi_ref[item] = o_diag + jnp.concatenate([jnp.zeros((sub, dk), F32), o_off], axis=0)

    def intra(gi, carry):
        items = [(gi * group + u // hp, u % hp) for u in range(group * hp)]
        mid = [stage_a(ci, hh) for ci, hh in items]
        for x in [stage_b(*x) for x in mid]:
            stage_c(*x)
        return carry

    lax.fori_loop(0, n_chunks // group, intra, 0)

    def inter(ci, carry):
        r0 = pl.multiple_of(ci * chunk, chunk)
        for hh in range(hp):
            cols = slice(hh * dk, (hh + 1) * dk)
            item = ci * hp + hh
            s_prev = st_ref[hh]
            o = oi_ref[item] + _dot_nt(qe_ref[item], s_prev.astype(BF16))
            st_ref[hh] = s_prev * dec_ref[item] + kv_ref[item]
            gate = gate_ref[0, pl.ds(r0, chunk), cols].astype(F32)
            o_ref[0, pl.ds(r0, chunk), cols] = (_rms(o, nw_ref[...]) * gate).astype(o_ref.dtype)
        return carry

    lax.fori_loop(0, n_chunks, inter, 0, unroll=unroll)

    @pl.when(j == pl.num_programs(2) - 1)
    def _():
        for hh in range(hp):
            s_ref[0, hh] = st_ref[hh].T


def _hgrn_prompt(a, hg_norm_l, *, heads, dk, tt, group):
    b, t, _ = a["qh"].shape
    hp = HG_HEADS_PER_STEP
    spec = pl.BlockSpec((1, tt, hp * dk), lambda i, h, j: (i, j, h))
    chunk, sub = HG_CHUNK, HG_SUB
    n_chunks = tt // chunk
    n_off_rows, _, n_off_pad = _hgrn_off_shape(chunk, sub)
    kern = functools.partial(_hgrn_kernel, chunk=chunk, sub=sub, n_chunks=n_chunks,
                             unroll=HG_UNROLL, group=group, hp=hp, dk=dk)
    n_items = n_chunks * hp
    return pl.pallas_call(
        kern,
        grid=(b, heads // hp, t // tt),
        out_shape=(jax.ShapeDtypeStruct((b, t, heads * dk), BF16),
                   jax.ShapeDtypeStruct((b, heads, dk, dk), F32)),
        in_specs=[spec, spec, spec, spec, spec, pl.BlockSpec((1, dk), lambda i, h, j: (0, 0))],
        out_specs=(spec, pl.BlockSpec((1, hp, dk, dk), lambda i, h, j: (i, h, 0, 0))),
        scratch_shapes=[pltpu.VMEM((hp, dk, dk), F32), pltpu.VMEM((chunk, chunk), BF16),
                        pltpu.VMEM((n_off_rows, n_off_pad), F32),
                        pltpu.VMEM((chunk * sub, chunk), F32),
                        pltpu.VMEM((n_items, chunk, dk), F32),
                        pltpu.VMEM((n_items, chunk, dk), BF16),
                        pltpu.VMEM((n_items, 1, dk), F32),
                        pltpu.VMEM((n_items, dk, dk), F32),
                        pltpu.VMEM((n_items, chunk, dk), F32)],
        compiler_params=_cparams(("arbitrary", "arbitrary", "arbitrary")),
        name="hgrn_prompt",
    )(a["qh"], a["kh"], a["vh"], a["lf"], a["gh"], hg_norm_l.reshape(1, dk))


def _hgrn_step_kernel(q_ref, lf_ref, v_ref, gate_ref, nw_ref, s_ref, *refs, heads, dk):
    o_ref, sn_ref = refs[-2:]

    def col(row):
        return jnp.broadcast_to(row, (dk, dk)).T

    for h in range(heads):
        sl = slice(h * dk, (h + 1) * dk)
        f = jnp.exp2(lf_ref[0, :, sl])
        s_new = col(f) * s_ref[0, h] + col(1.0 - f) * v_ref[0, :, sl].astype(F32)
        sn_ref[0, 0, h] = s_new
        o = jnp.sum(col(q_ref[0, :, sl].astype(F32)) * s_new, axis=0, keepdims=True)
        o_ref[0, :, sl] = (_rms(o, nw_ref[...]) * gate_ref[0, :, sl].astype(F32)).astype(o_ref.dtype)


def _hgrn_step(a, hg_norm_l, state_hgrn, carried, layer, *, heads, dk):
    db = a["qh"].shape[0]
    w = heads * dk
    depth = state_hgrn.shape[0]
    row = pl.BlockSpec((1, 1, w), lambda i: (i, 0, 0))
    state = pl.BlockSpec((1, 1, heads, dk, dk), lambda i: (layer, i, 0, 0, 0))
    in_specs = [row, row, row, row, pl.BlockSpec((1, dk), lambda i: (0, 0)),
                pl.BlockSpec((None, 1, heads, dk, dk), lambda i: (layer, i, 0, 0, 0))]
    in_specs += [pl.BlockSpec(memory_space=pl.ANY)] * len(carried)
    return pl.pallas_call(
        functools.partial(_hgrn_step_kernel, heads=heads, dk=dk),
        grid=(db,),
        out_shape=(jax.ShapeDtypeStruct((db, 1, w), BF16),
                   jax.ShapeDtypeStruct((depth, db, heads, dk, dk), F32)),
        in_specs=in_specs,
        out_specs=(row, state),
        input_output_aliases={len(in_specs) - 1: 1} if carried else {},
        compiler_params=_cparams(("arbitrary",)),
        name="hgrn_step",
    )(a["qh"], a["lf"], a["vh"], a["gh"], hg_norm_l.reshape(1, dk), state_hgrn, *carried)


def _da_prompt_kernel(q_ref, k_ref, vt_ref, bias_ref, gate_ref, nw_ref, lq1_ref, lk1_ref,
                      lq2_ref, lk2_ref, o_ref, acc_ref, s0_ref, s1_ref, p0_ref, p1_ref, al0_ref,
                      al1_ref, mb0_ref, mb1_ref, *, tq, tk, lam_init, hp, dv):
    mb_refs = (mb0_ref, mb1_ref)
    i = pl.program_id(2)
    half = dv // 2
    row = lax.broadcasted_iota(jnp.int32, (dv, tq), 0)

    def qbd_of(hh):
        qt = q_ref[0, :, hh * dv:(hh + 1) * dv].astype(F32).T
        return jnp.concatenate([jnp.where(row < half, qt, 0.0), jnp.where(row >= half, qt, 0.0)],
                               axis=1).astype(BF16)

    qbd = [qbd_of(hh) for hh in range(hp)]
    s_refs, p_refs, al_refs = (s0_ref, s1_ref), (p0_ref, p1_ref), (al0_ref, al1_ref)
    acc_ref[...] = jnp.zeros_like(acc_ref)

    def scores(slot, blk):
        for hh in range(hp):
            k_blk = k_ref[0, pl.ds(pl.multiple_of(blk * tk, tk), tk), hh * dv:(hh + 1) * dv]
            s = _dot(k_blk, qbd[hh])
            s_refs[slot][hh] = s
            mb_refs[slot][hh] = jnp.max(s, axis=0, keepdims=True)

    def probs(slot, ms, bias_idx):
        out = []
        for hh in range(hp):
            s = s_refs[slot][hh]
            if bias_idx is None:
                m_blk = mb_refs[slot][hh]
            else:
                bias = bias_ref[hh, bias_idx]
                s = s + jnp.concatenate([bias, bias], axis=1)
                m_blk = jnp.max(s, axis=0, keepdims=True)
            m_new = jnp.maximum(ms[hh], m_blk)
            al_refs[slot][hh] = jnp.exp2(ms[hh] - m_new)
            p_refs[slot][hh] = jnp.exp2(s - m_new).astype(BF16)
            out.append(m_new)
        return tuple(out)

    def accumulate(slot, blk):
        kb = jnp.maximum(blk, 0)
        for hh in range(hp):
            vt = jnp.concatenate([vt_ref[0, hh, kb], jnp.ones((ONES_ROWS, tk), BF16)], axis=0)
            acc_ref[hh] = al_refs[slot][hh] * acc_ref[hh] + _dot(vt, p_refs[slot][hh])

    m = tuple(jnp.full((1, 2 * tq), 0.1 * NEG_BIG, F32) for _ in range(hp))

    def far_pair(u, ms):
        e = 2 * u
        accumulate(1, e - 1)
        ms = probs(0, ms, None)
        scores(1, e + 1)
        accumulate(0, e)
        ms = probs(1, ms, None)
        scores(0, e + 2)
        return ms

    def tail_odd(ms):
        accumulate(1, i - 2)
        ms = probs(0, ms, 0)
        scores(1, i)
        accumulate(0, i - 1)
        ms = probs(1, ms, 1)
        accumulate(1, i)
        return ms

    def tail_even(ms):
        accumulate(1, i - 3)
        ms = probs(0, ms, None)
        scores(1, i - 1)
        accumulate(0, i - 2)
        ms = probs(1, ms, 0)
        scores(0, i)
        accumulate(1, i - 1)
        ms = probs(0, ms, 1)
        accumulate(0, i)
        return ms

    def tail_zero(ms):
        ms = probs(0, ms, 1)
        accumulate(0, i)
        return ms

    p1_ref[...] = jnp.zeros_like(p1_ref)
    al1_ref[...] = jnp.ones_like(al1_ref)
    scores(0, 0)
    m = lax.fori_loop(0, jnp.right_shift(jnp.maximum(i - 1, 0), 1), far_pair, m)
    lax.cond(i == 0, tail_zero,
             lambda ms: lax.cond((i & 1) == 1, tail_odd, tail_even, ms), m)

    lam = _lam(lq1_ref, lk1_ref, lq2_ref, lk2_ref, lam_init)
    for hh in range(hp):
        acc = acc_ref[hh, :dv, :]
        inv = 1.0 / acc_ref[hh, dv:dv + 1, :]
        out_t = acc[:, :tq] * inv[:, :tq] - lam * (acc[:, tq:] * inv[:, tq:])
        out = _rms(out_t.T, nw_ref[...]) * (1.0 - lam_init)
        cols = slice(hh * dv, (hh + 1) * dv)
        o_ref[0, :, cols] = (out * gate_ref[0, :, cols].astype(F32)).astype(o_ref.dtype)


def _da_prompt(a, bias, da_norm_l, lam_rows, lam_init, *, heads, dv):
    b, t, _ = a["dq"].shape
    tq, tk = DA_TQ, DA_TK
    hp = DA_HEADS_PER_STEP
    qspec = pl.BlockSpec((1, tq, hp * dv), lambda i, h, j: (i, j, h))
    vec = pl.BlockSpec((1, dv // 2), lambda i, h, j: (0, 0))
    kern = functools.partial(_da_prompt_kernel, tq=tq, tk=tk, lam_init=lam_init, hp=hp, dv=dv)
    return pl.pallas_call(
        kern,
        grid=(b, heads // hp, t // tq),
        out_shape=jax.ShapeDtypeStruct((b, t, heads * dv), BF16),
        in_specs=[
            qspec,
            pl.BlockSpec((1, t, hp * dv), lambda i, h, j: (i, 0, h)),
            pl.BlockSpec((1, hp, t // tk, dv, tk), lambda i, h, j: (i, h, 0, 0, 0)),
            pl.BlockSpec((hp, 2, tk, tq), lambda i, h, j: (h, 0, 0, 0)),
            qspec,
            pl.BlockSpec((1, dv), lambda i, h, j: (0, 0)),
            vec, vec, vec, vec,
        ],
        out_specs=qspec,
        scratch_shapes=[pltpu.VMEM((hp, dv + ONES_ROWS, 2 * tq), F32),
                        pltpu.VMEM((hp, tk, 2 * tq), F32), pltpu.VMEM((hp, tk, 2 * tq), F32),
                        pltpu.VMEM((hp, tk, 2 * tq), BF16), pltpu.VMEM((hp, tk, 2 * tq), BF16),
                        pltpu.VMEM((hp, 1, 2 * tq), F32), pltpu.VMEM((hp, 1, 2 * tq), F32),
                        pltpu.VMEM((hp, 1, 2 * tq), F32), pltpu.VMEM((hp, 1, 2 * tq), F32)],
        compiler_params=_cparams(("arbitrary", "arbitrary", "arbitrary")),
        name="da_prompt",
    )(a["dq"], a["dkb"], a["dvt"], bias, a["dg"], da_norm_l.reshape(1, dv), *lam_rows)


def _da_sample_kernel(pt_ref, q_ref, kn_ref, vn_ref, gate_ref, bias_ref, bias0_ref, nw_ref,
                      lq1_ref, lk1_ref, lq2_ref, lk2_ref, *refs, heads, dv, n_pages, lam_init):
    del pt_ref
    k_refs = refs[:n_pages]
    v_refs = refs[n_pages:2 * n_pages]
    o_ref, m_ref, l_ref, acc_ref = refs[2 * n_pages:]
    j = pl.program_id(1)
    half = dv // 2
    rows = 2 * heads

    @pl.when(j == 0)
    def _():
        m_ref[...] = jnp.full_like(m_ref, 0.1 * NEG_BIG)
        l_ref[...] = jnp.zeros_like(l_ref)
        acc_ref[...] = jnp.zeros_like(acc_ref)

    def per_map(x):
        return jnp.concatenate([x[:, h * dv:(h + 1) * dv] for h in range(heads) for _ in range(2)],
                               axis=0)

    r_i = lax.broadcasted_iota(jnp.int32, (rows, dv), 0)
    c_i = lax.broadcasted_iota(jnp.int32, (rows, dv), 1)
    q8 = jnp.where(c_i // half == r_i % 2, per_map(q_ref[0].astype(F32)), 0.0).astype(BF16)

    s = jnp.concatenate([_dot_nt(q8, k_refs[g][0, 0].astype(BF16)) for g in range(n_pages)],
                        axis=1)
    s = s + bias_ref[...]
    m_old = m_ref[...]
    m_new = jnp.maximum(m_old, jnp.max(s, axis=-1, keepdims=True))
    alpha = jnp.exp2(m_old - m_new)
    p = jnp.exp2(s - m_new)
    l_ref[...] = alpha * l_ref[...] + jnp.sum(p, axis=-1, keepdims=True)
    m_ref[...] = m_new
    n_cols = k_refs[0].shape[2]
    pv = jnp.zeros((rows, dv), F32)
    for g in range(n_pages):
        pv = pv + _dot(p[:, g * n_cols:(g + 1) * n_cols].astype(BF16),
                       v_refs[g][0, 0].astype(BF16))
    acc_ref[...] = alpha * acc_ref[...] + pv

    @pl.when(j == pl.num_programs(1) - 1)
    def _():
        kn = per_map(kn_ref[0].astype(BF16).astype(F32))
        s_n = jnp.sum(q8.astype(F32) * kn, axis=-1, keepdims=True) + bias0_ref[...]
        m_o = m_ref[...]
        m_f = jnp.maximum(m_o, s_n)
        al = jnp.exp2(m_o - m_f)
        p_n = jnp.exp2(s_n - m_f)
        l_f = al * l_ref[...] + p_n
        vn = per_map(vn_ref[0].astype(BF16).astype(F32))
        z = (al * acc_ref[...] + p_n.astype(BF16).astype(F32) * vn) / l_f
        lam = _lam(lq1_ref, lk1_ref, lq2_ref, lk2_ref, lam_init)
        gate = gate_ref[0].astype(F32)
        for h in range(heads):
            sl = slice(h * dv, (h + 1) * dv)
            o_h = z[2 * h:2 * h + 1] - lam * z[2 * h + 1:2 * h + 2]
            o_h = _rms(o_h, nw_ref[...]) * (1.0 - lam_init)
            o_ref[0, :, sl] = (o_h * gate[:, sl]).astype(o_ref.dtype)


def _da_sample(a, cache_k, cache_v, page_table, layer, bias_rows, bias0, da_norm_l, lam_rows,
               lam_init, *, heads, dv):
    db = a["dq"].shape[0]
    width = heads * dv
    n_pages_seq = page_table.shape[1]
    page = cache_k.shape[2]
    g_n = PAGES_PER_STEP
    steps = n_pages_seq // g_n
    rows = 2 * heads
    row = pl.BlockSpec((1, 1, width), lambda i, j, pt: (i, 0, 0))
    vec = pl.BlockSpec((1, dv // 2), lambda i, j, pt: (0, 0))

    def page_spec(g):
        return pl.BlockSpec((1, 1, page * heads, dv),
                            lambda i, j, pt, g=g: (layer, pt[i, j * g_n + g], 0, 0))

    kern = functools.partial(_da_sample_kernel, heads=heads, dv=dv, n_pages=g_n, lam_init=lam_init)
    grid_spec = pltpu.PrefetchScalarGridSpec(
        num_scalar_prefetch=1,
        grid=(db, steps),
        in_specs=[row, row, row, row,
                  pl.BlockSpec((rows, g_n * page * heads), lambda i, j, pt: (0, j)),
                  pl.BlockSpec((rows, 1), lambda i, j, pt: (0, 0)),
                  pl.BlockSpec((1, dv), lambda i, j, pt: (0, 0)),
                  vec, vec, vec, vec]
                 + [page_spec(g) for g in range(g_n)] * 2,
        out_specs=row,
        scratch_shapes=[pltpu.VMEM((rows, 1), F32), pltpu.VMEM((rows, 1), F32),
                        pltpu.VMEM((rows, dv), F32)],
    )
    ck = cache_k.reshape(cache_k.shape[0], cache_k.shape[1], page * heads, dv)
    cv = cache_v.reshape(cache_v.shape[0], cache_v.shape[1], page * heads, dv)
    return pl.pallas_call(
        kern,
        grid_spec=grid_spec,
        out_shape=jax.ShapeDtypeStruct((db, 1, width), BF16),
        compiler_params=_cparams(("arbitrary", "arbitrary")),
        name="da_sample",
    )(page_table, a["dq"], a["dkf"], a["dvf"], a["dg"], bias_rows, bias0,
      da_norm_l.reshape(1, dv), *lam_rows, *([ck] * g_n), *([cv] * g_n))


def _out_kernel(x_ref, oh_ref, od_ref, *refs, heads, dh, hgw, daw, fused_mem):
    if fused_mem:
        mq_ref, mg_ref, mk_ref, mv_ref, w_ref, nw_ref, y_ref = refs
        mq = mq_ref[0]
        mg = mg_ref[0].astype(F32)
        parts = []
        for h in range(heads):
            sl = slice(h * dh, (h + 1) * dh)
            s = _dot_nt(mq[:, sl], mk_ref[0, :, sl].astype(BF16)) * (dh ** -0.5)
            p = jnp.exp(s - jnp.max(s, axis=-1, keepdims=True))
            o = _dot(p.astype(BF16), mv_ref[0, :, sl].astype(BF16))
            o = o / jnp.sum(p, axis=-1, keepdims=True)
            parts.append((o * mg[:, sl]).astype(BF16))
        om = jnp.concatenate(parts, axis=-1)
    else:
        om_ref, w_ref, nw_ref, y_ref = refs
        om = om_ref[0]
    y = _dot(oh_ref[0], w_ref[0, :hgw, :])
    y = y + _dot(od_ref[0], w_ref[0, hgw:hgw + daw, :])
    y = y + _dot(om, w_ref[0, hgw + daw:, :])
    y_ref[0] = x_ref[0] + _rms(y, nw_ref[0])


def _out_proj(x, oh, od, mem_args, layer, w_out_bf, norm_post, *, heads, dh, tm, fused_mem):
    b, t, d_model = x.shape
    depth, d_mix, _ = w_out_bf.shape
    hgw = oh.shape[-1]
    daw = od.shape[-1]
    mxw = heads * dh

    def act(width):
        return pl.BlockSpec((1, tm, width), lambda i, j: (i, j, 0))

    if fused_mem:
        mem = pl.BlockSpec((None, 1, mem_args[2].shape[2], mxw), lambda i, j: (layer, i, 0, 0))
        mem_specs = [act(mxw), act(mxw), mem, mem]
    else:
        mem_specs = [act(mxw)]
    return pl.pallas_call(
        functools.partial(_out_kernel, heads=heads, dh=dh, hgw=hgw, daw=daw, fused_mem=fused_mem),
        grid=(b, t // tm),
        out_shape=jax.ShapeDtypeStruct((b, t, d_model), F32),
        in_specs=[act(d_model), act(hgw), act(daw)] + mem_specs
                 + [pl.BlockSpec((1, d_mix, d_model), lambda i, j: (layer, 0, 0),
                                 pipeline_mode=pl.Buffered(1)),
                    pl.BlockSpec((1, 1, d_model), lambda i, j: (layer, 0, 0))],
        out_specs=act(d_model),
        compiler_params=_cparams(("arbitrary", "arbitrary")),
        name="out_proj" if fused_mem else "out_proj_rows",
    )(x, oh, od, *mem_args, w_out_bf, norm_post.reshape(depth, 1, d_model))


def _mem_sample_kernel(mq_ref, mg_ref, mk_ref, mv_ref, om_ref, *, heads, dh):
    pad = 8
    q = mq_ref[0]
    q_rows = jnp.concatenate([q[:, h * dh:(h + 1) * dh] for h in range(heads)]
                             + [jnp.zeros((pad - heads, dh), BF16)], axis=0)
    s = _dot_nt(q_rows, mk_ref[0].astype(BF16)) * (dh ** -0.5)
    r_i = lax.broadcasted_iota(jnp.int32, s.shape, 0)
    c_i = lax.broadcasted_iota(jnp.int32, s.shape, 1)
    s = jnp.where(c_i % heads == r_i, s, NEG_BIG)
    p = jnp.exp(s - jnp.max(s, axis=-1, keepdims=True))
    o = _dot(p.astype(BF16), mv_ref[0].astype(BF16)) / jnp.sum(p, axis=-1, keepdims=True)
    mg = mg_ref[0].astype(F32)
    for h in range(heads):
        sl = slice(h * dh, (h + 1) * dh)
        om_ref[0, :, sl] = (o[h:h + 1] * mg[:, sl]).astype(om_ref.dtype)


def _mem_sample(a, cmk, cmv, layer, *, heads, dh):
    db = a["mq"].shape[0]
    width = heads * dh
    rows = cmk.shape[2]
    row = pl.BlockSpec((1, 1, width), lambda i: (i, 0, 0))
    mem = pl.BlockSpec((None, 1, rows, dh), lambda i: (layer, i, 0, 0))
    return pl.pallas_call(
        functools.partial(_mem_sample_kernel, heads=heads, dh=dh),
        grid=(db,),
        out_shape=jax.ShapeDtypeStruct((db, 1, width), BF16),
        in_specs=[row, row, mem, mem],
        out_specs=row,
        compiler_params=_cparams(("arbitrary",)),
        name="mem_sample",
    )(a["mq"], a["mg"], cmk, cmv)


def kernel(x_prompt, x_sample, mem_prompt, cache_da_k, cache_da_v, cache_mem_k, cache_mem_v,
           state_hgrn, page_table, w_in, w_out, w_mem_kv, norm_pre, norm_post, mem_norm, hg_norm,
           da_norm, hg_lb, da_lq1, da_lk1, da_lq2, da_lk2, rel_bias):
    depth = w_in.shape[0]
    b, t, d_model = x_prompt.shape
    db = x_sample.shape[0]
    hg_heads, hg_dk = state_hgrn.shape[2], state_hgrn.shape[3]
    hgw = hg_heads * hg_dk
    da_heads, dv = cache_da_v.shape[3], cache_da_v.shape[4]
    daw = da_heads * dv
    mx_heads, mx_dh = cache_mem_k.shape[3], cache_mem_k.shape[4]
    mxw = mx_heads * mx_dh
    n_mem = mem_prompt.shape[1]
    page = cache_da_k.shape[2]
    past = page_table.shape[1] * page
    assert w_in.shape[2] == 4 * hgw + 4 * daw + 2 * mxw
    assert x_sample.shape[1] == 1 and MAX_DIST <= page <= DA_TK
    assert t % max(DA_TQ, 512) == 0 and page_table.shape[1] % PAGES_PER_STEP == 0

    w_in_bf = w_in.astype(BF16)
    w_out_bf = w_out.astype(BF16)
    w_mem_bf = w_mem_kv.astype(BF16)

    bias = _bias_tiles(rel_bias, da_heads)
    near = bias[:, 0, DA_TK - page:, 0]
    bias_rows = jnp.concatenate([jnp.zeros((da_heads, past - page), F32), near], axis=1)
    bias_rows = jnp.repeat(bias_rows, 2, axis=0)
    col_head = jnp.arange(past * da_heads, dtype=jnp.int32) % da_heads
    row_head = jnp.arange(2 * da_heads, dtype=jnp.int32) // 2
    bias_rows = jnp.where(col_head[None, :] == row_head[:, None],
                          jnp.repeat(bias_rows, da_heads, axis=1), NEG_BIG)
    bias0 = jnp.repeat(bias[:, 1, 0, 0], 2).reshape(2 * da_heads, 1)

    mk_p, mv_p = _mem_kv(mem_prompt, mem_norm, w_mem_bf)
    cmk = cache_mem_k.reshape(depth, db, n_mem * mx_heads, mx_dh)
    cmv = cache_mem_v.reshape(depth, db, n_mem * mx_heads, mx_dh)

    xp, xs = x_prompt, x_sample.reshape(1, db, d_model)
    outs = {n: [] for n in ("sp", "ks", "vs")}
    kv_all = ()
    s_s = ()
    dims = dict(hgw=hgw, daw=daw, mxw=mxw, da_heads=da_heads)
    for l in range(depth):
        lam_init = 0.8 - 0.6 * math.exp(-0.3 * l)
        lam_rows = [v[l].reshape(1, dv // 2) for v in (da_lq1, da_lk1, da_lq2, da_lk2)]

        trial = l == 1
        a = _in_proj(xp, l, norm_pre, w_in_bf, hg_lb, kv_all, tm=512, emit_vt=True, **dims)
        kv_all = (a["dkf"], a["dvf"])
        oh, s_p = _hgrn_prompt(a, hg_norm[l], heads=hg_heads, dk=hg_dk, tt=512,
                               group=8 if trial else HG_GROUP)
        od = _da_prompt(a, bias, da_norm[l], lam_rows, lam_init, heads=da_heads, dv=dv)
        xp = _out_proj(xp, oh, od, (a["mq"], a["mg"], mk_p, mv_p), l, w_out_bf, norm_post,
                       heads=mx_heads, dh=mx_dh, tm=1024, fused_mem=True)
        outs["sp"].append(s_p)

        a = _in_proj(xs, l, norm_pre, w_in_bf, hg_lb, (), tm=db, emit_vt=False, **dims)
        a = {n: v.reshape(db, 1, v.shape[-1]) for n, v in a.items()}
        oh, s_s = _hgrn_step(a, hg_norm[l], state_hgrn, s_s, l, heads=hg_heads, dk=hg_dk)
        s_s = (s_s,)
        od = _da_sample(a, cache_da_k, cache_da_v, page_table, l, bias_rows, bias0, da_norm[l],
                        lam_rows, lam_init, heads=da_heads, dv=dv)
        om = _mem_sample(a, cmk, cmv, l, heads=mx_heads, dh=mx_dh)
        as_rows = lambda v: v.reshape(1, db, v.shape[-1])
        xs = _out_proj(xs, as_rows(oh), as_rows(od), (as_rows(om),), l, w_out_bf, norm_post,
                       heads=mx_heads, dh=mx_dh, tm=db, fused_mem=False)
        outs["ks"].append(a["dkf"].reshape(db, 1, da_heads, dv))
        outs["vs"].append(a["dvf"].reshape(db, 1, da_heads, dv))

    return (xp, xs.reshape(db, 1, d_model), kv_all[0], kv_all[1],
            jnp.stack(outs["sp"]), mk_p.reshape(depth, b, n_mem, mx_heads, mx_dh),
            mv_p.reshape(depth, b, n_mem, mx_heads, mx_dh), jnp.stack(outs["ks"]),
            jnp.stack(outs["vs"]), s_s[0])
```

```python
import functools
import math

import jax
import jax.numpy as jnp
from jax import lax
from jax.experimental import pallas as pl
from jax.experimental.pallas import tpu as pltpu

F32 = jnp.float32
BF16 = jnp.bfloat16

EPS = 1e-6
MAX_DIST = 128
HG_CHUNK = 64
HG_SUB = 8
HG_UNROLL = 8
HG_GROUP = 8
HG_HEADS_PER_STEP = 2
DA_HEADS_PER_STEP = 2
LOG2E = math.log2(math.e)
ONES_ROWS = 16
DA_TQ = 256
DA_TK = 256
PAGES_PER_STEP = 16
NEG_BIG = -1e30
VMEM_LIMIT_BYTES = 56 * 1024 * 1024


def _cparams(sem, flags=None):
    return pltpu.CompilerParams(dimension_semantics=sem, vmem_limit_bytes=VMEM_LIMIT_BYTES,
                                flags=flags)


def _sigmoid(x):
    return 1.0 / (1.0 + jnp.exp(-x))


def _silu(x):
    return x * _sigmoid(x)


def _rms(x, w):
    return x * lax.rsqrt(jnp.mean(x * x, axis=-1, keepdims=True) + EPS) * w


def _dot(a, b):
    return jnp.dot(a, b, preferred_element_type=F32)


def _dot_nt(a, b):
    return lax.dot_general(a, b, (((1,), (1,)), ((), ())), preferred_element_type=F32)


def _dot_tn(a, b):
    return lax.dot_general(a, b, (((0,), (0,)), ((), ())), preferred_element_type=F32)


def _lam(lq1_ref, lk1_ref, lq2_ref, lk2_ref, lam_init):
    a = jnp.sum(lq1_ref[...] * lk1_ref[...], axis=-1, keepdims=True)
    b = jnp.sum(lq2_ref[...] * lk2_ref[...], axis=-1, keepdims=True)
    return jnp.exp(a) - jnp.exp(b) + lam_init


def _bias_kernel(tab_ref, out_ref, *, n_buckets, heads, tk, tq):
    kk = lax.broadcasted_iota(jnp.int32, (tk, tq), 0)
    qq = lax.broadcasted_iota(jnp.int32, (tk, tq), 1)
    max_exact = n_buckets // 2
    for jj in range(2):
        d = qq - kk + (tk if jj == 0 else 0)
        n = jnp.maximum(d, 0)
        nf = jnp.maximum(n, 1).astype(F32)
        large = max_exact + (jnp.log(nf / max_exact) / math.log(MAX_DIST / max_exact)
                             * (n_buckets - max_exact)).astype(jnp.int32)
        large = jnp.minimum(large, n_buckets - 1)
        bucket = jnp.where(n < max_exact, n, large)
        for h in range(heads):
            val = jnp.zeros((tk, tq), F32)
            for b in range(n_buckets):
                val = jnp.where(bucket == b, tab_ref[b, h], val)
            val = (val - tab_ref[n_buckets - 1, h]) * LOG2E
            out_ref[h, jj] = jnp.where(d < 0, NEG_BIG, val)


def _bias_tiles(rel_bias, heads):
    n_buckets = rel_bias.shape[0]
    return pl.pallas_call(
        functools.partial(_bias_kernel, n_buckets=n_buckets, heads=heads, tk=DA_TK, tq=DA_TQ),
        out_shape=jax.ShapeDtypeStruct((heads, 2, DA_TK, DA_TQ), F32),
        in_specs=[pl.BlockSpec(memory_space=pltpu.SMEM)],
        out_specs=pl.BlockSpec(memory_space=pltpu.VMEM),
        name="t5_bias",
    )(rel_bias)


def _memkv_kernel(mem_ref, nw_ref, w_ref, k_ref, v_ref, *, width):
    xn = _rms(mem_ref[0], nw_ref[0]).astype(BF16)
    y = _dot(xn, w_ref[0])
    k_ref[0, 0] = y[:, :width]
    v_ref[0, 0] = y[:, width:]


def _mem_kv(mem_prompt, mem_norm, w_mem_kv_bf):
    depth, d_model, two_w = w_mem_kv_bf.shape
    width = two_w // 2
    b, n_mem, _ = mem_prompt.shape
    out = jax.ShapeDtypeStruct((depth, b, n_mem, width), F32)
    return pl.pallas_call(
        functools.partial(_memkv_kernel, width=width),
        grid=(depth, b),
        out_shape=(out, out),
        in_specs=[
            pl.BlockSpec((1, n_mem, d_model), lambda l, i: (i, 0, 0)),
            pl.BlockSpec((1, 1, d_model), lambda l, i: (l, 0, 0)),
            pl.BlockSpec((1, d_model, two_w), lambda l, i: (l, 0, 0)),
        ],
        out_specs=(pl.BlockSpec((1, 1, n_mem, width), lambda l, i: (l, i, 0, 0)),
                   pl.BlockSpec((1, 1, n_mem, width), lambda l, i: (l, i, 0, 0))),
        compiler_params=_cparams(("arbitrary", "arbitrary")),
        name="mem_kv",
    )(mem_prompt, mem_norm.reshape(depth, 1, d_model), w_mem_kv_bf)


def _inproj_kernel(x_ref, nw_ref, w_ref, lbp_ref, *refs, names, n_carried, layer, hgw, daw, mxw,
                   da_heads, tkb, emit_vt, col_chunk):
    r = dict(zip(names, refs[n_carried:]))
    xn = _rms(x_ref[0], nw_ref[0]).astype(BF16)

    p = lbp_ref[...]
    e = jnp.exp(p - jnp.max(p, axis=0, keepdims=True))
    sm = e / jnp.sum(e, axis=0, keepdims=True)
    lb = jnp.zeros((1, hgw), F32)
    for row in range(1, layer + 1):
        lb = lb + sm[row:row + 1, :]

    def seg(c0, width, fn):
        for c in range(0, width, col_chunk):
            w = min(col_chunk, width - c)
            fn(c, w, _dot(xn, w_ref[0, :, c0 + c:c0 + c + w]))

    def st(ref, fn):
        def go(c, w, y):
            ref[0, :, c:c + w] = fn(y).astype(ref.dtype)
        return go

    def forget(c, w, y):
        f = lb[:, c:c + w] + (1.0 - lb[:, c:c + w]) * _sigmoid(y)
        r["lf"][0, :, c:c + w] = jnp.log2(f)
        r["kh"][0, :, c:c + w] = (1.0 - f).astype(BF16)

    dv = daw // da_heads

    def st_f32(ref, c, w, y):
        if emit_vt:
            for hh in range(c // dv, (c + w) // dv):
                ref[0, 0, :, hh, :] = y[:, hh * dv - c:(hh + 1) * dv - c]
        else:
            ref[0, :, c:c + w] = y

    def da_k(c, w, y):
        st_f32(r["dkf"], c, w, y)
        r["dkb"][0, :, c:c + w] = y.astype(BF16)

    def da_v(c, w, y):
        st_f32(r["dvf"], c, w, y)
        if emit_vt:
            tm = y.shape[0]
            for hh in range(c // dv, (c + w) // dv):
                for kb in range(tm // tkb):
                    blk = y[kb * tkb:(kb + 1) * tkb, hh * dv - c:(hh + 1) * dv - c]
                    r["dvt"][0, hh, kb] = blk.T.astype(BF16)

    o = 0
    seg(o, hgw, st(r["qh"], _silu)); o += hgw
    seg(o, hgw, forget); o += hgw
    seg(o, hgw, st(r["vh"], lambda y: y)); o += hgw
    seg(o, hgw, st(r["gh"], _silu)); o += hgw
    dqk = daw // da_heads // 2
    seg(o, daw, st(r["dq"], lambda y: y * (dqk ** -0.5 * LOG2E))); o += daw
    seg(o, daw, da_k); o += daw
    seg(o, daw, da_v); o += daw
    seg(o, daw, st(r["dg"], _silu)); o += daw
    seg(o, mxw, st(r["mq"], lambda y: y)); o += mxw
    seg(o, mxw, st(r["mg"], _silu)); o += mxw


def _in_proj(x, layer, norm_pre, w_bf, hg_lb, carried, *, hgw, daw, mxw, da_heads, tm, emit_vt):
    b, t, d_model = x.shape
    depth, _, d_in = w_bf.shape
    tkb = DA_TK
    dv = daw // da_heads

    def act(width, dtype):
        return (jax.ShapeDtypeStruct((b, t, width), dtype),
                pl.BlockSpec((1, tm, width), lambda i, j: (i, j, 0)))

    def rows_f32():
        if not emit_vt:
            return act(daw, F32)
        return (jax.ShapeDtypeStruct((depth, b, t, da_heads, dv), F32),
                pl.BlockSpec((1, 1, tm, da_heads, dv), lambda i, j: (layer, i, j, 0, 0)))

    outs = dict(qh=act(hgw, BF16), kh=act(hgw, BF16), vh=act(hgw, BF16), lf=act(hgw, F32),
                gh=act(hgw, BF16), dq=act(daw, BF16), dkf=rows_f32(), dkb=act(daw, BF16),
                dvf=rows_f32())
    if emit_vt:
        outs["dvt"] = (jax.ShapeDtypeStruct((b, da_heads, t // tkb, dv, tkb), BF16),
                       pl.BlockSpec((1, da_heads, tm // tkb, dv, tkb),
                                    lambda i, j: (i, 0, j, 0, 0)))
    outs.update(dg=act(daw, BF16), mq=act(mxw, BF16), mg=act(mxw, BF16))
    names = tuple(outs)
    in_specs = [
        pl.BlockSpec((1, tm, d_model), lambda i, j: (i, j, 0)),
        pl.BlockSpec((1, 1, d_model), lambda i, j: (layer, 0, 0)),
        pl.BlockSpec((1, d_model, d_in), lambda i, j: (layer, 0, 0), pipeline_mode=pl.Buffered(1)),
        pl.BlockSpec((depth, hgw), lambda i, j: (0, 0)),
    ] + [pl.BlockSpec(memory_space=pl.ANY)] * len(carried)
    aliases = {len(in_specs) - len(carried) + n: names.index(name)
               for n, name in enumerate(("dkf", "dvf")[:len(carried)])}
    kern = functools.partial(_inproj_kernel, names=names, n_carried=len(carried), layer=layer,
                             hgw=hgw, daw=daw, mxw=mxw, da_heads=da_heads, tkb=tkb,
                             emit_vt=emit_vt, col_chunk=512)
    res = pl.pallas_call(
        kern,
        grid=(b, t // tm),
        out_shape=tuple(outs[n][0] for n in names),
        in_specs=in_specs,
        out_specs=tuple(outs[n][1] for n in names),
        input_output_aliases=aliases,
        compiler_params=_cparams(("arbitrary", "arbitrary")),
        name="in_proj_vt" if emit_vt else "in_proj",
    )(x, norm_pre.reshape(depth, 1, d_model), w_bf, hg_lb, *carried)
    return dict(zip(names, res))


def _hgrn_off_shape(chunk, sub):
    nsub = chunk // sub
    n_off_cols = sub * (nsub * (nsub - 1) // 2)
    return chunk - sub, n_off_cols, -(-n_off_cols // 128) * 128


def _hgrn_kernel(q_ref, k_ref, v_ref, g_ref, gate_ref, nw_ref, o_ref, s_ref, st_ref, tri_ref,
                 offm_ref, diagm_ref, cum_ref, qe_ref, dec_ref, kv_ref, oi_ref, *, chunk, sub,
                 n_chunks, unroll, group, hp, dk):
    j = pl.program_id(2)
    nsub = chunk // sub
    n_off_rows, n_off_cols, n_off_pad = _hgrn_off_shape(chunk, sub)

    @pl.when(j == 0)
    def _():
        st_ref[...] = jnp.zeros_like(st_ref)
        r_i = lax.broadcasted_iota(jnp.int32, (chunk, chunk), 0)
        c_i = lax.broadcasted_iota(jnp.int32, (chunk, chunk), 1)
        tri_ref[...] = jnp.where(c_i <= r_i, 1.0, 0.0).astype(BF16)

        ro = lax.broadcasted_iota(jnp.int32, (n_off_rows, n_off_pad), 0) // sub + 1
        co = lax.broadcasted_iota(jnp.int32, (n_off_rows, n_off_pad), 1)
        cblk = jnp.zeros_like(co)
        start = 0
        for i in range(1, nsub):
            cblk = jnp.where((co >= start) & (co < start + i * sub), i, cblk)
            start += i * sub
        offm_ref[...] = jnp.where(ro == cblk, 1.0, 0.0)

        rd = lax.broadcasted_iota(jnp.int32, (nsub * sub * sub, chunk), 0)
        cd = lax.broadcasted_iota(jnp.int32, (nsub * sub * sub, chunk), 1)
        r_is = rd // sub
        diagm_ref[...] = jnp.where((r_is == cd) & (rd % sub >= r_is % sub), 1.0, 0.0)

    g = jnp.concatenate([g_ref[0, c * chunk:(c + 1) * chunk, :] for c in range(n_chunks)], axis=1)
    g1 = g.astype(BF16)
    rem = g - g1.astype(F32)
    g2 = rem.astype(BF16)
    g3 = (rem - g2.astype(F32)).astype(BF16)
    tri = tri_ref[...]
    cum_all = _dot(tri, g1) + _dot(tri, g2) + _dot(tri, g3)
    for c in range(n_chunks * hp):
        cum_ref[c] = cum_all[:, c * dk:(c + 1) * dk]

    def stage_a(ci, hh):
        r0 = pl.multiple_of(ci * chunk, chunk)
        cols = slice(hh * dk, (hh + 1) * dk)
        item = ci * hp + hh
        q = q_ref[0, pl.ds(r0, chunk), cols].astype(F32)
        k_bf = k_ref[0, pl.ds(r0, chunk), cols]
        k = k_bf.astype(F32)
        v = v_ref[0, pl.ds(r0, chunk), cols]
        cum = cum_ref[item]

        last = cum[chunk - 1:chunk, :]
        qe_ref[item] = (q * jnp.exp2(cum)).astype(BF16)
        dec_ref[item] = jnp.exp2(last)
        kv = _dot_tn(v, (k * jnp.exp2(last - cum)).astype(BF16))

        q_parts, k_parts, v_parts = [], [], []
        v32 = v.astype(F32)
        for i in range(1, nsub):
            b_i = cum[i * sub - 1:i * sub, :]
            lo, hi = i * sub, (i + 1) * sub
            q_parts.append(q[lo:hi] * jnp.exp2(cum[lo:hi] - b_i))
            k_parts.append(k[:lo] * jnp.exp2(b_i - cum[:lo]))
            v_parts.append(v32[:lo])
        if n_off_pad > n_off_cols:
            k_parts.append(jnp.zeros((n_off_pad - n_off_cols, dk), F32))
            v_parts.append(jnp.zeros((n_off_pad - n_off_cols, v.shape[-1]), F32))
        a_off = _dot_nt(jnp.concatenate(q_parts, axis=0).astype(BF16),
                        jnp.concatenate(k_parts, axis=0).astype(BF16))
        v_off = jnp.concatenate(v_parts, axis=0).astype(BF16)

        n_parts = []
        for i in range(nsub):
            lo, hi = i * sub, (i + 1) * sub
            c_blk = cum[lo:hi]
            q_blk = q[lo:hi]
            for s in range(sub):
                dec = jnp.exp2(jnp.minimum(c_blk - c_blk[s:s + 1, :], 0.0))
                n_parts.append(q_blk * dec)
        r = _dot_nt(jnp.concatenate(n_parts, axis=0).astype(BF16), k_bf)
        return item, kv, a_off, v_off, r, v

    def stage_b(item, kv, a_off, v_off, r, v):
        kv_ref[item] = kv
        a_off = (a_off * offm_ref[...]).astype(BF16)
        o_off = _dot(a_off, v_off)
        r = r * diagm_ref[...]
        a_parts = []
        for i in range(nsub):
            base = i * sub * sub
            acc = r[base:base + sub]
            for s in range(1, sub):
                acc = acc + r[base + s * sub:base + (s + 1) * sub]
            a_parts.append(acc)
        a_diag = jnp.concatenate(a_parts, axis=0).astype(BF16)
        return item, o_off, _dot(a_diag, v)

    def stage_c(item, o_off, o_diag):
        oi_ref[item] = o_diag + jnp.concatenate([jnp.zeros((sub, dk), F32), o_off], axis=0)

    def intra(gi, carry):
        items = [(gi * group + u // hp, u % hp) for u in range(group * hp)]
        mid = [stage_a(ci, hh) for ci, hh in items]
        for x in [stage_b(*x) for x in mid]:
            stage_c(*x)
        return carry

    lax.fori_loop(0, n_chunks // group, intra, 0)

    def inter(ci, carry):
        r0 = pl.multiple_of(ci * chunk, chunk)
        for hh in range(hp):
            cols = slice(hh * dk, (hh + 1) * dk)
            item = ci * hp + hh
            s_prev = st_ref[hh]
            o = oi_ref[item] + _dot_nt(qe_ref[item], s_prev.astype(BF16))
            st_ref[hh] = s_prev * dec_ref[item] + kv_ref[item]
            gate = gate_ref[0, pl.ds(r0, chunk), cols].astype(F32)
            o_ref[0, pl.ds(r0, chunk), cols] = (_rms(o, nw_ref[...]) * gate).astype(o_ref.dtype)
        return carry

    lax.fori_loop(0, n_chunks, inter, 0, unroll=unroll)

    @pl.when(j == pl.num_programs(2) - 1)
    def _():
        for hh in range(hp):
            s_ref[0, hh] = st_ref[hh].T


def _hgrn_prompt(a, hg_norm_l, *, heads, dk, tt, group, unroll, hp):
    b, t, _ = a["qh"].shape
    spec = pl.BlockSpec((1, tt, hp * dk), lambda i, h, j: (i, j, h))
    chunk, sub = HG_CHUNK, HG_SUB
    n_chunks = tt // chunk
    n_off_rows, _, n_off_pad = _hgrn_off_shape(chunk, sub)
    kern = functools.partial(_hgrn_kernel, chunk=chunk, sub=sub, n_chunks=n_chunks,
                             unroll=unroll, group=group, hp=hp, dk=dk)
    n_items = n_chunks * hp
    return pl.pallas_call(
        kern,
        grid=(b, heads // hp, t // tt),
        out_shape=(jax.ShapeDtypeStruct((b, t, heads * dk), BF16),
                   jax.ShapeDtypeStruct((b, heads, dk, dk), F32)),
        in_specs=[spec, spec, spec, spec, spec, pl.BlockSpec((1, dk), lambda i, h, j: (0, 0))],
        out_specs=(spec, pl.BlockSpec((1, hp, dk, dk), lambda i, h, j: (i, h, 0, 0))),
        scratch_shapes=[pltpu.VMEM((hp, dk, dk), F32), pltpu.VMEM((chunk, chunk), BF16),
                        pltpu.VMEM((n_off_rows, n_off_pad), F32),
                        pltpu.VMEM((chunk * sub, chunk), F32),
                        pltpu.VMEM((n_items, chunk, dk), F32),
                        pltpu.VMEM((n_items, chunk, dk), BF16),
                        pltpu.VMEM((n_items, 1, dk), F32),
                        pltpu.VMEM((n_items, dk, dk), F32),
                        pltpu.VMEM((n_items, chunk, dk), F32)],
        compiler_params=_cparams(("arbitrary", "arbitrary", "arbitrary")),
        name="hgrn_prompt",
    )(a["qh"], a["kh"], a["vh"], a["lf"], a["gh"], hg_norm_l.reshape(1, dk))


def _hgrn_step_kernel(q_ref, lf_ref, v_ref, gate_ref, nw_ref, s_ref, *refs, heads, dk):
    o_ref, sn_ref = refs[-2:]

    def col(row):
        return jnp.broadcast_to(row, (dk, dk)).T

    for h in range(heads):
        sl = slice(h * dk, (h + 1) * dk)
        f = jnp.exp2(lf_ref[0, :, sl])
        s_new = col(f) * s_ref[0, h] + col(1.0 - f) * v_ref[0, :, sl].astype(F32)
        sn_ref[0, 0, h] = s_new
        o = jnp.sum(col(q_ref[0, :, sl].astype(F32)) * s_new, axis=0, keepdims=True)
        o_ref[0, :, sl] = (_rms(o, nw_ref[...]) * gate_ref[0, :, sl].astype(F32)).astype(o_ref.dtype)


def _hgrn_step(a, hg_norm_l, state_hgrn, carried, layer, *, heads, dk):
    db = a["qh"].shape[0]
    w = heads * dk
    depth = state_hgrn.shape[0]
    row = pl.BlockSpec((1, 1, w), lambda i: (i, 0, 0))
    state = pl.BlockSpec((1, 1, heads, dk, dk), lambda i: (layer, i, 0, 0, 0))
    in_specs = [row, row, row, row, pl.BlockSpec((1, dk), lambda i: (0, 0)),
                pl.BlockSpec((None, 1, heads, dk, dk), lambda i: (layer, i, 0, 0, 0))]
    in_specs += [pl.BlockSpec(memory_space=pl.ANY)] * len(carried)
    return pl.pallas_call(
        functools.partial(_hgrn_step_kernel, heads=heads, dk=dk),
        grid=(db,),
        out_shape=(jax.ShapeDtypeStruct((db, 1, w), BF16),
                   jax.ShapeDtypeStruct((depth, db, heads, dk, dk), F32)),
        in_specs=in_specs,
        out_specs=(row, state),
        input_output_aliases={len(in_specs) - 1: 1} if carried else {},
        compiler_params=_cparams(("arbitrary",)),
        name="hgrn_step",
    )(a["qh"], a["lf"], a["vh"], a["gh"], hg_norm_l.reshape(1, dk), state_hgrn, *carried)


def _da_prompt_kernel(q_ref, k_ref, vt_ref, bias_ref, gate_ref, nw_ref, lq1_ref, lk1_ref,
                      lq2_ref, lk2_ref, o_ref, acc_ref, s0_ref, s1_ref, p0_ref, p1_ref, al0_ref,
                      al1_ref, mb0_ref, mb1_ref, *, tq, tk, lam_init, hp, dv, pairs_per_trip):
    mb_refs = (mb0_ref, mb1_ref)
    i = pl.program_id(2)
    half = dv // 2
    row = lax.broadcasted_iota(jnp.int32, (dv, tq), 0)

    def qbd_of(hh):
        qt = q_ref[0, :, hh * dv:(hh + 1) * dv].astype(F32).T
        return jnp.concatenate([jnp.where(row < half, qt, 0.0), jnp.where(row >= half, qt, 0.0)],
                               axis=1).astype(BF16)

    qbd = [qbd_of(hh) for hh in range(hp)]
    s_refs, p_refs, al_refs = (s0_ref, s1_ref), (p0_ref, p1_ref), (al0_ref, al1_ref)
    acc_ref[...] = jnp.zeros_like(acc_ref)

    def scores(slot, blk):
        for hh in range(hp):
            k_blk = k_ref[0, pl.ds(pl.multiple_of(blk * tk, tk), tk), hh * dv:(hh + 1) * dv]
            s = _dot(k_blk, qbd[hh])
            s_refs[slot][hh] = s
            mb_refs[slot][hh] = jnp.max(s, axis=0, keepdims=True)

    def probs(slot, ms, bias_idx):
        out = []
        for hh in range(hp):
            s = s_refs[slot][hh]
            if bias_idx is None:
                m_blk = mb_refs[slot][hh]
            else:
                bias = bias_ref[hh, bias_idx]
                s = s + jnp.concatenate([bias, bias], axis=1)
                m_blk = jnp.max(s, axis=0, keepdims=True)
            m_new = jnp.maximum(ms[hh], m_blk)
            al_refs[slot][hh] = jnp.exp2(ms[hh] - m_new)
            p_refs[slot][hh] = jnp.exp2(s - m_new).astype(BF16)
            out.append(m_new)
        return tuple(out)

    def accumulate(slot, blk):
        kb = jnp.maximum(blk, 0)
        for hh in range(hp):
            vt = jnp.concatenate([vt_ref[0, hh, kb], jnp.ones((ONES_ROWS, tk), BF16)], axis=0)
            acc_ref[hh] = al_refs[slot][hh] * acc_ref[hh] + _dot(vt, p_refs[slot][hh])

    m = tuple(jnp.full((1, 2 * tq), 0.1 * NEG_BIG, F32) for _ in range(hp))

    def far_pair(e, ms):
        accumulate(1, e - 1)
        ms = probs(0, ms, None)
        scores(1, e + 1)
        accumulate(0, e)
        ms = probs(1, ms, None)
        scores(0, e + 2)
        return ms

    def far_pairs(u, ms):
        for v in range(pairs_per_trip):
            ms = far_pair(2 * (pairs_per_trip * u + v), ms)
        return ms

    def tail_odd(ms):
        accumulate(1, i - 2)
        ms = probs(0, ms, 0)
        scores(1, i)
        accumulate(0, i - 1)
        ms = probs(1, ms, 1)
        accumulate(1, i)
        return ms

    def tail_even(ms):
        accumulate(1, i - 3)
        ms = probs(0, ms, None)
        scores(1, i - 1)
        accumulate(0, i - 2)
        ms = probs(1, ms, 0)
        scores(0, i)
        accumulate(1, i - 1)
        ms = probs(0, ms, 1)
        accumulate(0, i)
        return ms

    def tail_zero(ms):
        ms = probs(0, ms, 1)
        accumulate(0, i)
        return ms

    p1_ref[...] = jnp.zeros_like(p1_ref)
    al1_ref[...] = jnp.ones_like(al1_ref)
    scores(0, 0)
    n_pairs = jnp.right_shift(jnp.maximum(i - 1, 0), 1)
    n_trips = n_pairs // pairs_per_trip
    m = lax.fori_loop(0, n_trips, far_pairs, m)
    if pairs_per_trip > 1:
        m = lax.fori_loop(n_trips * pairs_per_trip, n_pairs, lambda u, ms: far_pair(2 * u, ms), m)
    lax.cond(i == 0, tail_zero,
             lambda ms: lax.cond((i & 1) == 1, tail_odd, tail_even, ms), m)

    lam = _lam(lq1_ref, lk1_ref, lq2_ref, lk2_ref, lam_init)
    for hh in range(hp):
        acc = acc_ref[hh, :dv, :]
        inv = 1.0 / acc_ref[hh, dv:dv + 1, :]
        out_t = acc[:, :tq] * inv[:, :tq] - lam * (acc[:, tq:] * inv[:, tq:])
        out = _rms(out_t.T, nw_ref[...]) * (1.0 - lam_init)
        cols = slice(hh * dv, (hh + 1) * dv)
        o_ref[0, :, cols] = (out * gate_ref[0, :, cols].astype(F32)).astype(o_ref.dtype)


def _da_prompt(a, bias, da_norm_l, lam_rows, lam_init, *, heads, dv, pairs_per_trip):
    b, t, _ = a["dq"].shape
    tq, tk = DA_TQ, DA_TK
    hp = DA_HEADS_PER_STEP
    qspec = pl.BlockSpec((1, tq, hp * dv), lambda i, h, j: (i, j, h))
    vec = pl.BlockSpec((1, dv // 2), lambda i, h, j: (0, 0))
    kern = functools.partial(_da_prompt_kernel, tq=tq, tk=tk, lam_init=lam_init, hp=hp, dv=dv,
                             pairs_per_trip=pairs_per_trip)
    return pl.pallas_call(
        kern,
        grid=(b, heads // hp, t // tq),
        out_shape=jax.ShapeDtypeStruct((b, t, heads * dv), BF16),
        in_specs=[
            qspec,
            pl.BlockSpec((1, t, hp * dv), lambda i, h, j: (i, 0, h)),
            pl.BlockSpec((1, hp, t // tk, dv, tk), lambda i, h, j: (i, h, 0, 0, 0)),
            pl.BlockSpec((hp, 2, tk, tq), lambda i, h, j: (h, 0, 0, 0)),
            qspec,
            pl.BlockSpec((1, dv), lambda i, h, j: (0, 0)),
            vec, vec, vec, vec,
        ],
        out_specs=qspec,
        scratch_shapes=[pltpu.VMEM((hp, dv + ONES_ROWS, 2 * tq), F32),
                        pltpu.VMEM((hp, tk, 2 * tq), F32), pltpu.VMEM((hp, tk, 2 * tq), F32),
                        pltpu.VMEM((hp, tk, 2 * tq), BF16), pltpu.VMEM((hp, tk, 2 * tq), BF16),
                        pltpu.VMEM((hp, 1, 2 * tq), F32), pltpu.VMEM((hp, 1, 2 * tq), F32),
                        pltpu.VMEM((hp, 1, 2 * tq), F32), pltpu.VMEM((hp, 1, 2 * tq), F32)],
        compiler_params=_cparams(("arbitrary", "arbitrary", "arbitrary")),
        name="da_prompt",
    )(a["dq"], a["dkb"], a["dvt"], bias, a["dg"], da_norm_l.reshape(1, dv), *lam_rows)


def _da_sample_kernel(pt_ref, q_ref, kn_ref, vn_ref, gate_ref, bias_ref, bias0_ref, nw_ref,
                      lq1_ref, lk1_ref, lq2_ref, lk2_ref, *refs, heads, dv, n_pages, lam_init):
    del pt_ref
    k_refs = refs[:n_pages]
    v_refs = refs[n_pages:2 * n_pages]
    o_ref, m_ref, l_ref, acc_ref = refs[2 * n_pages:]
    j = pl.program_id(1)
    half = dv // 2
    rows = 2 * heads

    @pl.when(j == 0)
    def _():
        m_ref[...] = jnp.full_like(m_ref, 0.1 * NEG_BIG)
        l_ref[...] = jnp.zeros_like(l_ref)
        acc_ref[...] = jnp.zeros_like(acc_ref)

    def per_map(x):
        return jnp.concatenate([x[:, h * dv:(h + 1) * dv] for h in range(heads) for _ in range(2)],
                               axis=0)

    r_i = lax.broadcasted_iota(jnp.int32, (rows, dv), 0)
    c_i = lax.broadcasted_iota(jnp.int32, (rows, dv), 1)
    q8 = jnp.where(c_i // half == r_i % 2, per_map(q_ref[0].astype(F32)), 0.0).astype(BF16)

    s = jnp.concatenate([_dot_nt(q8, k_refs[g][0, 0].astype(BF16)) for g in range(n_pages)],
                        axis=1)
    s = s + bias_ref[...]
    m_old = m_ref[...]
    m_new = jnp.maximum(m_old, jnp.max(s, axis=-1, keepdims=True))
    alpha = jnp.exp2(m_old - m_new)
    p = jnp.exp2(s - m_new)
    l_ref[...] = alpha * l_ref[...] + jnp.sum(p, axis=-1, keepdims=True)
    m_ref[...] = m_new
    n_cols = k_refs[0].shape[2]
    pv = jnp.zeros((rows, dv), F32)
    for g in range(n_pages):
        pv = pv + _dot(p[:, g * n_cols:(g + 1) * n_cols].astype(BF16),
                       v_refs[g][0, 0].astype(BF16))
    acc_ref[...] = alpha * acc_ref[...] + pv

    @pl.when(j == pl.num_programs(1) - 1)
    def _():
        kn = per_map(kn_ref[0].astype(BF16).astype(F32))
        s_n = jnp.sum(q8.astype(F32) * kn, axis=-1, keepdims=True) + bias0_ref[...]
        m_o = m_ref[...]
        m_f = jnp.maximum(m_o, s_n)
        al = jnp.exp2(m_o - m_f)
        p_n = jnp.exp2(s_n - m_f)
        l_f = al * l_ref[...] + p_n
        vn = per_map(vn_ref[0].astype(BF16).astype(F32))
        z = (al * acc_ref[...] + p_n.astype(BF16).astype(F32) * vn) / l_f
        lam = _lam(lq1_ref, lk1_ref, lq2_ref, lk2_ref, lam_init)
        gate = gate_ref[0].astype(F32)
        for h in range(heads):
            sl = slice(h * dv, (h + 1) * dv)
            o_h = z[2 * h:2 * h + 1] - lam * z[2 * h + 1:2 * h + 2]
            o_h = _rms(o_h, nw_ref[...]) * (1.0 - lam_init)
            o_ref[0, :, sl] = (o_h * gate[:, sl]).astype(o_ref.dtype)


def _da_sample(a, cache_k, cache_v, page_table, layer, bias_rows, bias0, da_norm_l, lam_rows,
               lam_init, *, heads, dv):
    db = a["dq"].shape[0]
    width = heads * dv
    n_pages_seq = page_table.shape[1]
    page = cache_k.shape[2]
    g_n = PAGES_PER_STEP
    steps = n_pages_seq // g_n
    rows = 2 * heads
    row = pl.BlockSpec((1, 1, width), lambda i, j, pt: (i, 0, 0))
    vec = pl.BlockSpec((1, dv // 2), lambda i, j, pt: (0, 0))

    def page_spec(g):
        return pl.BlockSpec((1, 1, page * heads, dv),
                            lambda i, j, pt, g=g: (layer, pt[i, j * g_n + g], 0, 0))

    kern = functools.partial(_da_sample_kernel, heads=heads, dv=dv, n_pages=g_n, lam_init=lam_init)
    grid_spec = pltpu.PrefetchScalarGridSpec(
        num_scalar_prefetch=1,
        grid=(db, steps),
        in_specs=[row, row, row, row,
                  pl.BlockSpec((rows, g_n * page * heads), lambda i, j, pt: (0, j)),
                  pl.BlockSpec((rows, 1), lambda i, j, pt: (0, 0)),
                  pl.BlockSpec((1, dv), lambda i, j, pt: (0, 0)),
                  vec, vec, vec, vec]
                 + [page_spec(g) for g in range(g_n)] * 2,
        out_specs=row,
        scratch_shapes=[pltpu.VMEM((rows, 1), F32), pltpu.VMEM((rows, 1), F32),
                        pltpu.VMEM((rows, dv), F32)],
    )
    ck = cache_k.reshape(cache_k.shape[0], cache_k.shape[1], page * heads, dv)
    cv = cache_v.reshape(cache_v.shape[0], cache_v.shape[1], page * heads, dv)
    return pl.pallas_call(
        kern,
        grid_spec=grid_spec,
        out_shape=jax.ShapeDtypeStruct((db, 1, width), BF16),
        compiler_params=_cparams(("arbitrary", "arbitrary")),
        name="da_sample",
    )(page_table, a["dq"], a["dkf"], a["dvf"], a["dg"], bias_rows, bias0,
      da_norm_l.reshape(1, dv), *lam_rows, *([ck] * g_n), *([cv] * g_n))


def _out_kernel(x_ref, oh_ref, od_ref, *refs, heads, dh, hgw, daw, fused_mem):
    if fused_mem:
        mq_ref, mg_ref, mk_ref, mv_ref, w_ref, nw_ref, y_ref = refs
        mq = mq_ref[0]
        mg = mg_ref[0].astype(F32)
        parts = []
        for h in range(heads):
            sl = slice(h * dh, (h + 1) * dh)
            s = _dot_nt(mq[:, sl], mk_ref[0, :, sl].astype(BF16)) * (dh ** -0.5)
            p = jnp.exp(s - jnp.max(s, axis=-1, keepdims=True))
            o = _dot(p.astype(BF16), mv_ref[0, :, sl].astype(BF16))
            o = o / jnp.sum(p, axis=-1, keepdims=True)
            parts.append((o * mg[:, sl]).astype(BF16))
        om = jnp.concatenate(parts, axis=-1)
    else:
        om_ref, w_ref, nw_ref, y_ref = refs
        om = om_ref[0]
    y = _dot(oh_ref[0], w_ref[0, :hgw, :])
    y = y + _dot(od_ref[0], w_ref[0, hgw:hgw + daw, :])
    y = y + _dot(om, w_ref[0, hgw + daw:, :])
    y_ref[0] = x_ref[0] + _rms(y, nw_ref[0])


def _out_proj(x, oh, od, mem_args, layer, w_out_bf, norm_post, *, heads, dh, tm, fused_mem):
    b, t, d_model = x.shape
    depth, d_mix, _ = w_out_bf.shape
    hgw = oh.shape[-1]
    daw = od.shape[-1]
    mxw = heads * dh

    def act(width):
        return pl.BlockSpec((1, tm, width), lambda i, j: (i, j, 0))

    if fused_mem:
        mem = pl.BlockSpec((None, 1, mem_args[2].shape[2], mxw), lambda i, j: (layer, i, 0, 0))
        mem_specs = [act(mxw), act(mxw), mem, mem]
    else:
        mem_specs = [act(mxw)]
    return pl.pallas_call(
        functools.partial(_out_kernel, heads=heads, dh=dh, hgw=hgw, daw=daw, fused_mem=fused_mem),
        grid=(b, t // tm),
        out_shape=jax.ShapeDtypeStruct((b, t, d_model), F32),
        in_specs=[act(d_model), act(hgw), act(daw)] + mem_specs
                 + [pl.BlockSpec((1, d_mix, d_model), lambda i, j: (layer, 0, 0),
                                 pipeline_mode=pl.Buffered(1)),
                    pl.BlockSpec((1, 1, d_model), lambda i, j: (layer, 0, 0))],
        out_specs=act(d_model),
        compiler_params=_cparams(("arbitrary", "arbitrary")),
        name="out_proj" if fused_mem else "out_proj_rows",
    )(x, oh, od, *mem_args, w_out_bf, norm_post.reshape(depth, 1, d_model))


def _mem_sample_kernel(mq_ref, mg_ref, mk_ref, mv_ref, om_ref, *, heads, dh):
    pad = 8
    q = mq_ref[0]
    q_rows = jnp.concatenate([q[:, h * dh:(h + 1) * dh] for h in range(heads)]
                             + [jnp.zeros((pad - heads, dh), BF16)], axis=0)
    s = _dot_nt(q_rows, mk_ref[0].astype(BF16)) * (dh ** -0.5)
    r_i = lax.broadcasted_iota(jnp.int32, s.shape, 0)
    c_i = lax.broadcasted_iota(jnp.int32, s.shape, 1)
    s = jnp.where(c_i % heads == r_i, s, NEG_BIG)
    p = jnp.exp(s - jnp.max(s, axis=-1, keepdims=True))
    o = _dot(p.astype(BF16), mv_ref[0].astype(BF16)) / jnp.sum(p, axis=-1, keepdims=True)
    mg = mg_ref[0].astype(F32)
    for h in range(heads):
        sl = slice(h * dh, (h + 1) * dh)
        om_ref[0, :, sl] = (o[h:h + 1] * mg[:, sl]).astype(om_ref.dtype)


def _mem_sample(a, cmk, cmv, layer, *, heads, dh):
    db = a["mq"].shape[0]
    width = heads * dh
    rows = cmk.shape[2]
    row = pl.BlockSpec((1, 1, width), lambda i: (i, 0, 0))
    mem = pl.BlockSpec((None, 1, rows, dh), lambda i: (layer, i, 0, 0))
    return pl.pallas_call(
        functools.partial(_mem_sample_kernel, heads=heads, dh=dh),
        grid=(db,),
        out_shape=jax.ShapeDtypeStruct((db, 1, width), BF16),
        in_specs=[row, row, mem, mem],
        out_specs=row,
        compiler_params=_cparams(("arbitrary",)),
        name="mem_sample",
    )(a["mq"], a["mg"], cmk, cmv)


def kernel(x_prompt, x_sample, mem_prompt, cache_da_k, cache_da_v, cache_mem_k, cache_mem_v,
           state_hgrn, page_table, w_in, w_out, w_mem_kv, norm_pre, norm_post, mem_norm, hg_norm,
           da_norm, hg_lb, da_lq1, da_lk1, da_lq2, da_lk2, rel_bias):
    depth = w_in.shape[0]
    b, t, d_model = x_prompt.shape
    db = x_sample.shape[0]
    hg_heads, hg_dk = state_hgrn.shape[2], state_hgrn.shape[3]
    hgw = hg_heads * hg_dk
    da_heads, dv = cache_da_v.shape[3], cache_da_v.shape[4]
    daw = da_heads * dv
    mx_heads, mx_dh = cache_mem_k.shape[3], cache_mem_k.shape[4]
    mxw = mx_heads * mx_dh
    n_mem = mem_prompt.shape[1]
    page = cache_da_k.shape[2]
    past = page_table.shape[1] * page
    assert w_in.shape[2] == 4 * hgw + 4 * daw + 2 * mxw
    assert x_sample.shape[1] == 1 and MAX_DIST <= page <= DA_TK
    assert t % max(DA_TQ, 512) == 0 and page_table.shape[1] % PAGES_PER_STEP == 0

    w_in_bf = w_in.astype(BF16)
    w_out_bf = w_out.astype(BF16)
    w_mem_bf = w_mem_kv.astype(BF16)

    bias = _bias_tiles(rel_bias, da_heads)
    near = bias[:, 0, DA_TK - page:, 0]
    bias_rows = jnp.concatenate([jnp.zeros((da_heads, past - page), F32), near], axis=1)
    bias_rows = jnp.repeat(bias_rows, 2, axis=0)
    col_head = jnp.arange(past * da_heads, dtype=jnp.int32) % da_heads
    row_head = jnp.arange(2 * da_heads, dtype=jnp.int32) // 2
    bias_rows = jnp.where(col_head[None, :] == row_head[:, None],
                          jnp.repeat(bias_rows, da_heads, axis=1), NEG_BIG)
    bias0 = jnp.repeat(bias[:, 1, 0, 0], 2).reshape(2 * da_heads, 1)

    mk_p, mv_p = _mem_kv(mem_prompt, mem_norm, w_mem_bf)
    cmk = cache_mem_k.reshape(depth, db, n_mem * mx_heads, mx_dh)
    cmv = cache_mem_v.reshape(depth, db, n_mem * mx_heads, mx_dh)

    xp, xs = x_prompt, x_sample.reshape(1, db, d_model)
    outs = {n: [] for n in ("sp", "ks", "vs")}
    kv_all = ()
    s_s = ()
    dims = dict(hgw=hgw, daw=daw, mxw=mxw, da_heads=da_heads)
    for l in range(depth):
        lam_init = 0.8 - 0.6 * math.exp(-0.3 * l)
        lam_rows = [v[l].reshape(1, dv // 2) for v in (da_lq1, da_lk1, da_lq2, da_lk2)]

        trial = l == 1
        a = _in_proj(xp, l, norm_pre, w_in_bf, hg_lb, kv_all, tm=512, emit_vt=True, **dims)
        kv_all = (a["dkf"], a["dvf"])
        oh, s_p = _hgrn_prompt(a, hg_norm[l], heads=hg_heads, dk=hg_dk, tt=512,
                               group=4 if trial else HG_GROUP, unroll=HG_UNROLL,
                               hp=4 if trial else HG_HEADS_PER_STEP)
        od = _da_prompt(a, bias, da_norm[l], lam_rows, lam_init, heads=da_heads, dv=dv,
                        pairs_per_trip=4 if trial else 2)
        xp = _out_proj(xp, oh, od, (a["mq"], a["mg"], mk_p, mv_p), l, w_out_bf, norm_post,
                       heads=mx_heads, dh=mx_dh, tm=1024, fused_mem=True)
        outs["sp"].append(s_p)

        a = _in_proj(xs, l, norm_pre, w_in_bf, hg_lb, (), tm=db, emit_vt=False, **dims)
        a = {n: v.reshape(db, 1, v.shape[-1]) for n, v in a.items()}
        oh, s_s = _hgrn_step(a, hg_norm[l], state_hgrn, s_s, l, heads=hg_heads, dk=hg_dk)
        s_s = (s_s,)
        od = _da_sample(a, cache_da_k, cache_da_v, page_table, l, bias_rows, bias0, da_norm[l],
                        lam_rows, lam_init, heads=da_heads, dv=dv)
        om = _mem_sample(a, cmk, cmv, l, heads=mx_heads, dh=mx_dh)
        as_rows = lambda v: v.reshape(1, db, v.shape[-1])
        xs = _out_proj(xs, as_rows(oh), as_rows(od), (as_rows(om),), l, w_out_bf, norm_post,
                       heads=mx_heads, dh=mx_dh, tm=db, fused_mem=False)
        outs["ks"].append(a["dkf"].reshape(db, 1, da_heads, dv))
        outs["vs"].append(a["dvf"].reshape(db, 1, da_heads, dv))

    return (xp, xs.reshape(db, 1, d_model), kv_all[0], kv_all[1],
            jnp.stack(outs["sp"]), mk_p.reshape(depth, b, n_mem, mx_heads, mx_dh),
            mv_p.reshape(depth, b, n_mem, mx_heads, mx_dh), jnp.stack(outs["ks"]),
            jnp.stack(outs["vs"]), s_s[0])
```

```python
import functools
import math

import jax
import jax.numpy as jnp
from jax import lax
from jax.experimental import pallas as pl
from jax.experimental.pallas import tpu as pltpu

F32 = jnp.float32
BF16 = jnp.bfloat16

EPS = 1e-6
MAX_DIST = 128
HG_CHUNK = 64
HG_SUB = 8
HG_UNROLL = 8
HG_GROUP = 4
HG_HEADS_PER_STEP = 4
DA_HEADS_PER_STEP = 2
DA_PAIRS_PER_TRIP = 4
LOG2E = math.log2(math.e)
ONES_ROWS = 16
DA_TQ = 256
DA_TK = 256
PAGES_PER_STEP = 16
SAMPLE_ROWS_PER_STEP = 4
NEG_BIG = -1e30
VMEM_LIMIT_BYTES = 56 * 1024 * 1024


def _cparams(sem):
    return pltpu.CompilerParams(dimension_semantics=sem, vmem_limit_bytes=VMEM_LIMIT_BYTES)


def _sigmoid(x):
    return 1.0 / (1.0 + jnp.exp(-x))


def _silu(x):
    return x * _sigmoid(x)


def _rms(x, w):
    return x * lax.rsqrt(jnp.mean(x * x, axis=-1, keepdims=True) + EPS) * w


def _dot(a, b):
    return jnp.dot(a, b, preferred_element_type=F32)


def _dot_nt(a, b):
    return lax.dot_general(a, b, (((1,), (1,)), ((), ())), preferred_element_type=F32)


def _dot_tn(a, b):
    return lax.dot_general(a, b, (((0,), (0,)), ((), ())), preferred_element_type=F32)


def _lam(lq1_ref, lk1_ref, lq2_ref, lk2_ref, lam_init):
    a = jnp.sum(lq1_ref[...] * lk1_ref[...], axis=-1, keepdims=True)
    b = jnp.sum(lq2_ref[...] * lk2_ref[...], axis=-1, keepdims=True)
    return jnp.exp(a) - jnp.exp(b) + lam_init


def _bias_kernel(tab_ref, out_ref, *, n_buckets, heads, tk, tq):
    kk = lax.broadcasted_iota(jnp.int32, (tk, tq), 0)
    qq = lax.broadcasted_iota(jnp.int32, (tk, tq), 1)
    max_exact = n_buckets // 2
    for jj in range(2):
        d = qq - kk + (tk if jj == 0 else 0)
        n = jnp.maximum(d, 0)
        nf = jnp.maximum(n, 1).astype(F32)
        large = max_exact + (jnp.log(nf / max_exact) / math.log(MAX_DIST / max_exact)
                             * (n_buckets - max_exact)).astype(jnp.int32)
        large = jnp.minimum(large, n_buckets - 1)
        bucket = jnp.where(n < max_exact, n, large)
        for h in range(heads):
            val = jnp.zeros((tk, tq), F32)
            for b in range(n_buckets):
                val = jnp.where(bucket == b, tab_ref[b, h], val)
            val = (val - tab_ref[n_buckets - 1, h]) * LOG2E
            out_ref[h, jj] = jnp.where(d < 0, NEG_BIG, val)


def _bias_tiles(rel_bias, heads):
    n_buckets = rel_bias.shape[0]
    return pl.pallas_call(
        functools.partial(_bias_kernel, n_buckets=n_buckets, heads=heads, tk=DA_TK, tq=DA_TQ),
        out_shape=jax.ShapeDtypeStruct((heads, 2, DA_TK, DA_TQ), F32),
        in_specs=[pl.BlockSpec(memory_space=pltpu.SMEM)],
        out_specs=pl.BlockSpec(memory_space=pltpu.VMEM),
        name="t5_bias",
    )(rel_bias)


def _memkv_kernel(mem_ref, nw_ref, w_ref, k_ref, v_ref, *, width):
    xn = _rms(mem_ref[0], nw_ref[0]).astype(BF16)
    y = _dot(xn, w_ref[0])
    k_ref[0, 0] = y[:, :width]
    v_ref[0, 0] = y[:, width:]


def _mem_kv(mem_prompt, mem_norm, w_mem_kv_bf):
    depth, d_model, two_w = w_mem_kv_bf.shape
    width = two_w // 2
    b, n_mem, _ = mem_prompt.shape
    out = jax.ShapeDtypeStruct((depth, b, n_mem, width), F32)
    return pl.pallas_call(
        functools.partial(_memkv_kernel, width=width),
        grid=(depth, b),
        out_shape=(out, out),
        in_specs=[
            pl.BlockSpec((1, n_mem, d_model), lambda l, i: (i, 0, 0)),
            pl.BlockSpec((1, 1, d_model), lambda l, i: (l, 0, 0)),
            pl.BlockSpec((1, d_model, two_w), lambda l, i: (l, 0, 0)),
        ],
        out_specs=(pl.BlockSpec((1, 1, n_mem, width), lambda l, i: (l, i, 0, 0)),
                   pl.BlockSpec((1, 1, n_mem, width), lambda l, i: (l, i, 0, 0))),
        compiler_params=_cparams(("arbitrary", "arbitrary")),
        name="mem_kv",
    )(mem_prompt, mem_norm.reshape(depth, 1, d_model), w_mem_kv_bf)


def _inproj_kernel(x_ref, nw_ref, w_ref, lbp_ref, *refs, names, n_carried, layer, hgw, daw, mxw,
                   da_heads, tkb, emit_vt, col_chunk):
    r = dict(zip(names, refs[n_carried:]))
    xn = _rms(x_ref[0], nw_ref[0]).astype(BF16)

    p = lbp_ref[...]
    e = jnp.exp(p - jnp.max(p, axis=0, keepdims=True))
    sm = e / jnp.sum(e, axis=0, keepdims=True)
    lb = jnp.zeros((1, hgw), F32)
    for row in range(1, layer + 1):
        lb = lb + sm[row:row + 1, :]

    def seg(c0, width, fn):
        for c in range(0, width, col_chunk):
            w = min(col_chunk, width - c)
            fn(c, w, _dot(xn, w_ref[0, :, c0 + c:c0 + c + w]))

    def st(ref, fn):
        def go(c, w, y):
            ref[0, :, c:c + w] = fn(y).astype(ref.dtype)
        return go

    def forget(c, w, y):
        f = lb[:, c:c + w] + (1.0 - lb[:, c:c + w]) * _sigmoid(y)
        r["lf"][0, :, c:c + w] = jnp.log2(f)
        r["kh"][0, :, c:c + w] = (1.0 - f).astype(BF16)

    dv = daw // da_heads

    def st_f32(ref, c, w, y):
        if emit_vt:
            for hh in range(c // dv, (c + w) // dv):
                ref[0, 0, :, hh, :] = y[:, hh * dv - c:(hh + 1) * dv - c]
        else:
            ref[0, :, c:c + w] = y

    def da_k(c, w, y):
        st_f32(r["dkf"], c, w, y)
        r["dkb"][0, :, c:c + w] = y.astype(BF16)

    def da_v(c, w, y):
        st_f32(r["dvf"], c, w, y)
        if emit_vt:
            tm = y.shape[0]
            for hh in range(c // dv, (c + w) // dv):
                for kb in range(tm // tkb):
                    blk = y[kb * tkb:(kb + 1) * tkb, hh * dv - c:(hh + 1) * dv - c]
                    r["dvt"][0, hh, kb] = blk.T.astype(BF16)

    o = 0
    seg(o, hgw, st(r["qh"], _silu)); o += hgw
    seg(o, hgw, forget); o += hgw
    seg(o, hgw, st(r["vh"], lambda y: y)); o += hgw
    seg(o, hgw, st(r["gh"], _silu)); o += hgw
    dqk = daw // da_heads // 2
    seg(o, daw, st(r["dq"], lambda y: y * (dqk ** -0.5 * LOG2E))); o += daw
    seg(o, daw, da_k); o += daw
    seg(o, daw, da_v); o += daw
    seg(o, daw, st(r["dg"], _silu)); o += daw
    seg(o, mxw, st(r["mq"], lambda y: y)); o += mxw
    seg(o, mxw, st(r["mg"], _silu)); o += mxw


def _in_proj(x, layer, norm_pre, w_bf, hg_lb, carried, *, hgw, daw, mxw, da_heads, tm, emit_vt):
    b, t, d_model = x.shape
    depth = w_bf.shape[0]
    tkb = DA_TK
    dv = daw // da_heads

    def act(width, dtype):
        return (jax.ShapeDtypeStruct((b, t, width), dtype),
                pl.BlockSpec((1, tm, width), lambda i, j: (i, j, 0)))

    def rows_f32():
        if not emit_vt:
            return act(daw, F32)
        return (jax.ShapeDtypeStruct((depth, b, t, da_heads, dv), F32),
                pl.BlockSpec((1, 1, tm, da_heads, dv), lambda i, j: (layer, i, j, 0, 0)))

    outs = dict(qh=act(hgw, BF16), kh=act(hgw, BF16), vh=act(hgw, BF16), lf=act(hgw, F32),
                gh=act(hgw, BF16), dq=act(daw, BF16), dkf=rows_f32(), dkb=act(daw, BF16),
                dvf=rows_f32())
    if emit_vt:
        outs["dvt"] = (jax.ShapeDtypeStruct((b, da_heads, t // tkb, dv, tkb), BF16),
                       pl.BlockSpec((1, da_heads, tm // tkb, dv, tkb),
                                    lambda i, j: (i, 0, j, 0, 0)))
    outs.update(dg=act(daw, BF16), mq=act(mxw, BF16), mg=act(mxw, BF16))
    names = tuple(outs)
    in_specs = [
        pl.BlockSpec((1, tm, d_model), lambda i, j: (i, j, 0)),
        pl.BlockSpec((1, 1, d_model), lambda i, j: (layer, 0, 0)),
        pl.BlockSpec((1,) + w_bf.shape[1:], lambda i, j: (layer, 0, 0),
                     pipeline_mode=pl.Buffered(1)),
        pl.BlockSpec((depth, hgw), lambda i, j: (0, 0)),
    ] + [pl.BlockSpec(memory_space=pl.ANY)] * len(carried)
    aliases = {len(in_specs) - len(carried) + n: names.index(name)
               for n, name in enumerate(("dkf", "dvf")[:len(carried)])}
    kern = functools.partial(_inproj_kernel, names=names, n_carried=len(carried), layer=layer,
                             hgw=hgw, daw=daw, mxw=mxw, da_heads=da_heads, tkb=tkb,
                             emit_vt=emit_vt, col_chunk=512)
    res = pl.pallas_call(
        kern,
        grid=(b, t // tm),
        out_shape=tuple(outs[n][0] for n in names),
        in_specs=in_specs,
        out_specs=tuple(outs[n][1] for n in names),
        input_output_aliases=aliases,
        compiler_params=_cparams(("arbitrary", "arbitrary")),
        name="in_proj_vt" if emit_vt else "in_proj",
    )(x, norm_pre.reshape(depth, 1, d_model), w_bf, hg_lb, *carried)
    return dict(zip(names, res))


def _hgrn_off_shape(chunk, sub):
    nsub = chunk // sub
    n_off_cols = sub * (nsub * (nsub - 1) // 2)
    return chunk - sub, n_off_cols, -(-n_off_cols // 128) * 128


def _hgrn_kernel(q_ref, k_ref, v_ref, g_ref, gate_ref, nw_ref, o_ref, s_ref, st_ref, tri_ref,
                 offm_ref, diagm_ref, cum_ref, qe_ref, dec_ref, kv_ref, oi_ref, *, chunk, sub,
                 n_chunks, unroll, group, hp, dk):
    j = pl.program_id(2)
    nsub = chunk // sub
    n_off_rows, n_off_cols, n_off_pad = _hgrn_off_shape(chunk, sub)

    @pl.when(j == 0)
    def _():
        st_ref[...] = jnp.zeros_like(st_ref)
        r_i = lax.broadcasted_iota(jnp.int32, (chunk, chunk), 0)
        c_i = lax.broadcasted_iota(jnp.int32, (chunk, chunk), 1)
        tri_ref[...] = jnp.where(c_i <= r_i, 1.0, 0.0).astype(BF16)

        ro = lax.broadcasted_iota(jnp.int32, (n_off_rows, n_off_pad), 0) // sub + 1
        co = lax.broadcasted_iota(jnp.int32, (n_off_rows, n_off_pad), 1)
        cblk = jnp.zeros_like(co)
        start = 0
        for i in range(1, nsub):
            cblk = jnp.where((co >= start) & (co < start + i * sub), i, cblk)
            start += i * sub
        offm_ref[...] = jnp.where(ro == cblk, 1.0, 0.0)

        rd = lax.broadcasted_iota(jnp.int32, (nsub * sub * sub, chunk), 0)
        cd = lax.broadcasted_iota(jnp.int32, (nsub * sub * sub, chunk), 1)
        r_is = rd // sub
        diagm_ref[...] = jnp.where((r_is == cd) & (rd % sub >= r_is % sub), 1.0, 0.0)

    g = jnp.concatenate([g_ref[0, c * chunk:(c + 1) * chunk, :] for c in range(n_chunks)], axis=1)
    g1 = g.astype(BF16)
    rem = g - g1.astype(F32)
    g2 = rem.astype(BF16)
    g3 = (rem - g2.astype(F32)).astype(BF16)
    tri = tri_ref[...]
    cum_all = _dot(tri, g1) + _dot(tri, g2) + _dot(tri, g3)
    for c in range(n_chunks * hp):
        cum_ref[c] = cum_all[:, c * dk:(c + 1) * dk]

    def stage_a(ci, hh):
        r0 = pl.multiple_of(ci * chunk, chunk)
        cols = slice(hh * dk, (hh + 1) * dk)
        item = ci * hp + hh
        q = q_ref[0, pl.ds(r0, chunk), cols].astype(F32)
        k_bf = k_ref[0, pl.ds(r0, chunk), cols]
        k = k_bf.astype(F32)
        v = v_ref[0, pl.ds(r0, chunk), cols]
        cum = cum_ref[item]

        last = cum[chunk - 1:chunk, :]
        qe_ref[item] = (q * jnp.exp2(cum)).astype(BF16)
        dec_ref[item] = jnp.exp2(last)
        kv = _dot_tn(v, (k * jnp.exp2(last - cum)).astype(BF16))

        q_parts, k_parts, v_parts = [], [], []
        v32 = v.astype(F32)
        for i in range(1, nsub):
            b_i = cum[i * sub - 1:i * sub, :]
            lo, hi = i * sub, (i + 1) * sub
            q_parts.append(q[lo:hi] * jnp.exp2(cum[lo:hi] - b_i))
            k_parts.append(k[:lo] * jnp.exp2(b_i - cum[:lo]))
            v_parts.append(v32[:lo])
        if n_off_pad > n_off_cols:
            k_parts.append(jnp.zeros((n_off_pad - n_off_cols, dk), F32))
            v_parts.append(jnp.zeros((n_off_pad - n_off_cols, v.shape[-1]), F32))
        a_off = _dot_nt(jnp.concatenate(q_parts, axis=0).astype(BF16),
                        jnp.concatenate(k_parts, axis=0).astype(BF16))
        v_off = jnp.concatenate(v_parts, axis=0).astype(BF16)

        n_parts = []
        for i in range(nsub):
            lo, hi = i * sub, (i + 1) * sub
            c_blk = cum[lo:hi]
            q_blk = q[lo:hi]
            for s in range(sub):
                dec = jnp.exp2(jnp.minimum(c_blk - c_blk[s:s + 1, :], 0.0))
                n_parts.append(q_blk * dec)
        r = _dot_nt(jnp.concatenate(n_parts, axis=0).astype(BF16), k_bf)
        return item, kv, a_off, v_off, r, v

    def stage_b(item, kv, a_off, v_off, r, v):
        kv_ref[item] = kv
        a_off = (a_off * offm_ref[...]).astype(BF16)
        o_off = _dot(a_off, v_off)
        r = r * diagm_ref[...]
        a_parts = []
        for i in range(nsub):
            base = i * sub * sub
            acc = r[base:base + sub]
            for s in range(1, sub):
                acc = acc + r[base + s * sub:base + (s + 1) * sub]
            a_parts.append(acc)
        a_diag = jnp.concatenate(a_parts, axis=0).astype(BF16)
        return item, o_off, _dot(a_diag, v)

    def stage_c(item, o_off, o_diag):
        oi_ref[item] = o_diag + jnp.concatenate([jnp.zeros((sub, dk), F32), o_off], axis=0)

    def intra(gi, carry):
        items = [(gi * group + u // hp, u % hp) for u in range(group * hp)]
        mid = [stage_a(ci, hh) for ci, hh in items]
        for x in [stage_b(*x) for x in mid]:
            stage_c(*x)
        return carry

    lax.fori_loop(0, n_chunks // group, intra, 0)

    def inter(ci, carry):
        r0 = pl.multiple_of(ci * chunk, chunk)
        for hh in range(hp):
            cols = slice(hh * dk, (hh + 1) * dk)
            item = ci * hp + hh
            s_prev = st_ref[hh]
            o = oi_ref[item] + _dot_nt(qe_ref[item], s_prev.astype(BF16))
            st_ref[hh] = s_prev * dec_ref[item] + kv_ref[item]
            gate = gate_ref[0, pl.ds(r0, chunk), cols].astype(F32)
            o_ref[0, pl.ds(r0, chunk), cols] = (_rms(o, nw_ref[...]) * gate).astype(o_ref.dtype)
        return carry

    lax.fori_loop(0, n_chunks, inter, 0, unroll=unroll)

    @pl.when(j == pl.num_programs(2) - 1)
    def _():
        for hh in range(hp):
            s_ref[0, hh] = st_ref[hh].T


def _hgrn_prompt(a, hg_norm_l, *, heads, dk, tt, group, unroll, hp):
    b, t, _ = a["qh"].shape
    spec = pl.BlockSpec((1, tt, hp * dk), lambda i, h, j: (i, j, h))
    chunk, sub = HG_CHUNK, HG_SUB
    n_chunks = tt // chunk
    n_off_rows, _, n_off_pad = _hgrn_off_shape(chunk, sub)
    kern = functools.partial(_hgrn_kernel, chunk=chunk, sub=sub, n_chunks=n_chunks,
                             unroll=unroll, group=group, hp=hp, dk=dk)
    n_items = n_chunks * hp
    return pl.pallas_call(
        kern,
        grid=(b, heads // hp, t // tt),
        out_shape=(jax.ShapeDtypeStruct((b, t, heads * dk), BF16),
                   jax.ShapeDtypeStruct((b, heads, dk, dk), F32)),
        in_specs=[spec, spec, spec, spec, spec, pl.BlockSpec((1, dk), lambda i, h, j: (0, 0))],
        out_specs=(spec, pl.BlockSpec((1, hp, dk, dk), lambda i, h, j: (i, h, 0, 0))),
        scratch_shapes=[pltpu.VMEM((hp, dk, dk), F32), pltpu.VMEM((chunk, chunk), BF16),
                        pltpu.VMEM((n_off_rows, n_off_pad), F32),
                        pltpu.VMEM((chunk * sub, chunk), F32),
                        pltpu.VMEM((n_items, chunk, dk), F32),
                        pltpu.VMEM((n_items, chunk, dk), BF16),
                        pltpu.VMEM((n_items, 1, dk), F32),
                        pltpu.VMEM((n_items, dk, dk), F32),
                        pltpu.VMEM((n_items, chunk, dk), F32)],
        compiler_params=_cparams(("arbitrary", "arbitrary", "arbitrary")),
        name="hgrn_prompt",
    )(a["qh"], a["kh"], a["vh"], a["lf"], a["gh"], hg_norm_l.reshape(1, dk))


def _hgrn_step_kernel(q_ref, lf_ref, v_ref, gate_ref, nw_ref, s_ref, *refs, heads, dk):
    o_ref, sn_ref = refs[-2:]

    def col(row):
        return jnp.broadcast_to(row, (dk, dk)).T

    def one_sequence(s, carry):
        for h in range(heads):
            sl = slice(h * dk, (h + 1) * dk)
            f = jnp.exp2(lf_ref[s, :, sl])
            s_new = col(f) * s_ref[s, h] + col(1.0 - f) * v_ref[s, :, sl].astype(F32)
            sn_ref[0, s, h] = s_new
            o = jnp.sum(col(q_ref[s, :, sl].astype(F32)) * s_new, axis=0, keepdims=True)
            o = _rms(o, nw_ref[...]) * gate_ref[s, :, sl].astype(F32)
            o_ref[s, :, sl] = o.astype(o_ref.dtype)
        return carry

    lax.fori_loop(0, q_ref.shape[0], one_sequence, 0)


def _hgrn_step(a, hg_norm_l, state_hgrn, carried, layer, *, heads, dk):
    db = a["qh"].shape[0]
    w = heads * dk
    depth = state_hgrn.shape[0]
    bs = SAMPLE_ROWS_PER_STEP
    row = pl.BlockSpec((bs, 1, w), lambda i: (i, 0, 0))
    state = pl.BlockSpec((1, bs, heads, dk, dk), lambda i: (layer, i, 0, 0, 0))
    in_specs = [row, row, row, row, pl.BlockSpec((1, dk), lambda i: (0, 0)),
                pl.BlockSpec((None, bs, heads, dk, dk), lambda i: (layer, i, 0, 0, 0))]
    in_specs += [pl.BlockSpec(memory_space=pl.ANY)] * len(carried)
    return pl.pallas_call(
        functools.partial(_hgrn_step_kernel, heads=heads, dk=dk),
        grid=(db // bs,),
        out_shape=(jax.ShapeDtypeStruct((db, 1, w), BF16),
                   jax.ShapeDtypeStruct((depth, db, heads, dk, dk), F32)),
        in_specs=in_specs,
        out_specs=(row, state),
        input_output_aliases={len(in_specs) - 1: 1} if carried else {},
        compiler_params=_cparams(("arbitrary",)),
        name="hgrn_step",
    )(a["qh"], a["lf"], a["vh"], a["gh"], hg_norm_l.reshape(1, dk), state_hgrn, *carried)


def _da_prompt_kernel(q_ref, k_ref, vt_ref, bias_ref, gate_ref, nw_ref, lq1_ref, lk1_ref,
                      lq2_ref, lk2_ref, o_ref, acc_ref, s0_ref, s1_ref, p0_ref, p1_ref, al0_ref,
                      al1_ref, mb0_ref, mb1_ref, *, tq, tk, lam_init, hp, dv, pairs_per_trip):
    mb_refs = (mb0_ref, mb1_ref)
    i = pl.program_id(2)
    half = dv // 2
    row = lax.broadcasted_iota(jnp.int32, (dv, tq), 0)

    def qbd_of(hh):
        qt = q_ref[0, :, hh * dv:(hh + 1) * dv].astype(F32).T
        return jnp.concatenate([jnp.where(row < half, qt, 0.0), jnp.where(row >= half, qt, 0.0)],
                               axis=1).astype(BF16)

    qbd = [qbd_of(hh) for hh in range(hp)]
    s_refs, p_refs, al_refs = (s0_ref, s1_ref), (p0_ref, p1_ref), (al0_ref, al1_ref)
    acc_ref[...] = jnp.zeros_like(acc_ref)

    def scores(slot, blk):
        for hh in range(hp):
            k_blk = k_ref[0, pl.ds(pl.multiple_of(blk * tk, tk), tk), hh * dv:(hh + 1) * dv]
            s = _dot(k_blk, qbd[hh])
            s_refs[slot][hh] = s
            mb_refs[slot][hh] = jnp.max(s, axis=0, keepdims=True)

    def probs(slot, ms, bias_idx):
        out = []
        for hh in range(hp):
            s = s_refs[slot][hh]
            if bias_idx is None:
                m_blk = mb_refs[slot][hh]
            else:
                bias = bias_ref[hh, bias_idx]
                s = s + jnp.concatenate([bias, bias], axis=1)
                m_blk = jnp.max(s, axis=0, keepdims=True)
            m_new = jnp.maximum(ms[hh], m_blk)
            al_refs[slot][hh] = jnp.exp2(ms[hh] - m_new)
            p_refs[slot][hh] = jnp.exp2(s - m_new).astype(BF16)
            out.append(m_new)
        return tuple(out)

    def accumulate(slot, blk):
        kb = jnp.maximum(blk, 0)
        for hh in range(hp):
            vt = jnp.concatenate([vt_ref[0, hh, kb], jnp.ones((ONES_ROWS, tk), BF16)], axis=0)
            acc_ref[hh] = al_refs[slot][hh] * acc_ref[hh] + _dot(vt, p_refs[slot][hh])

    m = tuple(jnp.full((1, 2 * tq), 0.1 * NEG_BIG, F32) for _ in range(hp))

    def far_pair(e, ms):
        accumulate(1, e - 1)
        ms = probs(0, ms, None)
        scores(1, e + 1)
        accumulate(0, e)
        ms = probs(1, ms, None)
        scores(0, e + 2)
        return ms

    def far_pairs(u, ms):
        for v in range(pairs_per_trip):
            ms = far_pair(2 * (pairs_per_trip * u + v), ms)
        return ms

    def tail_odd(ms):
        accumulate(1, i - 2)
        ms = probs(0, ms, 0)
        scores(1, i)
        accumulate(0, i - 1)
        ms = probs(1, ms, 1)
        accumulate(1, i)
        return ms

    def tail_even(ms):
        accumulate(1, i - 3)
        ms = probs(0, ms, None)
        scores(1, i - 1)
        accumulate(0, i - 2)
        ms = probs(1, ms, 0)
        scores(0, i)
        accumulate(1, i - 1)
        ms = probs(0, ms, 1)
        accumulate(0, i)
        return ms

    def tail_zero(ms):
        ms = probs(0, ms, 1)
        accumulate(0, i)
        return ms

    p1_ref[...] = jnp.zeros_like(p1_ref)
    al1_ref[...] = jnp.ones_like(al1_ref)
    scores(0, 0)
    n_pairs = jnp.right_shift(jnp.maximum(i - 1, 0), 1)
    n_trips = n_pairs // pairs_per_trip
    m = lax.fori_loop(0, n_trips, far_pairs, m)
    if pairs_per_trip > 1:
        m = lax.fori_loop(n_trips * pairs_per_trip, n_pairs, lambda u, ms: far_pair(2 * u, ms), m)
    lax.cond(i == 0, tail_zero,
             lambda ms: lax.cond((i & 1) == 1, tail_odd, tail_even, ms), m)

    lam = _lam(lq1_ref, lk1_ref, lq2_ref, lk2_ref, lam_init)
    for hh in range(hp):
        acc = acc_ref[hh, :dv, :]
        inv = 1.0 / acc_ref[hh, dv:dv + 1, :]
        out_t = acc[:, :tq] * inv[:, :tq] - lam * (acc[:, tq:] * inv[:, tq:])
        out = _rms(out_t.T, nw_ref[...]) * (1.0 - lam_init)
        cols = slice(hh * dv, (hh + 1) * dv)
        o_ref[0, :, cols] = (out * gate_ref[0, :, cols].astype(F32)).astype(o_ref.dtype)


def _da_prompt(a, bias, da_norm_l, lam_rows, lam_init, *, heads, dv, pairs_per_trip):
    b, t, _ = a["dq"].shape
    tq, tk = DA_TQ, DA_TK
    hp = DA_HEADS_PER_STEP
    qspec = pl.BlockSpec((1, tq, hp * dv), lambda i, h, j: (i, j, h))
    vec = pl.BlockSpec((1, dv // 2), lambda i, h, j: (0, 0))
    kern = functools.partial(_da_prompt_kernel, tq=tq, tk=tk, lam_init=lam_init, hp=hp, dv=dv,
                             pairs_per_trip=pairs_per_trip)
    return pl.pallas_call(
        kern,
        grid=(b, heads // hp, t // tq),
        out_shape=jax.ShapeDtypeStruct((b, t, heads * dv), BF16),
        in_specs=[
            qspec,
            pl.BlockSpec((1, t, hp * dv), lambda i, h, j: (i, 0, h)),
            pl.BlockSpec((1, hp, t // tk, dv, tk), lambda i, h, j: (i, h, 0, 0, 0)),
            pl.BlockSpec((hp, 2, tk, tq), lambda i, h, j: (h, 0, 0, 0)),
            qspec,
            pl.BlockSpec((1, dv), lambda i, h, j: (0, 0)),
            vec, vec, vec, vec,
        ],
        out_specs=qspec,
        scratch_shapes=[pltpu.VMEM((hp, dv + ONES_ROWS, 2 * tq), F32),
                        pltpu.VMEM((hp, tk, 2 * tq), F32), pltpu.VMEM((hp, tk, 2 * tq), F32),
                        pltpu.VMEM((hp, tk, 2 * tq), BF16), pltpu.VMEM((hp, tk, 2 * tq), BF16),
                        pltpu.VMEM((hp, 1, 2 * tq), F32), pltpu.VMEM((hp, 1, 2 * tq), F32),
                        pltpu.VMEM((hp, 1, 2 * tq), F32), pltpu.VMEM((hp, 1, 2 * tq), F32)],
        compiler_params=_cparams(("arbitrary", "arbitrary", "arbitrary")),
        name="da_prompt",
    )(a["dq"], a["dkb"], a["dvt"], bias, a["dg"], da_norm_l.reshape(1, dv), *lam_rows)


def _da_sample_kernel(pt_ref, q_ref, kn_ref, vn_ref, gate_ref, bias_ref, bias0_ref, nw_ref,
                      lq1_ref, lk1_ref, lq2_ref, lk2_ref, *refs, heads, dv, n_pages, lam_init):
    del pt_ref
    k_refs = refs[:n_pages]
    v_refs = refs[n_pages:2 * n_pages]
    o_ref, m_ref, l_ref, acc_ref = refs[2 * n_pages:]
    j = pl.program_id(1)
    half = dv // 2
    rows = 2 * heads

    @pl.when(j == 0)
    def _():
        m_ref[...] = jnp.full_like(m_ref, 0.1 * NEG_BIG)
        l_ref[...] = jnp.zeros_like(l_ref)
        acc_ref[...] = jnp.zeros_like(acc_ref)

    def per_map(x):
        return jnp.concatenate([x[:, h * dv:(h + 1) * dv] for h in range(heads) for _ in range(2)],
                               axis=0)

    r_i = lax.broadcasted_iota(jnp.int32, (rows, dv), 0)
    c_i = lax.broadcasted_iota(jnp.int32, (rows, dv), 1)
    q8 = jnp.where(c_i // half == r_i % 2, per_map(q_ref[0].astype(F32)), 0.0).astype(BF16)

    s = jnp.concatenate([_dot_nt(q8, k_refs[g][0, 0].astype(BF16)) for g in range(n_pages)],
                        axis=1)
    s = s + bias_ref[...]
    m_old = m_ref[...]
    m_new = jnp.maximum(m_old, jnp.max(s, axis=-1, keepdims=True))
    alpha = jnp.exp2(m_old - m_new)
    p = jnp.exp2(s - m_new)
    l_ref[...] = alpha * l_ref[...] + jnp.sum(p, axis=-1, keepdims=True)
    m_ref[...] = m_new
    n_cols = k_refs[0].shape[2]
    pv = jnp.zeros((rows, dv), F32)
    for g in range(n_pages):
        pv = pv + _dot(p[:, g * n_cols:(g + 1) * n_cols].astype(BF16),
                       v_refs[g][0, 0].astype(BF16))
    acc_ref[...] = alpha * acc_ref[...] + pv

    @pl.when(j == pl.num_programs(1) - 1)
    def _():
        kn = per_map(kn_ref[0].astype(BF16).astype(F32))
        s_n = jnp.sum(q8.astype(F32) * kn, axis=-1, keepdims=True) + bias0_ref[...]
        m_o = m_ref[...]
        m_f = jnp.maximum(m_o, s_n)
        al = jnp.exp2(m_o - m_f)
        p_n = jnp.exp2(s_n - m_f)
        l_f = al * l_ref[...] + p_n
        vn = per_map(vn_ref[0].astype(BF16).astype(F32))
        z = (al * acc_ref[...] + p_n.astype(BF16).astype(F32) * vn) / l_f
        lam = _lam(lq1_ref, lk1_ref, lq2_ref, lk2_ref, lam_init)
        gate = gate_ref[0].astype(F32)
        for h in range(heads):
            sl = slice(h * dv, (h + 1) * dv)
            o_h = z[2 * h:2 * h + 1] - lam * z[2 * h + 1:2 * h + 2]
            o_h = _rms(o_h, nw_ref[...]) * (1.0 - lam_init)
            o_ref[0, :, sl] = (o_h * gate[:, sl]).astype(o_ref.dtype)


def _da_sample(a, cache_k, cache_v, page_table, layer, bias_rows, bias0, da_norm_l, lam_rows,
               lam_init, *, heads, dv):
    db = a["dq"].shape[0]
    width = heads * dv
    n_pages_seq = page_table.shape[1]
    page = cache_k.shape[2]
    g_n = PAGES_PER_STEP
    steps = n_pages_seq // g_n
    rows = 2 * heads
    row = pl.BlockSpec((1, 1, width), lambda i, j, pt: (i, 0, 0))
    vec = pl.BlockSpec((1, dv // 2), lambda i, j, pt: (0, 0))

    def page_spec(g):
        return pl.BlockSpec((1, 1, page * heads, dv),
                            lambda i, j, pt, g=g: (layer, pt[i, j * g_n + g], 0, 0))

    kern = functools.partial(_da_sample_kernel, heads=heads, dv=dv, n_pages=g_n, lam_init=lam_init)
    grid_spec = pltpu.PrefetchScalarGridSpec(
        num_scalar_prefetch=1,
        grid=(db, steps),
        in_specs=[row, row, row, row,
                  pl.BlockSpec((rows, g_n * page * heads), lambda i, j, pt: (0, j)),
                  pl.BlockSpec((rows, 1), lambda i, j, pt: (0, 0)),
                  pl.BlockSpec((1, dv), lambda i, j, pt: (0, 0)),
                  vec, vec, vec, vec]
                 + [page_spec(g) for g in range(g_n)] * 2,
        out_specs=row,
        scratch_shapes=[pltpu.VMEM((rows, 1), F32), pltpu.VMEM((rows, 1), F32),
                        pltpu.VMEM((rows, dv), F32)],
    )
    ck = cache_k.reshape(cache_k.shape[0], cache_k.shape[1], page * heads, dv)
    cv = cache_v.reshape(cache_v.shape[0], cache_v.shape[1], page * heads, dv)
    return pl.pallas_call(
        kern,
        grid_spec=grid_spec,
        out_shape=jax.ShapeDtypeStruct((db, 1, width), BF16),
        compiler_params=_cparams(("arbitrary", "arbitrary")),
        name="da_sample",
    )(page_table, a["dq"], a["dkf"], a["dvf"], a["dg"], bias_rows, bias0,
      da_norm_l.reshape(1, dv), *lam_rows, *([ck] * g_n), *([cv] * g_n))


def _out_kernel(x_ref, oh_ref, od_ref, *refs, heads, dh, hgw, daw, fused_mem):
    if fused_mem:
        mq_ref, mg_ref, mk_ref, mv_ref, w_ref, nw_ref, y_ref = refs
        mq = mq_ref[0]
        mg = mg_ref[0].astype(F32)
        parts = []
        for h in range(heads):
            sl = slice(h * dh, (h + 1) * dh)
            s = _dot_nt(mq[:, sl], mk_ref[0, :, sl].astype(BF16)) * (dh ** -0.5)
            p = jnp.exp(s - jnp.max(s, axis=-1, keepdims=True))
            o = _dot(p.astype(BF16), mv_ref[0, :, sl].astype(BF16))
            o = o / jnp.sum(p, axis=-1, keepdims=True)
            parts.append((o * mg[:, sl]).astype(BF16))
        om = jnp.concatenate(parts, axis=-1)
    else:
        om_ref, w_ref, nw_ref, y_ref = refs
        om = om_ref[0]
    y = _dot(oh_ref[0], w_ref[0, :hgw, :])
    y = y + _dot(od_ref[0], w_ref[0, hgw:hgw + daw, :])
    y = y + _dot(om, w_ref[0, hgw + daw:, :])
    y_ref[0] = x_ref[0] + _rms(y, nw_ref[0])


def _out_proj(x, oh, od, mem_args, layer, w_out_bf, norm_post, *, heads, dh, tm, fused_mem):
    b, t, d_model = x.shape
    depth, d_mix, _ = w_out_bf.shape
    hgw = oh.shape[-1]
    daw = od.shape[-1]
    mxw = heads * dh

    def act(width):
        return pl.BlockSpec((1, tm, width), lambda i, j: (i, j, 0))

    if fused_mem:
        mem = pl.BlockSpec((None, 1, mem_args[2].shape[2], mxw), lambda i, j: (layer, i, 0, 0))
        mem_specs = [act(mxw), act(mxw), mem, mem]
    else:
        mem_specs = [act(mxw)]
    return pl.pallas_call(
        functools.partial(_out_kernel, heads=heads, dh=dh, hgw=hgw, daw=daw, fused_mem=fused_mem),
        grid=(b, t // tm),
        out_shape=jax.ShapeDtypeStruct((b, t, d_model), F32),
        in_specs=[act(d_model), act(hgw), act(daw)] + mem_specs
                 + [pl.BlockSpec((1, d_mix, d_model), lambda i, j: (layer, 0, 0),
                                 pipeline_mode=pl.Buffered(1)),
                    pl.BlockSpec((1, 1, d_model), lambda i, j: (layer, 0, 0))],
        out_specs=act(d_model),
        compiler_params=_cparams(("arbitrary", "arbitrary")),
        name="out_proj" if fused_mem else "out_proj_rows",
    )(x, oh, od, *mem_args, w_out_bf, norm_post.reshape(depth, 1, d_model))


def _mem_sample_kernel(mq_ref, mg_ref, mk_ref, mv_ref, om_ref, *, heads, dh):
    pad = 8

    def one_sequence(n, carry):
        q = mq_ref[n]
        q_rows = jnp.concatenate([q[:, h * dh:(h + 1) * dh] for h in range(heads)]
                                 + [jnp.zeros((pad - heads, dh), BF16)], axis=0)
        s = _dot_nt(q_rows, mk_ref[n].astype(BF16)) * (dh ** -0.5)
        r_i = lax.broadcasted_iota(jnp.int32, s.shape, 0)
        c_i = lax.broadcasted_iota(jnp.int32, s.shape, 1)
        s = jnp.where(c_i % heads == r_i, s, NEG_BIG)
        p = jnp.exp(s - jnp.max(s, axis=-1, keepdims=True))
        o = _dot(p.astype(BF16), mv_ref[n].astype(BF16)) / jnp.sum(p, axis=-1, keepdims=True)
        mg = mg_ref[n].astype(F32)
        for h in range(heads):
            sl = slice(h * dh, (h + 1) * dh)
            om_ref[n, :, sl] = (o[h:h + 1] * mg[:, sl]).astype(om_ref.dtype)
        return carry

    lax.fori_loop(0, mq_ref.shape[0], one_sequence, 0)


def _mem_sample(a, cmk, cmv, layer, *, heads, dh):
    db = a["mq"].shape[0]
    width = heads * dh
    rows = cmk.shape[2]
    bs = SAMPLE_ROWS_PER_STEP
    row = pl.BlockSpec((bs, 1, width), lambda i: (i, 0, 0))
    mem = pl.BlockSpec((None, bs, rows, dh), lambda i: (layer, i, 0, 0))
    return pl.pallas_call(
        functools.partial(_mem_sample_kernel, heads=heads, dh=dh),
        grid=(db // bs,),
        out_shape=jax.ShapeDtypeStruct((db, 1, width), BF16),
        in_specs=[row, row, mem, mem],
        out_specs=row,
        compiler_params=_cparams(("arbitrary",)),
        name="mem_sample",
    )(a["mq"], a["mg"], cmk, cmv)


def kernel(x_prompt, x_sample, mem_prompt, cache_da_k, cache_da_v, cache_mem_k, cache_mem_v,
           state_hgrn, page_table, w_in, w_out, w_mem_kv, norm_pre, norm_post, mem_norm, hg_norm,
           da_norm, hg_lb, da_lq1, da_lk1, da_lq2, da_lk2, rel_bias):
    depth = w_in.shape[0]
    b, t, d_model = x_prompt.shape
    db = x_sample.shape[0]
    hg_heads, hg_dk = state_hgrn.shape[2], state_hgrn.shape[3]
    hgw = hg_heads * hg_dk
    da_heads, dv = cache_da_v.shape[3], cache_da_v.shape[4]
    daw = da_heads * dv
    mx_heads, mx_dh = cache_mem_k.shape[3], cache_mem_k.shape[4]
    mxw = mx_heads * mx_dh
    n_mem = mem_prompt.shape[1]
    page = cache_da_k.shape[2]
    past = page_table.shape[1] * page
    assert w_in.shape[2] == 4 * hgw + 4 * daw + 2 * mxw
    assert x_sample.shape[1] == 1 and MAX_DIST <= page <= DA_TK
    assert t % 1024 == 0 and page_table.shape[1] % PAGES_PER_STEP == 0
    assert db % SAMPLE_ROWS_PER_STEP == 0 and hg_heads % HG_HEADS_PER_STEP == 0

    w_in_bf = w_in.astype(BF16)
    w_out_bf = w_out.astype(BF16)
    w_mem_bf = w_mem_kv.astype(BF16)

    bias = _bias_tiles(rel_bias, da_heads)
    near = bias[:, 0, DA_TK - page:, 0]
    bias_rows = jnp.concatenate([jnp.zeros((da_heads, past - page), F32), near], axis=1)
    bias_rows = jnp.repeat(bias_rows, 2, axis=0)
    col_head = jnp.arange(past * da_heads, dtype=jnp.int32) % da_heads
    row_head = jnp.arange(2 * da_heads, dtype=jnp.int32) // 2
    bias_rows = jnp.where(col_head[None, :] == row_head[:, None],
                          jnp.repeat(bias_rows, da_heads, axis=1), NEG_BIG)
    bias0 = jnp.repeat(bias[:, 1, 0, 0], 2).reshape(2 * da_heads, 1)

    mk_p, mv_p = _mem_kv(mem_prompt, mem_norm, w_mem_bf)
    cmk = cache_mem_k.reshape(depth, db, n_mem * mx_heads, mx_dh)
    cmv = cache_mem_v.reshape(depth, db, n_mem * mx_heads, mx_dh)

    xp, xs = x_prompt, x_sample.reshape(1, db, d_model)
    outs = {n: [] for n in ("sp", "ks", "vs")}
    kv_all = ()
    s_s = ()
    dims = dict(hgw=hgw, daw=daw, mxw=mxw, da_heads=da_heads)
    for l in range(depth):
        lam_init = 0.8 - 0.6 * math.exp(-0.3 * l)
        lam_rows = [v[l].reshape(1, dv // 2) for v in (da_lq1, da_lk1, da_lq2, da_lk2)]

        a = _in_proj(xp, l, norm_pre, w_in_bf, hg_lb, kv_all, tm=512, emit_vt=True, **dims)
        kv_all = (a["dkf"], a["dvf"])
        oh, s_p = _hgrn_prompt(a, hg_norm[l], heads=hg_heads, dk=hg_dk, tt=512,
                               group=HG_GROUP, unroll=HG_UNROLL, hp=HG_HEADS_PER_STEP)
        od = _da_prompt(a, bias, da_norm[l], lam_rows, lam_init, heads=da_heads, dv=dv,
                        pairs_per_trip=DA_PAIRS_PER_TRIP)
        xp = _out_proj(xp, oh, od, (a["mq"], a["mg"], mk_p, mv_p), l, w_out_bf, norm_post,
                       heads=mx_heads, dh=mx_dh, tm=1024, fused_mem=True)
        outs["sp"].append(s_p)

        a = _in_proj(xs, l, norm_pre, w_in_bf, hg_lb, (), tm=db, emit_vt=False, **dims)
        a = {n: v.reshape(db, 1, v.shape[-1]) for n, v in a.items()}
        oh, s_s = _hgrn_step(a, hg_norm[l], state_hgrn, s_s, l, heads=hg_heads, dk=hg_dk)
        s_s = (s_s,)
        od = _da_sample(a, cache_da_k, cache_da_v, page_table, l, bias_rows, bias0, da_norm[l],
                        lam_rows, lam_init, heads=da_heads, dv=dv)
        om = _mem_sample(a, cmk, cmv, l, heads=mx_heads, dh=mx_dh)
        as_rows = lambda v: v.reshape(1, db, v.shape[-1])
        xs = _out_proj(xs, as_rows(oh), as_rows(od), (as_rows(om),), l, w_out_bf, norm_post,
                       heads=mx_heads, dh=mx_dh, tm=db, fused_mem=False)
        outs["ks"].append(a["dkf"].reshape(db, 1, da_heads, dv))
        outs["vs"].append(a["dvf"].reshape(db, 1, da_heads, dv))

    return (xp, xs.reshape(db, 1, d_model), kv_all[0], kv_all[1],
            jnp.stack(outs["sp"]), mk_p.reshape(depth, b, n_mem, mx_heads, mx_dh),
            mv_p.reshape(depth, b, n_mem, mx_heads, mx_dh), jnp.stack(outs["ks"]),
            jnp.stack(outs["vs"]), s_s[0])
```

```python
import functools
import math

import jax
import jax.numpy as jnp
from jax import lax
from jax.experimental import pallas as pl
from jax.experimental.pallas import tpu as pltpu

F32 = jnp.float32
BF16 = jnp.bfloat16

EPS = 1e-6
MAX_DIST = 128
HG_CHUNK = 64
HG_SUB = 8
HG_UNROLL = 8
HG_GROUP = 2
HG_HEADS_PER_STEP = 8
DA_HEADS_PER_STEP = 2
DA_PAIRS_PER_TRIP = 4
IN_ROW_SPLIT = 2
LOG2E = math.log2(math.e)
ONES_ROWS = 16
DA_TQ = 256
DA_TK = 256
PAGES_PER_STEP = 16
SAMPLE_ROWS_PER_STEP = 4
NEG_BIG = -1e30
VMEM_LIMIT_BYTES = 56 * 1024 * 1024


def _cparams(sem):
    return pltpu.CompilerParams(dimension_semantics=sem, vmem_limit_bytes=VMEM_LIMIT_BYTES)


def _sigmoid(x):
    return 1.0 / (1.0 + jnp.exp(-x))


def _silu(x):
    return x * _sigmoid(x)


def _rms(x, w):
    return x * lax.rsqrt(jnp.mean(x * x, axis=-1, keepdims=True) + EPS) * w


def _dot(a, b):
    return jnp.dot(a, b, preferred_element_type=F32)


def _dot_nt(a, b):
    return lax.dot_general(a, b, (((1,), (1,)), ((), ())), preferred_element_type=F32)


def _dot_tn(a, b):
    return lax.dot_general(a, b, (((0,), (0,)), ((), ())), preferred_element_type=F32)


def _lam(lq1_ref, lk1_ref, lq2_ref, lk2_ref, lam_init):
    a = jnp.sum(lq1_ref[...] * lk1_ref[...], axis=-1, keepdims=True)
    b = jnp.sum(lq2_ref[...] * lk2_ref[...], axis=-1, keepdims=True)
    return jnp.exp(a) - jnp.exp(b) + lam_init


def _bias_kernel(tab_ref, out_ref, *, n_buckets, heads, tk, tq):
    kk = lax.broadcasted_iota(jnp.int32, (tk, tq), 0)
    qq = lax.broadcasted_iota(jnp.int32, (tk, tq), 1)
    max_exact = n_buckets // 2
    for jj in range(2):
        d = qq - kk + (tk if jj == 0 else 0)
        n = jnp.maximum(d, 0)
        nf = jnp.maximum(n, 1).astype(F32)
        large = max_exact + (jnp.log(nf / max_exact) / math.log(MAX_DIST / max_exact)
                             * (n_buckets - max_exact)).astype(jnp.int32)
        large = jnp.minimum(large, n_buckets - 1)
        bucket = jnp.where(n < max_exact, n, large)
        for h in range(heads):
            val = jnp.zeros((tk, tq), F32)
            for b in range(n_buckets):
                val = jnp.where(bucket == b, tab_ref[b, h], val)
            val = (val - tab_ref[n_buckets - 1, h]) * LOG2E
            out_ref[h, jj] = jnp.where(d < 0, NEG_BIG, val)


def _bias_tiles(rel_bias, heads):
    n_buckets = rel_bias.shape[0]
    return pl.pallas_call(
        functools.partial(_bias_kernel, n_buckets=n_buckets, heads=heads, tk=DA_TK, tq=DA_TQ),
        out_shape=jax.ShapeDtypeStruct((heads, 2, DA_TK, DA_TQ), F32),
        in_specs=[pl.BlockSpec(memory_space=pltpu.SMEM)],
        out_specs=pl.BlockSpec(memory_space=pltpu.VMEM),
        name="t5_bias",
    )(rel_bias)


def _memkv_kernel(mem_ref, nw_ref, w_ref, k_ref, v_ref, *, width):
    xn = _rms(mem_ref[0], nw_ref[0]).astype(BF16)
    y = _dot(xn, w_ref[0])
    k_ref[0, 0] = y[:, :width]
    v_ref[0, 0] = y[:, width:]


def _mem_kv(mem_prompt, mem_norm, w_mem_kv_bf):
    depth, d_model, two_w = w_mem_kv_bf.shape
    width = two_w // 2
    b, n_mem, _ = mem_prompt.shape
    out = jax.ShapeDtypeStruct((depth, b, n_mem, width), F32)
    return pl.pallas_call(
        functools.partial(_memkv_kernel, width=width),
        grid=(depth, b),
        out_shape=(out, out),
        in_specs=[
            pl.BlockSpec((1, n_mem, d_model), lambda l, i: (i, 0, 0)),
            pl.BlockSpec((1, 1, d_model), lambda l, i: (l, 0, 0)),
            pl.BlockSpec((1, d_model, two_w), lambda l, i: (l, 0, 0)),
        ],
        out_specs=(pl.BlockSpec((1, 1, n_mem, width), lambda l, i: (l, i, 0, 0)),
                   pl.BlockSpec((1, 1, n_mem, width), lambda l, i: (l, i, 0, 0))),
        compiler_params=_cparams(("arbitrary", "arbitrary")),
        name="mem_kv",
    )(mem_prompt, mem_norm.reshape(depth, 1, d_model), w_mem_kv_bf)


def _inproj_kernel(x_ref, nw_ref, w_ref, lbp_ref, *refs, names, n_carried, layer, hgw, daw, mxw,
                   da_heads, tkb, emit_vt, col_chunk, row_split):
    r = dict(zip(names, refs[n_carried:]))

    p = lbp_ref[...]
    e = jnp.exp(p - jnp.max(p, axis=0, keepdims=True))
    sm = e / jnp.sum(e, axis=0, keepdims=True)
    lb = jnp.zeros((1, hgw), F32)
    for row in range(1, layer + 1):
        lb = lb + sm[row:row + 1, :]

    dv = daw // da_heads
    dqk = dv // 2

    def do_rows(r0, r1):
        rows = slice(r0, r1)
        xn = _rms(x_ref[0, rows], nw_ref[0]).astype(BF16)

        def seg(c0, width, fn):
            for c in range(0, width, col_chunk):
                w = min(col_chunk, width - c)
                fn(c, w, _dot(xn, w_ref[0, :, c0 + c:c0 + c + w]))

        def st(ref, fn):
            def go(c, w, y):
                ref[0, rows, c:c + w] = fn(y).astype(ref.dtype)
            return go

        def forget(c, w, y):
            f = lb[:, c:c + w] + (1.0 - lb[:, c:c + w]) * _sigmoid(y)
            r["lf"][0, rows, c:c + w] = jnp.log2(f)
            r["kh"][0, rows, c:c + w] = (1.0 - f).astype(BF16)

        def st_f32(ref, c, w, y):
            if emit_vt:
                for hh in range(c // dv, (c + w) // dv):
                    ref[0, 0, rows, hh, :] = y[:, hh * dv - c:(hh + 1) * dv - c]
            else:
                ref[0, rows, c:c + w] = y

        def da_k(c, w, y):
            st_f32(r["dkf"], c, w, y)
            r["dkb"][0, rows, c:c + w] = y.astype(BF16)

        def da_v(c, w, y):
            st_f32(r["dvf"], c, w, y)
            if emit_vt:
                piece = min(tkb, r1 - r0)
                for hh in range(c // dv, (c + w) // dv):
                    for q0 in range(r0, r1, piece):
                        blk = y[q0 - r0:q0 - r0 + piece, hh * dv - c:(hh + 1) * dv - c]
                        r["dvt"][0, hh, q0 // tkb, :, q0 % tkb:q0 % tkb + piece] = (
                            blk.T.astype(BF16))

        o = 0
        seg(o, hgw, st(r["qh"], _silu)); o += hgw
        seg(o, hgw, forget); o += hgw
        seg(o, hgw, st(r["vh"], lambda y: y)); o += hgw
        seg(o, hgw, st(r["gh"], _silu)); o += hgw
        seg(o, daw, st(r["dq"], lambda y: y * (dqk ** -0.5 * LOG2E))); o += daw
        seg(o, daw, da_k); o += daw
        seg(o, daw, da_v); o += daw
        seg(o, daw, st(r["dg"], _silu)); o += daw
        seg(o, mxw, st(r["mq"], lambda y: y)); o += mxw
        seg(o, mxw, st(r["mg"], _silu)); o += mxw

    tm = x_ref.shape[1]
    for part in range(row_split):
        do_rows(part * (tm // row_split), (part + 1) * (tm // row_split))


def _in_proj(x, layer, norm_pre, w_bf, hg_lb, carried, *, hgw, daw, mxw, da_heads, tm, emit_vt):
    b, t, d_model = x.shape
    depth = w_bf.shape[0]
    tkb = DA_TK
    dv = daw // da_heads

    def act(width, dtype):
        return (jax.ShapeDtypeStruct((b, t, width), dtype),
                pl.BlockSpec((1, tm, width), lambda i, j: (i, j, 0)))

    def rows_f32():
        if not emit_vt:
            return act(daw, F32)
        return (jax.ShapeDtypeStruct((depth, b, t, da_heads, dv), F32),
                pl.BlockSpec((1, 1, tm, da_heads, dv), lambda i, j: (layer, i, j, 0, 0)))

    outs = dict(qh=act(hgw, BF16), kh=act(hgw, BF16), vh=act(hgw, BF16), lf=act(hgw, F32),
                gh=act(hgw, BF16), dq=act(daw, BF16), dkf=rows_f32(), dkb=act(daw, BF16),
                dvf=rows_f32())
    if emit_vt:
        outs["dvt"] = (jax.ShapeDtypeStruct((b, da_heads, t // tkb, dv, tkb), BF16),
                       pl.BlockSpec((1, da_heads, tm // tkb, dv, tkb),
                                    lambda i, j: (i, 0, j, 0, 0)))
    outs.update(dg=act(daw, BF16), mq=act(mxw, BF16), mg=act(mxw, BF16))
    names = tuple(outs)
    in_specs = [
        pl.BlockSpec((1, tm, d_model), lambda i, j: (i, j, 0)),
        pl.BlockSpec((1, 1, d_model), lambda i, j: (layer, 0, 0)),
        pl.BlockSpec((1,) + w_bf.shape[1:], lambda i, j: (layer, 0, 0),
                     pipeline_mode=pl.Buffered(1)),
        pl.BlockSpec((depth, hgw), lambda i, j: (0, 0)),
    ] + [pl.BlockSpec(memory_space=pl.ANY)] * len(carried)
    aliases = {len(in_specs) - len(carried) + n: names.index(name)
               for n, name in enumerate(("dkf", "dvf")[:len(carried)])}
    kern = functools.partial(_inproj_kernel, names=names, n_carried=len(carried), layer=layer,
                             hgw=hgw, daw=daw, mxw=mxw, da_heads=da_heads, tkb=tkb,
                             emit_vt=emit_vt, col_chunk=512,
                             row_split=IN_ROW_SPLIT if emit_vt else 1)
    res = pl.pallas_call(
        kern,
        grid=(b, t // tm),
        out_shape=tuple(outs[n][0] for n in names),
        in_specs=in_specs,
        out_specs=tuple(outs[n][1] for n in names),
        input_output_aliases=aliases,
        compiler_params=_cparams(("arbitrary", "arbitrary")),
        name="in_proj_vt" if emit_vt else "in_proj",
    )(x, norm_pre.reshape(depth, 1, d_model), w_bf, hg_lb, *carried)
    return dict(zip(names, res))


def _hgrn_off_shape(chunk, sub):
    nsub = chunk // sub
    n_off_cols = sub * (nsub * (nsub - 1) // 2)
    return chunk - sub, n_off_cols, -(-n_off_cols // 128) * 128


def _hgrn_kernel(q_ref, k_ref, v_ref, g_ref, gate_ref, nw_ref, o_ref, s_ref, st_ref, tri_ref,
                 offm_ref, diagm_ref, cum_ref, qe_ref, dec_ref, kv_ref, oi_ref, *, chunk, sub,
                 n_chunks, unroll, group, hp, dk):
    j = pl.program_id(2)
    nsub = chunk // sub
    n_off_rows, n_off_cols, n_off_pad = _hgrn_off_shape(chunk, sub)

    @pl.when(j == 0)
    def _():
        st_ref[...] = jnp.zeros_like(st_ref)
        r_i = lax.broadcasted_iota(jnp.int32, (chunk, chunk), 0)
        c_i = lax.broadcasted_iota(jnp.int32, (chunk, chunk), 1)
        tri_ref[...] = jnp.where(c_i <= r_i, 1.0, 0.0).astype(BF16)

        ro = lax.broadcasted_iota(jnp.int32, (n_off_rows, n_off_pad), 0) // sub + 1
        co = lax.broadcasted_iota(jnp.int32, (n_off_rows, n_off_pad), 1)
        cblk = jnp.zeros_like(co)
        start = 0
        for i in range(1, nsub):
            cblk = jnp.where((co >= start) & (co < start + i * sub), i, cblk)
            start += i * sub
        offm_ref[...] = jnp.where(ro == cblk, 1.0, 0.0)

        rd = lax.broadcasted_iota(jnp.int32, (nsub * sub * sub, chunk), 0)
        cd = lax.broadcasted_iota(jnp.int32, (nsub * sub * sub, chunk), 1)
        r_is = rd // sub
        diagm_ref[...] = jnp.where((r_is == cd) & (rd % sub >= r_is % sub), 1.0, 0.0)

    g = jnp.concatenate([g_ref[0, c * chunk:(c + 1) * chunk, :] for c in range(n_chunks)], axis=1)
    g1 = g.astype(BF16)
    rem = g - g1.astype(F32)
    g2 = rem.astype(BF16)
    g3 = (rem - g2.astype(F32)).astype(BF16)
    tri = tri_ref[...]
    cum_all = _dot(tri, g1) + _dot(tri, g2) + _dot(tri, g3)
    for c in range(n_chunks * hp):
        cum_ref[c] = cum_all[:, c * dk:(c + 1) * dk]

    def stage_a(ci, hh):
        r0 = pl.multiple_of(ci * chunk, chunk)
        cols = slice(hh * dk, (hh + 1) * dk)
        item = ci * hp + hh
        q = q_ref[0, pl.ds(r0, chunk), cols].astype(F32)
        k_bf = k_ref[0, pl.ds(r0, chunk), cols]
        k = k_bf.astype(F32)
        v = v_ref[0, pl.ds(r0, chunk), cols]
        cum = cum_ref[item]

        last = cum[chunk - 1:chunk, :]
        qe_ref[item] = (q * jnp.exp2(cum)).astype(BF16)
        dec_ref[item] = jnp.exp2(last)
        kv = _dot_tn(v, (k * jnp.exp2(last - cum)).astype(BF16))

        q_parts, k_parts, v_parts = [], [], []
        v32 = v.astype(F32)
        for i in range(1, nsub):
            b_i = cum[i * sub - 1:i * sub, :]
            lo, hi = i * sub, (i + 1) * sub
            q_parts.append(q[lo:hi] * jnp.exp2(cum[lo:hi] - b_i))
            k_parts.append(k[:lo] * jnp.exp2(b_i - cum[:lo]))
            v_parts.append(v32[:lo])
        if n_off_pad > n_off_cols:
            k_parts.append(jnp.zeros((n_off_pad - n_off_cols, dk), F32))
            v_parts.append(jnp.zeros((n_off_pad - n_off_cols, v.shape[-1]), F32))
        a_off = _dot_nt(jnp.concatenate(q_parts, axis=0).astype(BF16),
                        jnp.concatenate(k_parts, axis=0).astype(BF16))
        v_off = jnp.concatenate(v_parts, axis=0).astype(BF16)

        n_parts = []
        for i in range(nsub):
            lo, hi = i * sub, (i + 1) * sub
            c_blk = cum[lo:hi]
            q_blk = q[lo:hi]
            for s in range(sub):
                dec = jnp.exp2(jnp.minimum(c_blk - c_blk[s:s + 1, :], 0.0))
                n_parts.append(q_blk * dec)
        r = _dot_nt(jnp.concatenate(n_parts, axis=0).astype(BF16), k_bf)
        return item, kv, a_off, v_off, r, v

    def stage_b(item, kv, a_off, v_off, r, v):
        kv_ref[item] = kv
        a_off = (a_off * offm_ref[...]).astype(BF16)
        o_off = _dot(a_off, v_off)
        r = r * diagm_ref[...]
        a_parts = []
        for i in range(nsub):
            base = i * sub * sub
            acc = r[base:base + sub]
            for s in range(1, sub):
                acc = acc + r[base + s * sub:base + (s + 1) * sub]
            a_parts.append(acc)
        a_diag = jnp.concatenate(a_parts, axis=0).astype(BF16)
        return item, o_off, _dot(a_diag, v)

    def stage_c(item, o_off, o_diag):
        oi_ref[item] = o_diag + jnp.concatenate([jnp.zeros((sub, dk), F32), o_off], axis=0)

    def intra(gi, carry):
        items = [(gi * group + u // hp, u % hp) for u in range(group * hp)]
        mid = [stage_a(ci, hh) for ci, hh in items]
        for x in [stage_b(*x) for x in mid]:
            stage_c(*x)
        return carry

    lax.fori_loop(0, n_chunks // group, intra, 0)

    def inter(ci, carry):
        r0 = pl.multiple_of(ci * chunk, chunk)
        for hh in range(hp):
            cols = slice(hh * dk, (hh + 1) * dk)
            item = ci * hp + hh
            s_prev = st_ref[hh]
            o = oi_ref[item] + _dot_nt(qe_ref[item], s_prev.astype(BF16))
            st_ref[hh] = s_prev * dec_ref[item] + kv_ref[item]
            gate = gate_ref[0, pl.ds(r0, chunk), cols].astype(F32)
            o_ref[0, pl.ds(r0, chunk), cols] = (_rms(o, nw_ref[...]) * gate).astype(o_ref.dtype)
        return carry

    lax.fori_loop(0, n_chunks, inter, 0, unroll=unroll)

    @pl.when(j == pl.num_programs(2) - 1)
    def _():
        for hh in range(hp):
            s_ref[0, hh] = st_ref[hh].T


def _hgrn_prompt(a, hg_norm_l, *, heads, dk, tt, group, unroll, hp):
    b, t, _ = a["qh"].shape
    spec = pl.BlockSpec((1, tt, hp * dk), lambda i, h, j: (i, j, h))
    chunk, sub = HG_CHUNK, HG_SUB
    n_chunks = tt // chunk
    n_off_rows, _, n_off_pad = _hgrn_off_shape(chunk, sub)
    kern = functools.partial(_hgrn_kernel, chunk=chunk, sub=sub, n_chunks=n_chunks,
                             unroll=unroll, group=group, hp=hp, dk=dk)
    n_items = n_chunks * hp
    return pl.pallas_call(
        kern,
        grid=(b, heads // hp, t // tt),
        out_shape=(jax.ShapeDtypeStruct((b, t, heads * dk), BF16),
                   jax.ShapeDtypeStruct((b, heads, dk, dk), F32)),
        in_specs=[spec, spec, spec, spec, spec, pl.BlockSpec((1, dk), lambda i, h, j: (0, 0))],
        out_specs=(spec, pl.BlockSpec((1, hp, dk, dk), lambda i, h, j: (i, h, 0, 0))),
        scratch_shapes=[pltpu.VMEM((hp, dk, dk), F32), pltpu.VMEM((chunk, chunk), BF16),
                        pltpu.VMEM((n_off_rows, n_off_pad), F32),
                        pltpu.VMEM((chunk * sub, chunk), F32),
                        pltpu.VMEM((n_items, chunk, dk), F32),
                        pltpu.VMEM((n_items, chunk, dk), BF16),
                        pltpu.VMEM((n_items, 1, dk), F32),
                        pltpu.VMEM((n_items, dk, dk), F32),
                        pltpu.VMEM((n_items, chunk, dk), F32)],
        compiler_params=_cparams(("arbitrary", "arbitrary", "arbitrary")),
        name="hgrn_prompt",
    )(a["qh"], a["kh"], a["vh"], a["lf"], a["gh"], hg_norm_l.reshape(1, dk))


def _hgrn_step_kernel(q_ref, lf_ref, v_ref, gate_ref, nw_ref, s_ref, *refs, heads, dk):
    o_ref, sn_ref = refs[-2:]

    def col(row):
        return jnp.broadcast_to(row, (dk, dk)).T

    def one_sequence(s, carry):
        for h in range(heads):
            sl = slice(h * dk, (h + 1) * dk)
            f = jnp.exp2(lf_ref[s, :, sl])
            s_new = col(f) * s_ref[s, h] + col(1.0 - f) * v_ref[s, :, sl].astype(F32)
            sn_ref[0, s, h] = s_new
            o = jnp.sum(col(q_ref[s, :, sl].astype(F32)) * s_new, axis=0, keepdims=True)
            o = _rms(o, nw_ref[...]) * gate_ref[s, :, sl].astype(F32)
            o_ref[s, :, sl] = o.astype(o_ref.dtype)
        return carry

    lax.fori_loop(0, q_ref.shape[0], one_sequence, 0)


def _hgrn_step(a, hg_norm_l, state_hgrn, carried, layer, *, heads, dk):
    db = a["qh"].shape[0]
    w = heads * dk
    depth = state_hgrn.shape[0]
    bs = SAMPLE_ROWS_PER_STEP
    row = pl.BlockSpec((bs, 1, w), lambda i: (i, 0, 0))
    state = pl.BlockSpec((1, bs, heads, dk, dk), lambda i: (layer, i, 0, 0, 0))
    in_specs = [row, row, row, row, pl.BlockSpec((1, dk), lambda i: (0, 0)),
                pl.BlockSpec((None, bs, heads, dk, dk), lambda i: (layer, i, 0, 0, 0))]
    in_specs += [pl.BlockSpec(memory_space=pl.ANY)] * len(carried)
    return pl.pallas_call(
        functools.partial(_hgrn_step_kernel, heads=heads, dk=dk),
        grid=(db // bs,),
        out_shape=(jax.ShapeDtypeStruct((db, 1, w), BF16),
                   jax.ShapeDtypeStruct((depth, db, heads, dk, dk), F32)),
        in_specs=in_specs,
        out_specs=(row, state),
        input_output_aliases={len(in_specs) - 1: 1} if carried else {},
        compiler_params=_cparams(("arbitrary",)),
        name="hgrn_step",
    )(a["qh"], a["lf"], a["vh"], a["gh"], hg_norm_l.reshape(1, dk), state_hgrn, *carried)


def _da_prompt_kernel(q_ref, k_ref, vt_ref, bias_ref, gate_ref, nw_ref, lq1_ref, lk1_ref,
                      lq2_ref, lk2_ref, o_ref, acc_ref, s0_ref, s1_ref, p0_ref, p1_ref, al0_ref,
                      al1_ref, mb0_ref, mb1_ref, *, tq, tk, lam_init, hp, dv, pairs_per_trip):
    mb_refs = (mb0_ref, mb1_ref)
    i = pl.program_id(2)
    half = dv // 2
    row = lax.broadcasted_iota(jnp.int32, (dv, tq), 0)

    def qbd_of(hh):
        qt = q_ref[0, :, hh * dv:(hh + 1) * dv].astype(F32).T
        return jnp.concatenate([jnp.where(row < half, qt, 0.0), jnp.where(row >= half, qt, 0.0)],
                               axis=1).astype(BF16)

    qbd = [qbd_of(hh) for hh in range(hp)]
    s_refs, p_refs, al_refs = (s0_ref, s1_ref), (p0_ref, p1_ref), (al0_ref, al1_ref)
    acc_ref[...] = jnp.zeros_like(acc_ref)

    def scores(slot, blk):
        for hh in range(hp):
            k_blk = k_ref[0, pl.ds(pl.multiple_of(blk * tk, tk), tk), hh * dv:(hh + 1) * dv]
            s = _dot(k_blk, qbd[hh])
            s_refs[slot][hh] = s
            mb_refs[slot][hh] = jnp.max(s, axis=0, keepdims=True)

    def probs(slot, ms, bias_idx):
        out = []
        for hh in range(hp):
            s = s_refs[slot][hh]
            if bias_idx is None:
                m_blk = mb_refs[slot][hh]
            else:
                bias = bias_ref[hh, bias_idx]
                s = s + jnp.concatenate([bias, bias], axis=1)
                m_blk = jnp.max(s, axis=0, keepdims=True)
            m_new = jnp.maximum(ms[hh], m_blk)
            al_refs[slot][hh] = jnp.exp2(ms[hh] - m_new)
            p_refs[slot][hh] = jnp.exp2(s - m_new).astype(BF16)
            out.append(m_new)
        return tuple(out)

    def accumulate(slot, blk):
        kb = jnp.maximum(blk, 0)
        for hh in range(hp):
            vt = jnp.concatenate([vt_ref[0, hh, kb], jnp.ones((ONES_ROWS, tk), BF16)], axis=0)
            acc_ref[hh] = al_refs[slot][hh] * acc_ref[hh] + _dot(vt, p_refs[slot][hh])

    m = tuple(jnp.full((1, 2 * tq), 0.1 * NEG_BIG, F32) for _ in range(hp))

    def far_pair(e, ms):
        accumulate(1, e - 1)
        ms = probs(0, ms, None)
        scores(1, e + 1)
        accumulate(0, e)
        ms = probs(1, ms, None)
        scores(0, e + 2)
        return ms

    def far_pairs(u, ms):
        for v in range(pairs_per_trip):
            ms = far_pair(2 * (pairs_per_trip * u + v), ms)
        return ms

    def tail_odd(ms):
        accumulate(1, i - 2)
        ms = probs(0, ms, 0)
        scores(1, i)
        accumulate(0, i - 1)
        ms = probs(1, ms, 1)
        accumulate(1, i)
        return ms

    def tail_even(ms):
        accumulate(1, i - 3)
        ms = probs(0, ms, None)
        scores(1, i - 1)
        accumulate(0, i - 2)
        ms = probs(1, ms, 0)
        scores(0, i)
        accumulate(1, i - 1)
        ms = probs(0, ms, 1)
        accumulate(0, i)
        return ms

    def tail_zero(ms):
        ms = probs(0, ms, 1)
        accumulate(0, i)
        return ms

    p1_ref[...] = jnp.zeros_like(p1_ref)
    al1_ref[...] = jnp.ones_like(al1_ref)
    scores(0, 0)
    n_pairs = jnp.right_shift(jnp.maximum(i - 1, 0), 1)
    n_trips = n_pairs // pairs_per_trip
    m = lax.fori_loop(0, n_trips, far_pairs, m)
    if pairs_per_trip > 1:
        m = lax.fori_loop(n_trips * pairs_per_trip, n_pairs, lambda u, ms: far_pair(2 * u, ms), m)
    lax.cond(i == 0, tail_zero,
             lambda ms: lax.cond((i & 1) == 1, tail_odd, tail_even, ms), m)

    lam = _lam(lq1_ref, lk1_ref, lq2_ref, lk2_ref, lam_init)
    for hh in range(hp):
        acc = acc_ref[hh, :dv, :]
        inv = 1.0 / acc_ref[hh, dv:dv + 1, :]
        out_t = acc[:, :tq] * inv[:, :tq] - lam * (acc[:, tq:] * inv[:, tq:])
        out = _rms(out_t.T, nw_ref[...]) * (1.0 - lam_init)
        cols = slice(hh * dv, (hh + 1) * dv)
        o_ref[0, :, cols] = (out * gate_ref[0, :, cols].astype(F32)).astype(o_ref.dtype)


def _da_prompt(a, bias, da_norm_l, lam_rows, lam_init, *, heads, dv, pairs_per_trip):
    b, t, _ = a["dq"].shape
    tq, tk = DA_TQ, DA_TK
    hp = DA_HEADS_PER_STEP
    qspec = pl.BlockSpec((1, tq, hp * dv), lambda i, h, j: (i, j, h))
    vec = pl.BlockSpec((1, dv // 2), lambda i, h, j: (0, 0))
    kern = functools.partial(_da_prompt_kernel, tq=tq, tk=tk, lam_init=lam_init, hp=hp, dv=dv,
                             pairs_per_trip=pairs_per_trip)
    return pl.pallas_call(
        kern,
        grid=(b, heads // hp, t // tq),
        out_shape=jax.ShapeDtypeStruct((b, t, heads * dv), BF16),
        in_specs=[
            qspec,
            pl.BlockSpec((1, t, hp * dv), lambda i, h, j: (i, 0, h)),
            pl.BlockSpec((1, hp, t // tk, dv, tk), lambda i, h, j: (i, h, 0, 0, 0)),
            pl.BlockSpec((hp, 2, tk, tq), lambda i, h, j: (h, 0, 0, 0)),
            qspec,
            pl.BlockSpec((1, dv), lambda i, h, j: (0, 0)),
            vec, vec, vec, vec,
        ],
        out_specs=qspec,
        scratch_shapes=[pltpu.VMEM((hp, dv + ONES_ROWS, 2 * tq), F32),
                        pltpu.VMEM((hp, tk, 2 * tq), F32), pltpu.VMEM((hp, tk, 2 * tq), F32),
                        pltpu.VMEM((hp, tk, 2 * tq), BF16), pltpu.VMEM((hp, tk, 2 * tq), BF16),
                        pltpu.VMEM((hp, 1, 2 * tq), F32), pltpu.VMEM((hp, 1, 2 * tq), F32),
                        pltpu.VMEM((hp, 1, 2 * tq), F32), pltpu.VMEM((hp, 1, 2 * tq), F32)],
        compiler_params=_cparams(("arbitrary", "arbitrary", "arbitrary")),
        name="da_prompt",
    )(a["dq"], a["dkb"], a["dvt"], bias, a["dg"], da_norm_l.reshape(1, dv), *lam_rows)


def _da_sample_kernel(pt_ref, q_ref, kn_ref, vn_ref, gate_ref, bias_ref, bias0_ref, nw_ref,
                      lq1_ref, lk1_ref, lq2_ref, lk2_ref, *refs, heads, dv, n_pages, lam_init):
    del pt_ref
    k_refs = refs[:n_pages]
    v_refs = refs[n_pages:2 * n_pages]
    o_ref, m_ref, l_ref, acc_ref = refs[2 * n_pages:]
    j = pl.program_id(1)
    half = dv // 2
    rows = 2 * heads

    @pl.when(j == 0)
    def _():
        m_ref[...] = jnp.full_like(m_ref, 0.1 * NEG_BIG)
        l_ref[...] = jnp.zeros_like(l_ref)
        acc_ref[...] = jnp.zeros_like(acc_ref)

    def per_map(x):
        return jnp.concatenate([x[:, h * dv:(h + 1) * dv] for h in range(heads) for _ in range(2)],
                               axis=0)

    r_i = lax.broadcasted_iota(jnp.int32, (rows, dv), 0)
    c_i = lax.broadcasted_iota(jnp.int32, (rows, dv), 1)
    q8 = jnp.where(c_i // half == r_i % 2, per_map(q_ref[0].astype(F32)), 0.0).astype(BF16)

    s = jnp.concatenate([_dot_nt(q8, k_refs[g][0, 0].astype(BF16)) for g in range(n_pages)],
                        axis=1)
    s = s + bias_ref[...]
    m_old = m_ref[...]
    m_new = jnp.maximum(m_old, jnp.max(s, axis=-1, keepdims=True))
    alpha = jnp.exp2(m_old - m_new)
    p = jnp.exp2(s - m_new)
    l_ref[...] = alpha * l_ref[...] + jnp.sum(p, axis=-1, keepdims=True)
    m_ref[...] = m_new
    n_cols = k_refs[0].shape[2]
    pv = jnp.zeros((rows, dv), F32)
    for g in range(n_pages):
        pv = pv + _dot(p[:, g * n_cols:(g + 1) * n_cols].astype(BF16),
                       v_refs[g][0, 0].astype(BF16))
    acc_ref[...] = alpha * acc_ref[...] + pv

    @pl.when(j == pl.num_programs(1) - 1)
    def _():
        kn = per_map(kn_ref[0].astype(BF16).astype(F32))
        s_n = jnp.sum(q8.astype(F32) * kn, axis=-1, keepdims=True) + bias0_ref[...]
        m_o = m_ref[...]
        m_f = jnp.maximum(m_o, s_n)
        al = jnp.exp2(m_o - m_f)
        p_n = jnp.exp2(s_n - m_f)
        l_f = al * l_ref[...] + p_n
        vn = per_map(vn_ref[0].astype(BF16).astype(F32))
        z = (al * acc_ref[...] + p_n.astype(BF16).astype(F32) * vn) / l_f
        lam = _lam(lq1_ref, lk1_ref, lq2_ref, lk2_ref, lam_init)
        gate = gate_ref[0].astype(F32)
        for h in range(heads):
            sl = slice(h * dv, (h + 1) * dv)
            o_h = z[2 * h:2 * h + 1] - lam * z[2 * h + 1:2 * h + 2]
            o_h = _rms(o_h, nw_ref[...]) * (1.0 - lam_init)
            o_ref[0, :, sl] = (o_h * gate[:, sl]).astype(o_ref.dtype)


def _da_sample(a, cache_k, cache_v, page_table, layer, bias_rows, bias0, da_norm_l, lam_rows,
               lam_init, *, heads, dv):
    db = a["dq"].shape[0]
    width = heads * dv
    n_pages_seq = page_table.shape[1]
    page = cache_k.shape[2]
    g_n = PAGES_PER_STEP
    steps = n_pages_seq // g_n
    rows = 2 * heads
    row = pl.BlockSpec((1, 1, width), lambda i, j, pt: (i, 0, 0))
    vec = pl.BlockSpec((1, dv // 2), lambda i, j, pt: (0, 0))

    def page_spec(g):
        return pl.BlockSpec((1, 1, page * heads, dv),
                            lambda i, j, pt, g=g: (layer, pt[i, j * g_n + g], 0, 0))

    kern = functools.partial(_da_sample_kernel, heads=heads, dv=dv, n_pages=g_n, lam_init=lam_init)
    grid_spec = pltpu.PrefetchScalarGridSpec(
        num_scalar_prefetch=1,
        grid=(db, steps),
        in_specs=[row, row, row, row,
                  pl.BlockSpec((rows, g_n * page * heads), lambda i, j, pt: (0, j)),
                  pl.BlockSpec((rows, 1), lambda i, j, pt: (0, 0)),
                  pl.BlockSpec((1, dv), lambda i, j, pt: (0, 0)),
                  vec, vec, vec, vec]
                 + [page_spec(g) for g in range(g_n)] * 2,
        out_specs=row,
        scratch_shapes=[pltpu.VMEM((rows, 1), F32), pltpu.VMEM((rows, 1), F32),
                        pltpu.VMEM((rows, dv), F32)],
    )
    ck = cache_k.reshape(cache_k.shape[0], cache_k.shape[1], page * heads, dv)
    cv = cache_v.reshape(cache_v.shape[0], cache_v.shape[1], page * heads, dv)
    return pl.pallas_call(
        kern,
        grid_spec=grid_spec,
        out_shape=jax.ShapeDtypeStruct((db, 1, width), BF16),
        compiler_params=_cparams(("arbitrary", "arbitrary")),
        name="da_sample",
    )(page_table, a["dq"], a["dkf"], a["dvf"], a["dg"], bias_rows, bias0,
      da_norm_l.reshape(1, dv), *lam_rows, *([ck] * g_n), *([cv] * g_n))


def _out_kernel(x_ref, oh_ref, od_ref, *refs, heads, dh, hgw, daw, fused_mem):
    if fused_mem:
        mq_ref, mg_ref, mk_ref, mv_ref, w_ref, nw_ref, y_ref = refs
        mq = mq_ref[0]
        mg = mg_ref[0].astype(F32)
        parts = []
        for h in range(heads):
            sl = slice(h * dh, (h + 1) * dh)
            s = _dot_nt(mq[:, sl], mk_ref[0, :, sl].astype(BF16)) * (dh ** -0.5)
            p = jnp.exp(s - jnp.max(s, axis=-1, keepdims=True))
            o = _dot(p.astype(BF16), mv_ref[0, :, sl].astype(BF16))
            o = o / jnp.sum(p, axis=-1, keepdims=True)
            parts.append((o * mg[:, sl]).astype(BF16))
        om = jnp.concatenate(parts, axis=-1)
    else:
        om_ref, w_ref, nw_ref, y_ref = refs
        om = om_ref[0]
    y = _dot(oh_ref[0], w_ref[0, :hgw, :])
    y = y + _dot(od_ref[0], w_ref[0, hgw:hgw + daw, :])
    y = y + _dot(om, w_ref[0, hgw + daw:, :])
    y_ref[0] = x_ref[0] + _rms(y, nw_ref[0])


def _out_proj(x, oh, od, mem_args, layer, w_out_bf, norm_post, *, heads, dh, tm, fused_mem):
    b, t, d_model = x.shape
    depth, d_mix, _ = w_out_bf.shape
    hgw = oh.shape[-1]
    daw = od.shape[-1]
    mxw = heads * dh

    def act(width):
        return pl.BlockSpec((1, tm, width), lambda i, j: (i, j, 0))

    if fused_mem:
        mem = pl.BlockSpec((None, 1, mem_args[2].shape[2], mxw), lambda i, j: (layer, i, 0, 0))
        mem_specs = [act(mxw), act(mxw), mem, mem]
    else:
        mem_specs = [act(mxw)]
    return pl.pallas_call(
        functools.partial(_out_kernel, heads=heads, dh=dh, hgw=hgw, daw=daw, fused_mem=fused_mem),
        grid=(b, t // tm),
        out_shape=jax.ShapeDtypeStruct((b, t, d_model), F32),
        in_specs=[act(d_model), act(hgw), act(daw)] + mem_specs
                 + [pl.BlockSpec((1, d_mix, d_model), lambda i, j: (layer, 0, 0),
                                 pipeline_mode=pl.Buffered(1)),
                    pl.BlockSpec((1, 1, d_model), lambda i, j: (layer, 0, 0))],
        out_specs=act(d_model),
        compiler_params=_cparams(("arbitrary", "arbitrary")),
        name="out_proj" if fused_mem else "out_proj_rows",
    )(x, oh, od, *mem_args, w_out_bf, norm_post.reshape(depth, 1, d_model))


def _mem_sample_kernel(mq_ref, mg_ref, mk_ref, mv_ref, om_ref, *, heads, dh):
    pad = 8

    def one_sequence(n, carry):
        q = mq_ref[n]
        q_rows = jnp.concatenate([q[:, h * dh:(h + 1) * dh] for h in range(heads)]
                                 + [jnp.zeros((pad - heads, dh), BF16)], axis=0)
        s = _dot_nt(q_rows, mk_ref[n].astype(BF16)) * (dh ** -0.5)
        r_i = lax.broadcasted_iota(jnp.int32, s.shape, 0)
        c_i = lax.broadcasted_iota(jnp.int32, s.shape, 1)
        s = jnp.where(c_i % heads == r_i, s, NEG_BIG)
        p = jnp.exp(s - jnp.max(s, axis=-1, keepdims=True))
        o = _dot(p.astype(BF16), mv_ref[n].astype(BF16)) / jnp.sum(p, axis=-1, keepdims=True)
        mg = mg_ref[n].astype(F32)
        for h in range(heads):
            sl = slice(h * dh, (h + 1) * dh)
            om_ref[n, :, sl] = (o[h:h + 1] * mg[:, sl]).astype(om_ref.dtype)
        return carry

    lax.fori_loop(0, mq_ref.shape[0], one_sequence, 0)


def _mem_sample(a, cmk, cmv, layer, *, heads, dh):
    db = a["mq"].shape[0]
    width = heads * dh
    rows = cmk.shape[2]
    bs = SAMPLE_ROWS_PER_STEP
    row = pl.BlockSpec((bs, 1, width), lambda i: (i, 0, 0))
    mem = pl.BlockSpec((None, bs, rows, dh), lambda i: (layer, i, 0, 0))
    return pl.pallas_call(
        functools.partial(_mem_sample_kernel, heads=heads, dh=dh),
        grid=(db // bs,),
        out_shape=jax.ShapeDtypeStruct((db, 1, width), BF16),
        in_specs=[row, row, mem, mem],
        out_specs=row,
        compiler_params=_cparams(("arbitrary",)),
        name="mem_sample",
    )(a["mq"], a["mg"], cmk, cmv)


def kernel(x_prompt, x_sample, mem_prompt, cache_da_k, cache_da_v, cache_mem_k, cache_mem_v,
           state_hgrn, page_table, w_in, w_out, w_mem_kv, norm_pre, norm_post, mem_norm, hg_norm,
           da_norm, hg_lb, da_lq1, da_lk1, da_lq2, da_lk2, rel_bias):
    depth = w_in.shape[0]
    b, t, d_model = x_prompt.shape
    db = x_sample.shape[0]
    hg_heads, hg_dk = state_hgrn.shape[2], state_hgrn.shape[3]
    hgw = hg_heads * hg_dk
    da_heads, dv = cache_da_v.shape[3], cache_da_v.shape[4]
    daw = da_heads * dv
    mx_heads, mx_dh = cache_mem_k.shape[3], cache_mem_k.shape[4]
    mxw = mx_heads * mx_dh
    n_mem = mem_prompt.shape[1]
    page = cache_da_k.shape[2]
    past = page_table.shape[1] * page
    assert w_in.shape[2] == 4 * hgw + 4 * daw + 2 * mxw
    assert x_sample.shape[1] == 1 and MAX_DIST <= page <= DA_TK
    assert t % 1024 == 0 and page_table.shape[1] % PAGES_PER_STEP == 0
    assert db % SAMPLE_ROWS_PER_STEP == 0 and hg_heads % HG_HEADS_PER_STEP == 0

    w_in_bf = w_in.astype(BF16)
    w_out_bf = w_out.astype(BF16)
    w_mem_bf = w_mem_kv.astype(BF16)

    bias = _bias_tiles(rel_bias, da_heads)
    near = bias[:, 0, DA_TK - page:, 0]
    bias_rows = jnp.concatenate([jnp.zeros((da_heads, past - page), F32), near], axis=1)
    bias_rows = jnp.repeat(bias_rows, 2, axis=0)
    col_head = jnp.arange(past * da_heads, dtype=jnp.int32) % da_heads
    row_head = jnp.arange(2 * da_heads, dtype=jnp.int32) // 2
    bias_rows = jnp.where(col_head[None, :] == row_head[:, None],
                          jnp.repeat(bias_rows, da_heads, axis=1), NEG_BIG)
    bias0 = jnp.repeat(bias[:, 1, 0, 0], 2).reshape(2 * da_heads, 1)

    mk_p, mv_p = _mem_kv(mem_prompt, mem_norm, w_mem_bf)
    cmk = cache_mem_k.reshape(depth, db, n_mem * mx_heads, mx_dh)
    cmv = cache_mem_v.reshape(depth, db, n_mem * mx_heads, mx_dh)

    xp, xs = x_prompt, x_sample.reshape(1, db, d_model)
    outs = {n: [] for n in ("sp", "ks", "vs")}
    kv_all = ()
    s_s = ()
    dims = dict(hgw=hgw, daw=daw, mxw=mxw, da_heads=da_heads)
    for l in range(depth):
        lam_init = 0.8 - 0.6 * math.exp(-0.3 * l)
        lam_rows = [v[l].reshape(1, dv // 2) for v in (da_lq1, da_lk1, da_lq2, da_lk2)]

        a = _in_proj(xp, l, norm_pre, w_in_bf, hg_lb, kv_all, tm=512, emit_vt=True, **dims)
        kv_all = (a["dkf"], a["dvf"])
        oh, s_p = _hgrn_prompt(a, hg_norm[l], heads=hg_heads, dk=hg_dk, tt=512,
                               group=HG_GROUP, unroll=HG_UNROLL, hp=HG_HEADS_PER_STEP)
        od = _da_prompt(a, bias, da_norm[l], lam_rows, lam_init, heads=da_heads, dv=dv,
                        pairs_per_trip=DA_PAIRS_PER_TRIP)
        xp = _out_proj(xp, oh, od, (a["mq"], a["mg"], mk_p, mv_p), l, w_out_bf, norm_post,
                       heads=mx_heads, dh=mx_dh, tm=1024, fused_mem=True)
        outs["sp"].append(s_p)

        a = _in_proj(xs, l, norm_pre, w_in_bf, hg_lb, (), tm=db, emit_vt=False, **dims)
        a = {n: v.reshape(db, 1, v.shape[-1]) for n, v in a.items()}
        oh, s_s = _hgrn_step(a, hg_norm[l], state_hgrn, s_s, l, heads=hg_heads, dk=hg_dk)
        s_s = (s_s,)
        od = _da_sample(a, cache_da_k, cache_da_v, page_table, l, bias_rows, bias0, da_norm[l],
                        lam_rows, lam_init, heads=da_heads, dv=dv)
        om = _mem_sample(a, cmk, cmv, l, heads=mx_heads, dh=mx_dh)
        as_rows = lambda v: v.reshape(1, db, v.shape[-1])
        xs = _out_proj(xs, as_rows(oh), as_rows(od), (as_rows(om),), l, w_out_bf, norm_post,
                       heads=mx_heads, dh=mx_dh, tm=db, fused_mem=False)
        outs["ks"].append(a["dkf"].reshape(db, 1, da_heads, dv))
        outs["vs"].append(a["dvf"].reshape(db, 1, da_heads, dv))

    return (xp, xs.reshape(db, 1, d_model), kv_all[0], kv_all[1],
            jnp.stack(outs["sp"]), mk_p.reshape(depth, b, n_mem, mx_heads, mx_dh),
            mv_p.reshape(depth, b, n_mem, mx_heads, mx_dh), jnp.stack(outs["ks"]),
            jnp.stack(outs["vs"]), s_s[0])
```

```python
import functools
import math

import jax
import jax.numpy as jnp
from jax import lax
from jax.experimental import pallas as pl
from jax.experimental.pallas import tpu as pltpu

F32 = jnp.float32
BF16 = jnp.bfloat16

EPS = 1e-6
MAX_DIST = 128
HG_CHUNK = 64
HG_SUB = 8
HG_UNROLL = 8
HG_GROUP = 2
HG_HEADS_PER_STEP = 8
DA_HEADS_PER_STEP = 2
DA_PAIRS_PER_TRIP = 4
IN_ROW_SPLIT = 2
LOG2E = math.log2(math.e)
ONES_ROWS = 16
DA_TQ = 256
DA_TK = 256
PAGES_PER_STEP = 16
SAMPLE_ROWS_PER_STEP = 4
NEG_BIG = -1e30
VMEM_LIMIT_BYTES = 56 * 1024 * 1024


def _cparams(sem):
    return pltpu.CompilerParams(dimension_semantics=sem, vmem_limit_bytes=VMEM_LIMIT_BYTES)


def _sigmoid(x):
    return 1.0 / (1.0 + jnp.exp(-x))


def _silu(x):
    return x * _sigmoid(x)


def _rms(x, w):
    return x * lax.rsqrt(jnp.mean(x * x, axis=-1, keepdims=True) + EPS) * w


def _dot(a, b):
    return jnp.dot(a, b, preferred_element_type=F32)


def _dot_nt(a, b):
    return lax.dot_general(a, b, (((1,), (1,)), ((), ())), preferred_element_type=F32)


def _dot_tn(a, b):
    return lax.dot_general(a, b, (((0,), (0,)), ((), ())), preferred_element_type=F32)


def _lam(lq1_ref, lk1_ref, lq2_ref, lk2_ref, lam_init):
    a = jnp.sum(lq1_ref[...] * lk1_ref[...], axis=-1, keepdims=True)
    b = jnp.sum(lq2_ref[...] * lk2_ref[...], axis=-1, keepdims=True)
    return jnp.exp(a) - jnp.exp(b) + lam_init


def _bias_kernel(tab_ref, out_ref, *, n_buckets, heads, tk, tq):
    kk = lax.broadcasted_iota(jnp.int32, (tk, tq), 0)
    qq = lax.broadcasted_iota(jnp.int32, (tk, tq), 1)
    max_exact = n_buckets // 2
    for jj in range(2):
        d = qq - kk + (tk if jj == 0 else 0)
        n = jnp.maximum(d, 0)
        nf = jnp.maximum(n, 1).astype(F32)
        large = max_exact + (jnp.log(nf / max_exact) / math.log(MAX_DIST / max_exact)
                             * (n_buckets - max_exact)).astype(jnp.int32)
        large = jnp.minimum(large, n_buckets - 1)
        bucket = jnp.where(n < max_exact, n, large)
        for h in range(heads):
            val = jnp.zeros((tk, tq), F32)
            for b in range(n_buckets):
                val = jnp.where(bucket == b, tab_ref[b, h], val)
            val = (val - tab_ref[n_buckets - 1, h]) * LOG2E
            out_ref[h, jj] = jnp.where(d < 0, NEG_BIG, val)


def _bias_tiles(rel_bias, heads):
    n_buckets = rel_bias.shape[0]
    return pl.pallas_call(
        functools.partial(_bias_kernel, n_buckets=n_buckets, heads=heads, tk=DA_TK, tq=DA_TQ),
        out_shape=jax.ShapeDtypeStruct((heads, 2, DA_TK, DA_TQ), F32),
        in_specs=[pl.BlockSpec(memory_space=pltpu.SMEM)],
        out_specs=pl.BlockSpec(memory_space=pltpu.VMEM),
        name="t5_bias",
    )(rel_bias)


def _memkv_kernel(mem_ref, nw_ref, w_ref, k_ref, v_ref, *, width):
    xn = _rms(mem_ref[0], nw_ref[0]).astype(BF16)
    y = _dot(xn, w_ref[0])
    k_ref[0, 0] = y[:, :width]
    v_ref[0, 0] = y[:, width:]


def _mem_kv(mem_prompt, mem_norm, w_mem_kv_bf):
    depth, d_model, two_w = w_mem_kv_bf.shape
    width = two_w // 2
    b, n_mem, _ = mem_prompt.shape
    out = jax.ShapeDtypeStruct((depth, b, n_mem, width), F32)
    return pl.pallas_call(
        functools.partial(_memkv_kernel, width=width),
        grid=(depth, b),
        out_shape=(out, out),
        in_specs=[
            pl.BlockSpec((1, n_mem, d_model), lambda l, i: (i, 0, 0)),
            pl.BlockSpec((1, 1, d_model), lambda l, i: (l, 0, 0)),
            pl.BlockSpec((1, d_model, two_w), lambda l, i: (l, 0, 0)),
        ],
        out_specs=(pl.BlockSpec((1, 1, n_mem, width), lambda l, i: (l, i, 0, 0)),
                   pl.BlockSpec((1, 1, n_mem, width), lambda l, i: (l, i, 0, 0))),
        compiler_params=_cparams(("arbitrary", "arbitrary")),
        name="mem_kv",
    )(mem_prompt, mem_norm.reshape(depth, 1, d_model), w_mem_kv_bf)


def _inproj_kernel(x_ref, nw_ref, w_ref, lbp_ref, *refs, names, n_carried, layer, hgw, daw, mxw,
                   da_heads, tkb, emit_vt, col_chunk, row_split):
    r = dict(zip(names, refs[n_carried:]))

    p = lbp_ref[...]
    e = jnp.exp(p - jnp.max(p, axis=0, keepdims=True))
    sm = e / jnp.sum(e, axis=0, keepdims=True)
    lb = jnp.zeros((1, hgw), F32)
    for row in range(1, layer + 1):
        lb = lb + sm[row:row + 1, :]

    dv = daw // da_heads
    dqk = dv // 2

    def do_rows(r0, r1):
        rows = slice(r0, r1)
        xn = _rms(x_ref[0, rows], nw_ref[0]).astype(BF16)

        def seg(c0, width, fn):
            for c in range(0, width, col_chunk):
                w = min(col_chunk, width - c)
                fn(c, w, _dot(xn, w_ref[0, :, c0 + c:c0 + c + w]))

        def st(ref, fn):
            def go(c, w, y):
                ref[0, rows, c:c + w] = fn(y).astype(ref.dtype)
            return go

        def forget(c, w, y):
            f = lb[:, c:c + w] + (1.0 - lb[:, c:c + w]) * _sigmoid(y)
            r["lf"][0, rows, c:c + w] = jnp.log2(f)
            r["kh"][0, rows, c:c + w] = (1.0 - f).astype(BF16)

        def st_f32(ref, c, w, y):
            if emit_vt:
                for hh in range(c // dv, (c + w) // dv):
                    ref[0, 0, rows, hh, :] = y[:, hh * dv - c:(hh + 1) * dv - c]
            else:
                ref[0, rows, c:c + w] = y

        def da_k(c, w, y):
            st_f32(r["dkf"], c, w, y)
            r["dkb"][0, rows, c:c + w] = y.astype(BF16)

        def da_v(c, w, y):
            st_f32(r["dvf"], c, w, y)
            if emit_vt:
                piece = min(tkb, r1 - r0)
                for hh in range(c // dv, (c + w) // dv):
                    for q0 in range(r0, r1, piece):
                        blk = y[q0 - r0:q0 - r0 + piece, hh * dv - c:(hh + 1) * dv - c]
                        r["dvt"][0, hh, q0 // tkb, :, q0 % tkb:q0 % tkb + piece] = (
                            blk.T.astype(BF16))

        o = 0
        seg(o, hgw, st(r["qh"], _silu)); o += hgw
        seg(o, hgw, forget); o += hgw
        seg(o, hgw, st(r["vh"], lambda y: y)); o += hgw
        seg(o, hgw, st(r["gh"], _silu)); o += hgw
        seg(o, daw, st(r["dq"], lambda y: y * (dqk ** -0.5 * LOG2E))); o += daw
        seg(o, daw, da_k); o += daw
        seg(o, daw, da_v); o += daw
        seg(o, daw, st(r["dg"], _silu)); o += daw
        seg(o, mxw, st(r["mq"], lambda y: y)); o += mxw
        seg(o, mxw, st(r["mg"], _silu)); o += mxw

    tm = x_ref.shape[1]
    for part in range(row_split):
        do_rows(part * (tm // row_split), (part + 1) * (tm // row_split))


def _in_proj(x, layer, norm_pre, w_bf, hg_lb, carried, *, hgw, daw, mxw, da_heads, tm, emit_vt):
    b, t, d_model = x.shape
    depth = w_bf.shape[0]
    tkb = DA_TK
    dv = daw // da_heads

    def act(width, dtype):
        return (jax.ShapeDtypeStruct((b, t, width), dtype),
                pl.BlockSpec((1, tm, width), lambda i, j: (i, j, 0)))

    def rows_f32():
        if not emit_vt:
            return act(daw, F32)
        return (jax.ShapeDtypeStruct((depth, b, t, da_heads, dv), F32),
                pl.BlockSpec((1, 1, tm, da_heads, dv), lambda i, j: (layer, i, j, 0, 0)))

    outs = dict(qh=act(hgw, BF16), kh=act(hgw, BF16), vh=act(hgw, BF16), lf=act(hgw, F32),
                gh=act(hgw, BF16), dq=act(daw, BF16), dkf=rows_f32(), dkb=act(daw, BF16),
                dvf=rows_f32())
    if emit_vt:
        outs["dvt"] = (jax.ShapeDtypeStruct((b, da_heads, t // tkb, dv, tkb), BF16),
                       pl.BlockSpec((1, da_heads, tm // tkb, dv, tkb),
                                    lambda i, j: (i, 0, j, 0, 0)))
    outs.update(dg=act(daw, BF16), mq=act(mxw, BF16), mg=act(mxw, BF16))
    names = tuple(outs)
    in_specs = [
        pl.BlockSpec((1, tm, d_model), lambda i, j: (i, j, 0)),
        pl.BlockSpec((1, 1, d_model), lambda i, j: (layer, 0, 0)),
        pl.BlockSpec((1,) + w_bf.shape[1:], lambda i, j: (layer, 0, 0),
                     pipeline_mode=pl.Buffered(1)),
        pl.BlockSpec((depth, hgw), lambda i, j: (0, 0)),
    ] + [pl.BlockSpec(memory_space=pl.ANY)] * len(carried)
    aliases = {len(in_specs) - len(carried) + n: names.index(name)
               for n, name in enumerate(("dkf", "dvf")[:len(carried)])}
    kern = functools.partial(_inproj_kernel, names=names, n_carried=len(carried), layer=layer,
                             hgw=hgw, daw=daw, mxw=mxw, da_heads=da_heads, tkb=tkb,
                             emit_vt=emit_vt, col_chunk=512,
                             row_split=IN_ROW_SPLIT if emit_vt else 1)
    res = pl.pallas_call(
        kern,
        grid=(b, t // tm),
        out_shape=tuple(outs[n][0] for n in names),
        in_specs=in_specs,
        out_specs=tuple(outs[n][1] for n in names),
        input_output_aliases=aliases,
        compiler_params=_cparams(("arbitrary", "arbitrary")),
        name="in_proj_vt" if emit_vt else "in_proj",
    )(x, norm_pre.reshape(depth, 1, d_model), w_bf, hg_lb, *carried)
    return dict(zip(names, res))


def _hgrn_off_shape(chunk, sub):
    nsub = chunk // sub
    n_off_cols = sub * (nsub * (nsub - 1) // 2)
    return chunk - sub, n_off_cols, -(-n_off_cols // 128) * 128


def _hgrn_kernel(q_ref, k_ref, v_ref, g_ref, gate_ref, nw_ref, o_ref, s_ref, st_ref, tri_ref,
                 offm_ref, diagm_ref, cum_ref, qe_ref, dec_ref, kv_ref, oi_ref, *, chunk, sub,
                 n_chunks, unroll, group, hp, dk):
    j = pl.program_id(2)
    nsub = chunk // sub
    n_off_rows, n_off_cols, n_off_pad = _hgrn_off_shape(chunk, sub)

    @pl.when(j == 0)
    def _():
        st_ref[...] = jnp.zeros_like(st_ref)
        r_i = lax.broadcasted_iota(jnp.int32, (chunk, chunk), 0)
        c_i = lax.broadcasted_iota(jnp.int32, (chunk, chunk), 1)
        tri_ref[...] = jnp.where(c_i <= r_i, 1.0, 0.0).astype(BF16)

        ro = lax.broadcasted_iota(jnp.int32, (n_off_rows, n_off_pad), 0) // sub + 1
        co = lax.broadcasted_iota(jnp.int32, (n_off_rows, n_off_pad), 1)
        cblk = jnp.zeros_like(co)
        start = 0
        for i in range(1, nsub):
            cblk = jnp.where((co >= start) & (co < start + i * sub), i, cblk)
            start += i * sub
        offm_ref[...] = jnp.where(ro == cblk, 1.0, 0.0)

        rd = lax.broadcasted_iota(jnp.int32, (nsub * sub * sub, chunk), 0)
        cd = lax.broadcasted_iota(jnp.int32, (nsub * sub * sub, chunk), 1)
        r_is = rd // sub
        diagm_ref[...] = jnp.where((r_is == cd) & (rd % sub >= r_is % sub), 1.0, 0.0)

    g = jnp.concatenate([g_ref[0, c * chunk:(c + 1) * chunk, :] for c in range(n_chunks)], axis=1)
    g1 = g.astype(BF16)
    rem = g - g1.astype(F32)
    g2 = rem.astype(BF16)
    g3 = (rem - g2.astype(F32)).astype(BF16)
    tri = tri_ref[...]
    cum_all = _dot(tri, g1) + _dot(tri, g2) + _dot(tri, g3)
    for c in range(n_chunks * hp):
        cum_ref[c] = cum_all[:, c * dk:(c + 1) * dk]

    def stage_a(ci, hh):
        r0 = pl.multiple_of(ci * chunk, chunk)
        cols = slice(hh * dk, (hh + 1) * dk)
        item = ci * hp + hh
        q = q_ref[0, pl.ds(r0, chunk), cols].astype(F32)
        k_bf = k_ref[0, pl.ds(r0, chunk), cols]
        k = k_bf.astype(F32)
        v = v_ref[0, pl.ds(r0, chunk), cols]
        cum = cum_ref[item]

        last = cum[chunk - 1:chunk, :]
        qe_ref[item] = (q * jnp.exp2(cum)).astype(BF16)
        dec_ref[item] = jnp.exp2(last)
        kv = _dot_tn(v, (k * jnp.exp2(last - cum)).astype(BF16))

        q_parts, k_parts, v_parts = [], [], []
        v32 = v.astype(F32)
        for i in range(1, nsub):
            b_i = cum[i * sub - 1:i * sub, :]
            lo, hi = i * sub, (i + 1) * sub
            q_parts.append(q[lo:hi] * jnp.exp2(cum[lo:hi] - b_i))
            k_parts.append(k[:lo] * jnp.exp2(b_i - cum[:lo]))
            v_parts.append(v32[:lo])
        if n_off_pad > n_off_cols:
            k_parts.append(jnp.zeros((n_off_pad - n_off_cols, dk), F32))
            v_parts.append(jnp.zeros((n_off_pad - n_off_cols, v.shape[-1]), F32))
        a_off = _dot_nt(jnp.concatenate(q_parts, axis=0).astype(BF16),
                        jnp.concatenate(k_parts, axis=0).astype(BF16))
        v_off = jnp.concatenate(v_parts, axis=0).astype(BF16)

        n_parts = []
        for i in range(nsub):
            lo, hi = i * sub, (i + 1) * sub
            c_blk = cum[lo:hi]
            q_blk = q[lo:hi]
            for s in range(sub):
                dec = jnp.exp2(jnp.minimum(c_blk - c_blk[s:s + 1, :], 0.0))
                n_parts.append(q_blk * dec)
        r = _dot_nt(jnp.concatenate(n_parts, axis=0).astype(BF16), k_bf)
        return item, kv, a_off, v_off, r, v

    def stage_b(item, kv, a_off, v_off, r, v):
        kv_ref[item] = kv
        a_off = (a_off * offm_ref[...]).astype(BF16)
        o_off = _dot(a_off, v_off)
        r = r * diagm_ref[...]
        a_parts = []
        for i in range(nsub):
            base = i * sub * sub
            acc = r[base:base + sub]
            for s in range(1, sub):
                acc = acc + r[base + s * sub:base + (s + 1) * sub]
            a_parts.append(acc)
        a_diag = jnp.concatenate(a_parts, axis=0).astype(BF16)
        return item, o_off, _dot(a_diag, v)

    def stage_c(item, o_off, o_diag):
        oi_ref[item] = o_diag + jnp.concatenate([jnp.zeros((sub, dk), F32), o_off], axis=0)

    def intra(gi, carry):
        items = [(gi * group + u // hp, u % hp) for u in range(group * hp)]
        mid = [stage_a(ci, hh) for ci, hh in items]
        for x in [stage_b(*x) for x in mid]:
            stage_c(*x)
        return carry

    lax.fori_loop(0, n_chunks // group, intra, 0)

    def inter(ci, carry):
        r0 = pl.multiple_of(ci * chunk, chunk)
        for hh in range(hp):
            cols = slice(hh * dk, (hh + 1) * dk)
            item = ci * hp + hh
            s_prev = st_ref[hh]
            o = oi_ref[item] + _dot_nt(qe_ref[item], s_prev.astype(BF16))
            st_ref[hh] = s_prev * dec_ref[item] + kv_ref[item]
            gate = gate_ref[0, pl.ds(r0, chunk), cols].astype(F32)
            o_ref[0, pl.ds(r0, chunk), cols] = (_rms(o, nw_ref[...]) * gate).astype(o_ref.dtype)
        return carry

    lax.fori_loop(0, n_chunks, inter, 0, unroll=unroll)

    @pl.when(j == pl.num_programs(2) - 1)
    def _():
        for hh in range(hp):
            s_ref[0, hh] = st_ref[hh].T


def _hgrn_prompt(a, hg_norm_l, *, heads, dk, tt, group, unroll, hp):
    b, t, _ = a["qh"].shape
    spec = pl.BlockSpec((1, tt, hp * dk), lambda i, h, j: (i, j, h))
    chunk, sub = HG_CHUNK, HG_SUB
    n_chunks = tt // chunk
    n_off_rows, _, n_off_pad = _hgrn_off_shape(chunk, sub)
    kern = functools.partial(_hgrn_kernel, chunk=chunk, sub=sub, n_chunks=n_chunks,
                             unroll=unroll, group=group, hp=hp, dk=dk)
    n_items = n_chunks * hp
    return pl.pallas_call(
        kern,
        grid=(b, heads // hp, t // tt),
        out_shape=(jax.ShapeDtypeStruct((b, t, heads * dk), BF16),
                   jax.ShapeDtypeStruct((b, heads, dk, dk), F32)),
        in_specs=[spec, spec, spec, spec, spec, pl.BlockSpec((1, dk), lambda i, h, j: (0, 0))],
        out_specs=(spec, pl.BlockSpec((1, hp, dk, dk), lambda i, h, j: (i, h, 0, 0))),
        scratch_shapes=[pltpu.VMEM((hp, dk, dk), F32), pltpu.VMEM((chunk, chunk), BF16),
                        pltpu.VMEM((n_off_rows, n_off_pad), F32),
                        pltpu.VMEM((chunk * sub, chunk), F32),
                        pltpu.VMEM((n_items, chunk, dk), F32),
                        pltpu.VMEM((n_items, chunk, dk), BF16),
                        pltpu.VMEM((n_items, 1, dk), F32),
                        pltpu.VMEM((n_items, dk, dk), F32),
                        pltpu.VMEM((n_items, chunk, dk), F32)],
        compiler_params=_cparams(("arbitrary", "arbitrary", "arbitrary")),
        name="hgrn_prompt",
    )(a["qh"], a["kh"], a["vh"], a["lf"], a["gh"], hg_norm_l.reshape(1, dk))


def _hgrn_step_kernel(q_ref, lf_ref, v_ref, gate_ref, nw_ref, s_ref, *refs, heads, dk):
    o_ref, sn_ref = refs[-2:]

    def col(row):
        return jnp.broadcast_to(row, (dk, dk)).T

    def one_sequence(s, carry):
        for h in range(heads):
            sl = slice(h * dk, (h + 1) * dk)
            f = jnp.exp2(lf_ref[s, :, sl])
            s_new = col(f) * s_ref[s, h] + col(1.0 - f) * v_ref[s, :, sl].astype(F32)
            sn_ref[0, s, h] = s_new
            o = jnp.sum(col(q_ref[s, :, sl].astype(F32)) * s_new, axis=0, keepdims=True)
            o = _rms(o, nw_ref[...]) * gate_ref[s, :, sl].astype(F32)
            o_ref[s, :, sl] = o.astype(o_ref.dtype)
        return carry

    lax.fori_loop(0, q_ref.shape[0], one_sequence, 0)


def _hgrn_step(a, hg_norm_l, state_hgrn, carried, layer, *, heads, dk):
    db = a["qh"].shape[0]
    w = heads * dk
    depth = state_hgrn.shape[0]
    bs = SAMPLE_ROWS_PER_STEP
    row = pl.BlockSpec((bs, 1, w), lambda i: (i, 0, 0))
    state = pl.BlockSpec((1, bs, heads, dk, dk), lambda i: (layer, i, 0, 0, 0))
    in_specs = [row, row, row, row, pl.BlockSpec((1, dk), lambda i: (0, 0)),
                pl.BlockSpec((None, bs, heads, dk, dk), lambda i: (layer, i, 0, 0, 0))]
    in_specs += [pl.BlockSpec(memory_space=pl.ANY)] * len(carried)
    return pl.pallas_call(
        functools.partial(_hgrn_step_kernel, heads=heads, dk=dk),
        grid=(db // bs,),
        out_shape=(jax.ShapeDtypeStruct((db, 1, w), BF16),
                   jax.ShapeDtypeStruct((depth, db, heads, dk, dk), F32)),
        in_specs=in_specs,
        out_specs=(row, state),
        input_output_aliases={len(in_specs) - 1: 1} if carried else {},
        compiler_params=_cparams(("arbitrary",)),
        name="hgrn_step",
    )(a["qh"], a["lf"], a["vh"], a["gh"], hg_norm_l.reshape(1, dk), state_hgrn, *carried)


def _da_prompt_kernel(q_ref, k_ref, vt_ref, bias_ref, gate_ref, nw_ref, lq1_ref, lk1_ref,
                      lq2_ref, lk2_ref, o_ref, acc_ref, s0_ref, s1_ref, p0_ref, p1_ref, al0_ref,
                      al1_ref, mb0_ref, mb1_ref, *, tq, tk, lam_init, hp, dv, pairs_per_trip):
    mb_refs = (mb0_ref, mb1_ref)
    i = pl.program_id(2)
    half = dv // 2
    row = lax.broadcasted_iota(jnp.int32, (dv, tq), 0)

    def qbd_of(hh):
        qt = q_ref[0, :, hh * dv:(hh + 1) * dv].astype(F32).T
        return jnp.concatenate([jnp.where(row < half, qt, 0.0), jnp.where(row >= half, qt, 0.0)],
                               axis=1).astype(BF16)

    qbd = [qbd_of(hh) for hh in range(hp)]
    s_refs, p_refs, al_refs = (s0_ref, s1_ref), (p0_ref, p1_ref), (al0_ref, al1_ref)
    acc_ref[...] = jnp.zeros_like(acc_ref)

    def scores(slot, blk):
        for hh in range(hp):
            k_blk = k_ref[0, pl.ds(pl.multiple_of(blk * tk, tk), tk), hh * dv:(hh + 1) * dv]
            s = _dot(k_blk, qbd[hh])
            s_refs[slot][hh] = s
            mb_refs[slot][hh] = jnp.max(s, axis=0, keepdims=True)

    def probs(slot, ms, bias_idx):
        out = []
        for hh in range(hp):
            s = s_refs[slot][hh]
            if bias_idx is None:
                m_blk = mb_refs[slot][hh]
            else:
                bias = bias_ref[hh, bias_idx]
                s = s + jnp.concatenate([bias, bias], axis=1)
                m_blk = jnp.max(s, axis=0, keepdims=True)
            m_new = jnp.maximum(ms[hh], m_blk)
            al_refs[slot][hh] = jnp.exp2(ms[hh] - m_new)
            p_refs[slot][hh] = jnp.exp2(s - m_new).astype(BF16)
            out.append(m_new)
        return tuple(out)

    def accumulate(slot, blk):
        kb = jnp.maximum(blk, 0)
        for hh in range(hp):
            vt = jnp.concatenate([vt_ref[0, hh, kb], jnp.ones((ONES_ROWS, tk), BF16)], axis=0)
            acc_ref[hh] = al_refs[slot][hh] * acc_ref[hh] + _dot(vt, p_refs[slot][hh])

    m = tuple(jnp.full((1, 2 * tq), 0.1 * NEG_BIG, F32) for _ in range(hp))

    def far_pair(e, ms):
        accumulate(1, e - 1)
        ms = probs(0, ms, None)
        scores(1, e + 1)
        accumulate(0, e)
        ms = probs(1, ms, None)
        scores(0, e + 2)
        return ms

    def far_pairs(u, ms):
        for v in range(pairs_per_trip):
            ms = far_pair(2 * (pairs_per_trip * u + v), ms)
        return ms

    def tail_odd(ms):
        accumulate(1, i - 2)
        ms = probs(0, ms, 0)
        scores(1, i)
        accumulate(0, i - 1)
        ms = probs(1, ms, 1)
        accumulate(1, i)
        return ms

    def tail_even(ms):
        accumulate(1, i - 3)
        ms = probs(0, ms, None)
        scores(1, i - 1)
        accumulate(0, i - 2)
        ms = probs(1, ms, 0)
        scores(0, i)
        accumulate(1, i - 1)
        ms = probs(0, ms, 1)
        accumulate(0, i)
        return ms

    def tail_zero(ms):
        ms = probs(0, ms, 1)
        accumulate(0, i)
        return ms

    p1_ref[...] = jnp.zeros_like(p1_ref)
    al1_ref[...] = jnp.ones_like(al1_ref)
    scores(0, 0)
    n_pairs = jnp.right_shift(jnp.maximum(i - 1, 0), 1)
    n_trips = n_pairs // pairs_per_trip
    m = lax.fori_loop(0, n_trips, far_pairs, m)
    if pairs_per_trip > 1:
        m = lax.fori_loop(n_trips * pairs_per_trip, n_pairs, lambda u, ms: far_pair(2 * u, ms), m)
    lax.cond(i == 0, tail_zero,
             lambda ms: lax.cond((i & 1) == 1, tail_odd, tail_even, ms), m)

    lam = _lam(lq1_ref, lk1_ref, lq2_ref, lk2_ref, lam_init)
    for hh in range(hp):
        acc = acc_ref[hh, :dv, :]
        inv = 1.0 / acc_ref[hh, dv:dv + 1, :]
        out_t = acc[:, :tq] * inv[:, :tq] - lam * (acc[:, tq:] * inv[:, tq:])
        out = _rms(out_t.T, nw_ref[...]) * (1.0 - lam_init)
        cols = slice(hh * dv, (hh + 1) * dv)
        o_ref[0, :, cols] = (out * gate_ref[0, :, cols].astype(F32)).astype(o_ref.dtype)


def _da_prompt(a, bias, da_norm_l, lam_rows, lam_init, *, heads, dv, pairs_per_trip):
    b, t, _ = a["dq"].shape
    tq, tk = DA_TQ, DA_TK
    hp = DA_HEADS_PER_STEP
    qspec = pl.BlockSpec((1, tq, hp * dv), lambda i, h, j: (i, j, h))
    vec = pl.BlockSpec((1, dv // 2), lambda i, h, j: (0, 0))
    kern = functools.partial(_da_prompt_kernel, tq=tq, tk=tk, lam_init=lam_init, hp=hp, dv=dv,
                             pairs_per_trip=pairs_per_trip)
    return pl.pallas_call(
        kern,
        grid=(b, heads // hp, t // tq),
        out_shape=jax.ShapeDtypeStruct((b, t, heads * dv), BF16),
        in_specs=[
            qspec,
            pl.BlockSpec((1, t, hp * dv), lambda i, h, j: (i, 0, h)),
            pl.BlockSpec((1, hp, t // tk, dv, tk), lambda i, h, j: (i, h, 0, 0, 0)),
            pl.BlockSpec((hp, 2, tk, tq), lambda i, h, j: (h, 0, 0, 0)),
            qspec,
            pl.BlockSpec((1, dv), lambda i, h, j: (0, 0)),
            vec, vec, vec, vec,
        ],
        out_specs=qspec,
        scratch_shapes=[pltpu.VMEM((hp, dv + ONES_ROWS, 2 * tq), F32),
                        pltpu.VMEM((hp, tk, 2 * tq), F32), pltpu.VMEM((hp, tk, 2 * tq), F32),
                        pltpu.VMEM((hp, tk, 2 * tq), BF16), pltpu.VMEM((hp, tk, 2 * tq), BF16),
                        pltpu.VMEM((hp, 1, 2 * tq), F32), pltpu.VMEM((hp, 1, 2 * tq), F32),
                        pltpu.VMEM((hp, 1, 2 * tq), F32), pltpu.VMEM((hp, 1, 2 * tq), F32)],
        compiler_params=_cparams(("arbitrary", "arbitrary", "arbitrary")),
        name="da_prompt",
    )(a["dq"], a["dkb"], a["dvt"], bias, a["dg"], da_norm_l.reshape(1, dv), *lam_rows)


def _da_sample_kernel(pt_ref, q_ref, kn_ref, vn_ref, gate_ref, bias_ref, bias0_ref, nw_ref,
                      lq1_ref, lk1_ref, lq2_ref, lk2_ref, *refs, heads, dv, n_pages, lam_init):
    del pt_ref
    k_refs = refs[:n_pages]
    v_refs = refs[n_pages:2 * n_pages]
    o_ref, m_ref, l_ref, acc_ref = refs[2 * n_pages:]
    j = pl.program_id(1)
    half = dv // 2
    rows = 2 * heads

    @pl.when(j == 0)
    def _():
        m_ref[...] = jnp.full_like(m_ref, 0.1 * NEG_BIG)
        l_ref[...] = jnp.zeros_like(l_ref)
        acc_ref[...] = jnp.zeros_like(acc_ref)

    def per_map(x):
        return jnp.concatenate([x[:, h * dv:(h + 1) * dv] for h in range(heads) for _ in range(2)],
                               axis=0)

    r_i = lax.broadcasted_iota(jnp.int32, (rows, dv), 0)
    c_i = lax.broadcasted_iota(jnp.int32, (rows, dv), 1)
    q8 = jnp.where(c_i // half == r_i % 2, per_map(q_ref[0].astype(F32)), 0.0).astype(BF16)

    s = jnp.concatenate([_dot_nt(q8, k_refs[g][0, 0].astype(BF16)) for g in range(n_pages)],
                        axis=1)
    s = s + bias_ref[...]
    m_old = m_ref[...]
    m_new = jnp.maximum(m_old, jnp.max(s, axis=-1, keepdims=True))
    alpha = jnp.exp2(m_old - m_new)
    p = jnp.exp2(s - m_new)
    l_ref[...] = alpha * l_ref[...] + jnp.sum(p, axis=-1, keepdims=True)
    m_ref[...] = m_new
    n_cols = k_refs[0].shape[2]
    pv = jnp.zeros((rows, dv), F32)
    for g in range(n_pages):
        pv = pv + _dot(p[:, g * n_cols:(g + 1) * n_cols].astype(BF16),
                       v_refs[g][0, 0].astype(BF16))
    acc_ref[...] = alpha * acc_ref[...] + pv

    @pl.when(j == pl.num_programs(1) - 1)
    def _():
        kn = per_map(kn_ref[0].astype(BF16).astype(F32))
        s_n = jnp.sum(q8.astype(F32) * kn, axis=-1, keepdims=True) + bias0_ref[...]
        m_o = m_ref[...]
        m_f = jnp.maximum(m_o, s_n)
        al = jnp.exp2(m_o - m_f)
        p_n = jnp.exp2(s_n - m_f)
        l_f = al * l_ref[...] + p_n
        vn = per_map(vn_ref[0].astype(BF16).astype(F32))
        z = (al * acc_ref[...] + p_n.astype(BF16).astype(F32) * vn) / l_f
        lam = _lam(lq1_ref, lk1_ref, lq2_ref, lk2_ref, lam_init)
        gate = gate_ref[0].astype(F32)
        for h in range(heads):
            sl = slice(h * dv, (h + 1) * dv)
            o_h = z[2 * h:2 * h + 1] - lam * z[2 * h + 1:2 * h + 2]
            o_h = _rms(o_h, nw_ref[...]) * (1.0 - lam_init)
            o_ref[0, :, sl] = (o_h * gate[:, sl]).astype(o_ref.dtype)


def _da_sample(a, cache_k, cache_v, page_table, layer, bias_rows, bias0, da_norm_l, lam_rows,
               lam_init, *, heads, dv):
    db = a["dq"].shape[0]
    width = heads * dv
    n_pages_seq = page_table.shape[1]
    page = cache_k.shape[2]
    g_n = PAGES_PER_STEP
    steps = n_pages_seq // g_n
    rows = 2 * heads
    row = pl.BlockSpec((1, 1, width), lambda i, j, pt: (i, 0, 0))
    vec = pl.BlockSpec((1, dv // 2), lambda i, j, pt: (0, 0))

    def page_spec(g):
        return pl.BlockSpec((1, 1, page * heads, dv),
                            lambda i, j, pt, g=g: (layer, pt[i, j * g_n + g], 0, 0))

    kern = functools.partial(_da_sample_kernel, heads=heads, dv=dv, n_pages=g_n, lam_init=lam_init)
    grid_spec = pltpu.PrefetchScalarGridSpec(
        num_scalar_prefetch=1,
        grid=(db, steps),
        in_specs=[row, row, row, row,
                  pl.BlockSpec((rows, g_n * page * heads), lambda i, j, pt: (0, j)),
                  pl.BlockSpec((rows, 1), lambda i, j, pt: (0, 0)),
                  pl.BlockSpec((1, dv), lambda i, j, pt: (0, 0)),
                  vec, vec, vec, vec]
                 + [page_spec(g) for g in range(g_n)] * 2,
        out_specs=row,
        scratch_shapes=[pltpu.VMEM((rows, 1), F32), pltpu.VMEM((rows, 1), F32),
                        pltpu.VMEM((rows, dv), F32)],
    )
    ck = cache_k.reshape(cache_k.shape[0], cache_k.shape[1], page * heads, dv)
    cv = cache_v.reshape(cache_v.shape[0], cache_v.shape[1], page * heads, dv)
    return pl.pallas_call(
        kern,
        grid_spec=grid_spec,
        out_shape=jax.ShapeDtypeStruct((db, 1, width), BF16),
        compiler_params=_cparams(("arbitrary", "arbitrary")),
        name="da_sample",
    )(page_table, a["dq"], a["dkf"], a["dvf"], a["dg"], bias_rows, bias0,
      da_norm_l.reshape(1, dv), *lam_rows, *([ck] * g_n), *([cv] * g_n))


def _out_kernel(x_ref, oh_ref, od_ref, *refs, heads, dh, hgw, daw, fused_mem):
    if fused_mem:
        mq_ref, mg_ref, mk_ref, mv_ref, w_ref, nw_ref, y_ref = refs
        mq = mq_ref[0]
        mg = mg_ref[0].astype(F32)
        parts = []
        for h in range(heads):
            sl = slice(h * dh, (h + 1) * dh)
            s = _dot_nt(mq[:, sl], mk_ref[0, :, sl].astype(BF16)) * (dh ** -0.5)
            p = jnp.exp(s - jnp.max(s, axis=-1, keepdims=True))
            o = _dot(p.astype(BF16), mv_ref[0, :, sl].astype(BF16))
            o = o / jnp.sum(p, axis=-1, keepdims=True)
            parts.append((o * mg[:, sl]).astype(BF16))
        om = jnp.concatenate(parts, axis=-1)
    else:
        om_ref, w_ref, nw_ref, y_ref = refs
        om = om_ref[0]
    y = _dot(oh_ref[0], w_ref[0, :hgw, :])
    y = y + _dot(od_ref[0], w_ref[0, hgw:hgw + daw, :])
    y = y + _dot(om, w_ref[0, hgw + daw:, :])
    y_ref[0] = x_ref[0] + _rms(y, nw_ref[0])


def _out_proj(x, oh, od, mem_args, layer, w_out_bf, norm_post, *, heads, dh, tm, fused_mem):
    b, t, d_model = x.shape
    depth, d_mix, _ = w_out_bf.shape
    hgw = oh.shape[-1]
    daw = od.shape[-1]
    mxw = heads * dh

    def act(width):
        return pl.BlockSpec((1, tm, width), lambda i, j: (i, j, 0))

    if fused_mem:
        mem = pl.BlockSpec((None, 1, mem_args[2].shape[2], mxw), lambda i, j: (layer, i, 0, 0))
        mem_specs = [act(mxw), act(mxw), mem, mem]
    else:
        mem_specs = [act(mxw)]
    return pl.pallas_call(
        functools.partial(_out_kernel, heads=heads, dh=dh, hgw=hgw, daw=daw, fused_mem=fused_mem),
        grid=(b, t // tm),
        out_shape=jax.ShapeDtypeStruct((b, t, d_model), F32),
        in_specs=[act(d_model), act(hgw), act(daw)] + mem_specs
                 + [pl.BlockSpec((1, d_mix, d_model), lambda i, j: (layer, 0, 0),
                                 pipeline_mode=pl.Buffered(1)),
                    pl.BlockSpec((1, 1, d_model), lambda i, j: (layer, 0, 0))],
        out_specs=act(d_model),
        compiler_params=_cparams(("arbitrary", "arbitrary")),
        name="out_proj" if fused_mem else "out_proj_rows",
    )(x, oh, od, *mem_args, w_out_bf, norm_post.reshape(depth, 1, d_model))


def _mem_sample_kernel(mq_ref, mg_ref, mk_ref, mv_ref, om_ref, *, heads, dh):
    pad = 8

    def one_sequence(n, carry):
        q = mq_ref[n]
        q_rows = jnp.concatenate([q[:, h * dh:(h + 1) * dh] for h in range(heads)]
                                 + [jnp.zeros((pad - heads, dh), BF16)], axis=0)
        s = _dot_nt(q_rows, mk_ref[n].astype(BF16)) * (dh ** -0.5)
        r_i = lax.broadcasted_iota(jnp.int32, s.shape, 0)
        c_i = lax.broadcasted_iota(jnp.int32, s.shape, 1)
        s = jnp.where(c_i % heads == r_i, s, NEG_BIG)
        p = jnp.exp(s - jnp.max(s, axis=-1, keepdims=True))
        o = _dot(p.astype(BF16), mv_ref[n].astype(BF16)) / jnp.sum(p, axis=-1, keepdims=True)
        mg = mg_ref[n].astype(F32)
        for h in range(heads):
            sl = slice(h * dh, (h + 1) * dh)
            om_ref[n, :, sl] = (o[h:h + 1] * mg[:, sl]).astype(om_ref.dtype)
        return carry

    lax.fori_loop(0, mq_ref.shape[0], one_sequence, 0)


def _mem_sample(a, cmk, cmv, layer, *, heads, dh):
    db = a["mq"].shape[0]
    width = heads * dh
    rows = cmk.shape[2]
    bs = SAMPLE_ROWS_PER_STEP
    row = pl.BlockSpec((bs, 1, width), lambda i: (i, 0, 0))
    mem = pl.BlockSpec((None, bs, rows, dh), lambda i: (layer, i, 0, 0))
    return pl.pallas_call(
        functools.partial(_mem_sample_kernel, heads=heads, dh=dh),
        grid=(db // bs,),
        out_shape=jax.ShapeDtypeStruct((db, 1, width), BF16),
        in_specs=[row, row, mem, mem],
        out_specs=row,
        compiler_params=_cparams(("arbitrary",)),
        name="mem_sample",
    )(a["mq"], a["mg"], cmk, cmv)


def kernel(x_prompt, x_sample, mem_prompt, cache_da_k, cache_da_v, cache_mem_k, cache_mem_v,
           state_hgrn, page_table, w_in, w_out, w_mem_kv, norm_pre, norm_post, mem_norm, hg_norm,
           da_norm, hg_lb, da_lq1, da_lk1, da_lq2, da_lk2, rel_bias):
    depth = w_in.shape[0]
    b, t, d_model = x_prompt.shape
    db = x_sample.shape[0]
    hg_heads, hg_dk = state_hgrn.shape[2], state_hgrn.shape[3]
    hgw = hg_heads * hg_dk
    da_heads, dv = cache_da_v.shape[3], cache_da_v.shape[4]
    daw = da_heads * dv
    mx_heads, mx_dh = cache_mem_k.shape[3], cache_mem_k.shape[4]
    mxw = mx_heads * mx_dh
    n_mem = mem_prompt.shape[1]
    page = cache_da_k.shape[2]
    past = page_table.shape[1] * page
    assert w_in.shape[2] == 4 * hgw + 4 * daw + 2 * mxw
    assert x_sample.shape[1] == 1 and MAX_DIST <= page <= DA_TK
    assert t % 1024 == 0 and page_table.shape[1] % PAGES_PER_STEP == 0
    assert db % SAMPLE_ROWS_PER_STEP == 0 and hg_heads % HG_HEADS_PER_STEP == 0

    w_in_bf = w_in.astype(BF16)
    w_out_bf = w_out.astype(BF16)
    w_mem_bf = w_mem_kv.astype(BF16)

    bias = _bias_tiles(rel_bias, da_heads)
    near = bias[:, 0, DA_TK - page:, 0]
    near_cols = jnp.repeat(jnp.repeat(near, 2, axis=0), da_heads, axis=1)
    bias_rows = jnp.concatenate(
        [jnp.zeros((2 * da_heads, (past - page) * da_heads), F32), near_cols], axis=1)
    col_head = jnp.arange(past * da_heads, dtype=jnp.int32) % da_heads
    row_head = jnp.arange(2 * da_heads, dtype=jnp.int32) // 2
    bias_rows = jnp.where(col_head[None, :] == row_head[:, None], bias_rows, NEG_BIG)
    bias0 = jnp.repeat(bias[:, 1, 0, 0], 2).reshape(2 * da_heads, 1)

    mk_p, mv_p = _mem_kv(mem_prompt, mem_norm, w_mem_bf)
    cmk = cache_mem_k.reshape(depth, db, n_mem * mx_heads, mx_dh)
    cmv = cache_mem_v.reshape(depth, db, n_mem * mx_heads, mx_dh)

    xp, xs = x_prompt, x_sample.reshape(1, db, d_model)
    outs = {n: [] for n in ("sp", "ks", "vs")}
    kv_all = ()
    s_s = ()
    dims = dict(hgw=hgw, daw=daw, mxw=mxw, da_heads=da_heads)
    for l in range(depth):
        lam_init = 0.8 - 0.6 * math.exp(-0.3 * l)
        lam_rows = [v[l].reshape(1, dv // 2) for v in (da_lq1, da_lk1, da_lq2, da_lk2)]

        a = _in_proj(xp, l, norm_pre, w_in_bf, hg_lb, kv_all, tm=512, emit_vt=True, **dims)
        kv_all = (a["dkf"], a["dvf"])
        oh, s_p = _hgrn_prompt(a, hg_norm[l], heads=hg_heads, dk=hg_dk, tt=512,
                               group=HG_GROUP, unroll=HG_UNROLL, hp=HG_HEADS_PER_STEP)
        od = _da_prompt(a, bias, da_norm[l], lam_rows, lam_init, heads=da_heads, dv=dv,
                        pairs_per_trip=DA_PAIRS_PER_TRIP)
        xp = _out_proj(xp, oh, od, (a["mq"], a["mg"], mk_p, mv_p), l, w_out_bf, norm_post,
                       heads=mx_heads, dh=mx_dh, tm=1024, fused_mem=True)
        outs["sp"].append(s_p)

        a = _in_proj(xs, l, norm_pre, w_in_bf, hg_lb, (), tm=db, emit_vt=False, **dims)
        a = {n: v.reshape(db, 1, v.shape[-1]) for n, v in a.items()}
        oh, s_s = _hgrn_step(a, hg_norm[l], state_hgrn, s_s, l, heads=hg_heads, dk=hg_dk)
        s_s = (s_s,)
        od = _da_sample(a, cache_da_k, cache_da_v, page_table, l, bias_rows, bias0, da_norm[l],
                        lam_rows, lam_init, heads=da_heads, dv=dv)
        om = _mem_sample(a, cmk, cmv, l, heads=mx_heads, dh=mx_dh)
        as_rows = lambda v: v.reshape(1, db, v.shape[-1])
        xs = _out_proj(xs, as_rows(oh), as_rows(od), (as_rows(om),), l, w_out_bf, norm_post,
                       heads=mx_heads, dh=mx_dh, tm=db, fused_mem=False)
        outs["ks"].append(a["dkf"].reshape(db, 1, da_heads, dv))
        outs["vs"].append(a["dvf"].reshape(db, 1, da_heads, dv))

    return (xp, xs.reshape(db, 1, d_model), kv_all[0], kv_all[1],
            jnp.stack(outs["sp"]), mk_p.reshape(depth, b, n_mem, mx_heads, mx_dh),
            mv_p.reshape(depth, b, n_mem, mx_heads, mx_dh), jnp.stack(outs["ks"]),
            jnp.stack(outs["vs"]), s_s[0])
```

```python
import functools
import math

import jax
import jax.numpy as jnp
from jax import lax
from jax.experimental import pallas as pl
from jax.experimental.pallas import tpu as pltpu

F32 = jnp.float32
BF16 = jnp.bfloat16

EPS = 1e-6
MAX_DIST = 128
HG_CHUNK = 64
HG_SUB = 8
HG_UNROLL = 8
HG_GROUP = 2
HG_HEADS_PER_STEP = 8
DA_HEADS_PER_STEP = 2
DA_PAIRS_PER_TRIP = 4
IN_ROW_SPLIT = 2
LOG2E = math.log2(math.e)
ONES_ROWS = 16
DA_TQ = 256
DA_TK = 256
PAGES_PER_STEP = 32
SAMPLE_ROWS_PER_STEP = 4
NEG_BIG = -1e30
VMEM_LIMIT_BYTES = 56 * 1024 * 1024


def _cparams(sem):
    return pltpu.CompilerParams(dimension_semantics=sem, vmem_limit_bytes=VMEM_LIMIT_BYTES)


def _sigmoid(x):
    return 1.0 / (1.0 + jnp.exp(-x))


def _silu(x):
    return x * _sigmoid(x)


def _rms(x, w):
    return x * lax.rsqrt(jnp.mean(x * x, axis=-1, keepdims=True) + EPS) * w


def _dot(a, b):
    return jnp.dot(a, b, preferred_element_type=F32)


def _dot_nt(a, b):
    return lax.dot_general(a, b, (((1,), (1,)), ((), ())), preferred_element_type=F32)


def _dot_tn(a, b):
    return lax.dot_general(a, b, (((0,), (0,)), ((), ())), preferred_element_type=F32)


def _lam(lq1_ref, lk1_ref, lq2_ref, lk2_ref, lam_init):
    a = jnp.sum(lq1_ref[...] * lk1_ref[...], axis=-1, keepdims=True)
    b = jnp.sum(lq2_ref[...] * lk2_ref[...], axis=-1, keepdims=True)
    return jnp.exp(a) - jnp.exp(b) + lam_init


def _bias_kernel(tab_ref, out_ref, *, n_buckets, heads, tk, tq):
    kk = lax.broadcasted_iota(jnp.int32, (tk, tq), 0)
    qq = lax.broadcasted_iota(jnp.int32, (tk, tq), 1)
    max_exact = n_buckets // 2
    for jj in range(2):
        d = qq - kk + (tk if jj == 0 else 0)
        n = jnp.maximum(d, 0)
        nf = jnp.maximum(n, 1).astype(F32)
        large = max_exact + (jnp.log(nf / max_exact) / math.log(MAX_DIST / max_exact)
                             * (n_buckets - max_exact)).astype(jnp.int32)
        large = jnp.minimum(large, n_buckets - 1)
        bucket = jnp.where(n < max_exact, n, large)
        for h in range(heads):
            val = jnp.zeros((tk, tq), F32)
            for b in range(n_buckets):
                val = jnp.where(bucket == b, tab_ref[b, h], val)
            val = (val - tab_ref[n_buckets - 1, h]) * LOG2E
            out_ref[h, jj] = jnp.where(d < 0, NEG_BIG, val)


def _bias_tiles(rel_bias, heads):
    n_buckets = rel_bias.shape[0]
    return pl.pallas_call(
        functools.partial(_bias_kernel, n_buckets=n_buckets, heads=heads, tk=DA_TK, tq=DA_TQ),
        out_shape=jax.ShapeDtypeStruct((heads, 2, DA_TK, DA_TQ), F32),
        in_specs=[pl.BlockSpec(memory_space=pltpu.SMEM)],
        out_specs=pl.BlockSpec(memory_space=pltpu.VMEM),
        name="t5_bias",
    )(rel_bias)


def _memkv_kernel(mem_ref, nw_ref, w_ref, k_ref, v_ref, *, width):
    xn = _rms(mem_ref[0], nw_ref[0]).astype(BF16)
    y = _dot(xn, w_ref[0])
    k_ref[0, 0] = y[:, :width]
    v_ref[0, 0] = y[:, width:]


def _mem_kv(mem_prompt, mem_norm, w_mem_kv_bf):
    depth, d_model, two_w = w_mem_kv_bf.shape
    width = two_w // 2
    b, n_mem, _ = mem_prompt.shape
    out = jax.ShapeDtypeStruct((depth, b, n_mem, width), F32)
    return pl.pallas_call(
        functools.partial(_memkv_kernel, width=width),
        grid=(depth, b),
        out_shape=(out, out),
        in_specs=[
            pl.BlockSpec((1, n_mem, d_model), lambda l, i: (i, 0, 0)),
            pl.BlockSpec((1, 1, d_model), lambda l, i: (l, 0, 0)),
            pl.BlockSpec((1, d_model, two_w), lambda l, i: (l, 0, 0)),
        ],
        out_specs=(pl.BlockSpec((1, 1, n_mem, width), lambda l, i: (l, i, 0, 0)),
                   pl.BlockSpec((1, 1, n_mem, width), lambda l, i: (l, i, 0, 0))),
        compiler_params=_cparams(("arbitrary", "arbitrary")),
        name="mem_kv",
    )(mem_prompt, mem_norm.reshape(depth, 1, d_model), w_mem_kv_bf)


def _inproj_kernel(x_ref, nw_ref, w_ref, lbp_ref, *refs, names, n_carried, layer, hgw, daw, mxw,
                   da_heads, tkb, emit_vt, col_chunk, row_split):
    r = dict(zip(names, refs[n_carried:]))

    p = lbp_ref[...]
    e = jnp.exp(p - jnp.max(p, axis=0, keepdims=True))
    sm = e / jnp.sum(e, axis=0, keepdims=True)
    lb = jnp.zeros((1, hgw), F32)
    for row in range(1, layer + 1):
        lb = lb + sm[row:row + 1, :]

    dv = daw // da_heads
    dqk = dv // 2

    def do_rows(r0, r1):
        rows = slice(r0, r1)
        xn = _rms(x_ref[0, rows], nw_ref[0]).astype(BF16)

        def seg(c0, width, fn):
            for c in range(0, width, col_chunk):
                w = min(col_chunk, width - c)
                fn(c, w, _dot(xn, w_ref[0, :, c0 + c:c0 + c + w]))

        def st(ref, fn):
            def go(c, w, y):
                ref[0, rows, c:c + w] = fn(y).astype(ref.dtype)
            return go

        def forget(c, w, y):
            f = lb[:, c:c + w] + (1.0 - lb[:, c:c + w]) * _sigmoid(y)
            r["lf"][0, rows, c:c + w] = jnp.log2(f)
            r["kh"][0, rows, c:c + w] = (1.0 - f).astype(BF16)

        def st_f32(ref, c, w, y):
            if emit_vt:
                for hh in range(c // dv, (c + w) // dv):
                    ref[0, 0, rows, hh, :] = y[:, hh * dv - c:(hh + 1) * dv - c]
            else:
                ref[0, rows, c:c + w] = y

        def da_k(c, w, y):
            st_f32(r["dkf"], c, w, y)
            r["dkb"][0, rows, c:c + w] = y.astype(BF16)

        def da_v(c, w, y):
            st_f32(r["dvf"], c, w, y)
            if emit_vt:
                piece = min(tkb, r1 - r0)
                for hh in range(c // dv, (c + w) // dv):
                    for q0 in range(r0, r1, piece):
                        blk = y[q0 - r0:q0 - r0 + piece, hh * dv - c:(hh + 1) * dv - c]
                        r["dvt"][0, hh, q0 // tkb, :, q0 % tkb:q0 % tkb + piece] = (
                            blk.T.astype(BF16))

        o = 0
        seg(o, hgw, st(r["qh"], _silu)); o += hgw
        seg(o, hgw, forget); o += hgw
        seg(o, hgw, st(r["vh"], lambda y: y)); o += hgw
        seg(o, hgw, st(r["gh"], _silu)); o += hgw
        seg(o, daw, st(r["dq"], lambda y: y * (dqk ** -0.5 * LOG2E))); o += daw
        seg(o, daw, da_k); o += daw
        seg(o, daw, da_v); o += daw
        seg(o, daw, st(r["dg"], _silu)); o += daw
        seg(o, mxw, st(r["mq"], lambda y: y)); o += mxw
        seg(o, mxw, st(r["mg"], _silu)); o += mxw

    tm = x_ref.shape[1]
    for part in range(row_split):
        do_rows(part * (tm // row_split), (part + 1) * (tm // row_split))


def _in_proj(x, layer, norm_pre, w_bf, hg_lb, carried, *, hgw, daw, mxw, da_heads, tm, emit_vt):
    b, t, d_model = x.shape
    depth = w_bf.shape[0]
    tkb = DA_TK
    dv = daw // da_heads

    def act(width, dtype):
        return (jax.ShapeDtypeStruct((b, t, width), dtype),
                pl.BlockSpec((1, tm, width), lambda i, j: (i, j, 0)))

    def rows_f32():
        if not emit_vt:
            return act(daw, F32)
        return (jax.ShapeDtypeStruct((depth, b, t, da_heads, dv), F32),
                pl.BlockSpec((1, 1, tm, da_heads, dv), lambda i, j: (layer, i, j, 0, 0)))

    outs = dict(qh=act(hgw, BF16), kh=act(hgw, BF16), vh=act(hgw, BF16), lf=act(hgw, F32),
                gh=act(hgw, BF16), dq=act(daw, BF16), dkf=rows_f32(), dkb=act(daw, BF16),
                dvf=rows_f32())
    if emit_vt:
        outs["dvt"] = (jax.ShapeDtypeStruct((b, da_heads, t // tkb, dv, tkb), BF16),
                       pl.BlockSpec((1, da_heads, tm // tkb, dv, tkb),
                                    lambda i, j: (i, 0, j, 0, 0)))
    outs.update(dg=act(daw, BF16), mq=act(mxw, BF16), mg=act(mxw, BF16))
    names = tuple(outs)
    in_specs = [
        pl.BlockSpec((1, tm, d_model), lambda i, j: (i, j, 0)),
        pl.BlockSpec((1, 1, d_model), lambda i, j: (layer, 0, 0)),
        pl.BlockSpec((1,) + w_bf.shape[1:], lambda i, j: (layer, 0, 0),
                     pipeline_mode=pl.Buffered(1)),
        pl.BlockSpec((depth, hgw), lambda i, j: (0, 0)),
    ] + [pl.BlockSpec(memory_space=pl.ANY)] * len(carried)
    aliases = {len(in_specs) - len(carried) + n: names.index(name)
               for n, name in enumerate(("dkf", "dvf")[:len(carried)])}
    kern = functools.partial(_inproj_kernel, names=names, n_carried=len(carried), layer=layer,
                             hgw=hgw, daw=daw, mxw=mxw, da_heads=da_heads, tkb=tkb,
                             emit_vt=emit_vt, col_chunk=512,
                             row_split=IN_ROW_SPLIT if emit_vt else 1)
    res = pl.pallas_call(
        kern,
        grid=(b, t // tm),
        out_shape=tuple(outs[n][0] for n in names),
        in_specs=in_specs,
        out_specs=tuple(outs[n][1] for n in names),
        input_output_aliases=aliases,
        compiler_params=_cparams(("arbitrary", "arbitrary")),
        name="in_proj_vt" if emit_vt else "in_proj",
    )(x, norm_pre.reshape(depth, 1, d_model), w_bf, hg_lb, *carried)
    return dict(zip(names, res))


def _hgrn_off_shape(chunk, sub):
    nsub = chunk // sub
    n_off_cols = sub * (nsub * (nsub - 1) // 2)
    return chunk - sub, n_off_cols, -(-n_off_cols // 128) * 128


def _hgrn_kernel(q_ref, k_ref, v_ref, g_ref, gate_ref, nw_ref, o_ref, s_ref, st_ref, tri_ref,
                 offm_ref, diagm_ref, cum_ref, qe_ref, dec_ref, kv_ref, oi_ref, *, chunk, sub,
                 n_chunks, unroll, group, hp, dk):
    j = pl.program_id(2)
    nsub = chunk // sub
    n_off_rows, n_off_cols, n_off_pad = _hgrn_off_shape(chunk, sub)

    @pl.when(j == 0)
    def _():
        st_ref[...] = jnp.zeros_like(st_ref)
        r_i = lax.broadcasted_iota(jnp.int32, (chunk, chunk), 0)
        c_i = lax.broadcasted_iota(jnp.int32, (chunk, chunk), 1)
        tri_ref[...] = jnp.where(c_i <= r_i, 1.0, 0.0).astype(BF16)

        ro = lax.broadcasted_iota(jnp.int32, (n_off_rows, n_off_pad), 0) // sub + 1
        co = lax.broadcasted_iota(jnp.int32, (n_off_rows, n_off_pad), 1)
        cblk = jnp.zeros_like(co)
        start = 0
        for i in range(1, nsub):
            cblk = jnp.where((co >= start) & (co < start + i * sub), i, cblk)
            start += i * sub
        offm_ref[...] = jnp.where(ro == cblk, 1.0, 0.0)

        rd = lax.broadcasted_iota(jnp.int32, (nsub * sub * sub, chunk), 0)
        cd = lax.broadcasted_iota(jnp.int32, (nsub * sub * sub, chunk), 1)
        r_is = rd // sub
        diagm_ref[...] = jnp.where((r_is == cd) & (rd % sub >= r_is % sub), 1.0, 0.0)

    g = jnp.concatenate([g_ref[0, c * chunk:(c + 1) * chunk, :] for c in range(n_chunks)], axis=1)
    g1 = g.astype(BF16)
    rem = g - g1.astype(F32)
    g2 = rem.astype(BF16)
    g3 = (rem - g2.astype(F32)).astype(BF16)
    tri = tri_ref[...]
    cum_all = _dot(tri, g1) + _dot(tri, g2) + _dot(tri, g3)
    for c in range(n_chunks * hp):
        cum_ref[c] = cum_all[:, c * dk:(c + 1) * dk]

    def stage_a(ci, hh):
        r0 = pl.multiple_of(ci * chunk, chunk)
        cols = slice(hh * dk, (hh + 1) * dk)
        item = ci * hp + hh
        q = q_ref[0, pl.ds(r0, chunk), cols].astype(F32)
        k_bf = k_ref[0, pl.ds(r0, chunk), cols]
        k = k_bf.astype(F32)
        v = v_ref[0, pl.ds(r0, chunk), cols]
        cum = cum_ref[item]

        last = cum[chunk - 1:chunk, :]
        qe_ref[item] = (q * jnp.exp2(cum)).astype(BF16)
        dec_ref[item] = jnp.exp2(last)
        kv = _dot_tn(v, (k * jnp.exp2(last - cum)).astype(BF16))

        q_parts, k_parts, v_parts = [], [], []
        v32 = v.astype(F32)
        for i in range(1, nsub):
            b_i = cum[i * sub - 1:i * sub, :]
            lo, hi = i * sub, (i + 1) * sub
            q_parts.append(q[lo:hi] * jnp.exp2(cum[lo:hi] - b_i))
            k_parts.append(k[:lo] * jnp.exp2(b_i - cum[:lo]))
            v_parts.append(v32[:lo])
        if n_off_pad > n_off_cols:
            k_parts.append(jnp.zeros((n_off_pad - n_off_cols, dk), F32))
            v_parts.append(jnp.zeros((n_off_pad - n_off_cols, v.shape[-1]), F32))
        a_off = _dot_nt(jnp.concatenate(q_parts, axis=0).astype(BF16),
                        jnp.concatenate(k_parts, axis=0).astype(BF16))
        v_off = jnp.concatenate(v_parts, axis=0).astype(BF16)

        n_parts = []
        for i in range(nsub):
            lo, hi = i * sub, (i + 1) * sub
            c_blk = cum[lo:hi]
            q_blk = q[lo:hi]
            for s in range(sub):
                dec = jnp.exp2(jnp.minimum(c_blk - c_blk[s:s + 1, :], 0.0))
                n_parts.append(q_blk * dec)
        r = _dot_nt(jnp.concatenate(n_parts, axis=0).astype(BF16), k_bf)
        return item, kv, a_off, v_off, r, v

    def stage_b(item, kv, a_off, v_off, r, v):
        kv_ref[item] = kv
        a_off = (a_off * offm_ref[...]).astype(BF16)
        o_off = _dot(a_off, v_off)
        r = r * diagm_ref[...]
        a_parts = []
        for i in range(nsub):
            base = i * sub * sub
            acc = r[base:base + sub]
            for s in range(1, sub):
                acc = acc + r[base + s * sub:base + (s + 1) * sub]
            a_parts.append(acc)
        a_diag = jnp.concatenate(a_parts, axis=0).astype(BF16)
        return item, o_off, _dot(a_diag, v)

    def stage_c(item, o_off, o_diag):
        oi_ref[item] = o_diag + jnp.concatenate([jnp.zeros((sub, dk), F32), o_off], axis=0)

    def intra(gi, carry):
        items = [(gi * group + u // hp, u % hp) for u in range(group * hp)]
        mid = [stage_a(ci, hh) for ci, hh in items]
        for x in [stage_b(*x) for x in mid]:
            stage_c(*x)
        return carry

    lax.fori_loop(0, n_chunks // group, intra, 0)

    def inter(ci, carry):
        r0 = pl.multiple_of(ci * chunk, chunk)
        for hh in range(hp):
            cols = slice(hh * dk, (hh + 1) * dk)
            item = ci * hp + hh
            s_prev = st_ref[hh]
            o = oi_ref[item] + _dot_nt(qe_ref[item], s_prev.astype(BF16))
            st_ref[hh] = s_prev * dec_ref[item] + kv_ref[item]
            gate = gate_ref[0, pl.ds(r0, chunk), cols].astype(F32)
            o_ref[0, pl.ds(r0, chunk), cols] = (_rms(o, nw_ref[...]) * gate).astype(o_ref.dtype)
        return carry

    lax.fori_loop(0, n_chunks, inter, 0, unroll=unroll)

    @pl.when(j == pl.num_programs(2) - 1)
    def _():
        for hh in range(hp):
            s_ref[0, hh] = st_ref[hh].T


def _hgrn_prompt(a, hg_norm_l, *, heads, dk, tt, group, unroll, hp):
    b, t, _ = a["qh"].shape
    spec = pl.BlockSpec((1, tt, hp * dk), lambda i, h, j: (i, j, h))
    chunk, sub = HG_CHUNK, HG_SUB
    n_chunks = tt // chunk
    n_off_rows, _, n_off_pad = _hgrn_off_shape(chunk, sub)
    kern = functools.partial(_hgrn_kernel, chunk=chunk, sub=sub, n_chunks=n_chunks,
                             unroll=unroll, group=group, hp=hp, dk=dk)
    n_items = n_chunks * hp
    return pl.pallas_call(
        kern,
        grid=(b, heads // hp, t // tt),
        out_shape=(jax.ShapeDtypeStruct((b, t, heads * dk), BF16),
                   jax.ShapeDtypeStruct((b, heads, dk, dk), F32)),
        in_specs=[spec, spec, spec, spec, spec, pl.BlockSpec((1, dk), lambda i, h, j: (0, 0))],
        out_specs=(spec, pl.BlockSpec((1, hp, dk, dk), lambda i, h, j: (i, h, 0, 0))),
        scratch_shapes=[pltpu.VMEM((hp, dk, dk), F32), pltpu.VMEM((chunk, chunk), BF16),
                        pltpu.VMEM((n_off_rows, n_off_pad), F32),
                        pltpu.VMEM((chunk * sub, chunk), F32),
                        pltpu.VMEM((n_items, chunk, dk), F32),
                        pltpu.VMEM((n_items, chunk, dk), BF16),
                        pltpu.VMEM((n_items, 1, dk), F32),
                        pltpu.VMEM((n_items, dk, dk), F32),
                        pltpu.VMEM((n_items, chunk, dk), F32)],
        compiler_params=_cparams(("arbitrary", "arbitrary", "arbitrary")),
        name="hgrn_prompt",
    )(a["qh"], a["kh"], a["vh"], a["lf"], a["gh"], hg_norm_l.reshape(1, dk))


def _hgrn_step_kernel(q_ref, lf_ref, v_ref, gate_ref, nw_ref, s_ref, *refs, heads, dk):
    o_ref, sn_ref = refs[-2:]

    def col(row):
        return jnp.broadcast_to(row, (dk, dk)).T

    def one_sequence(s, carry):
        for h in range(heads):
            sl = slice(h * dk, (h + 1) * dk)
            f = jnp.exp2(lf_ref[s, :, sl])
            s_new = col(f) * s_ref[s, h] + col(1.0 - f) * v_ref[s, :, sl].astype(F32)
            sn_ref[0, s, h] = s_new
            o = jnp.sum(col(q_ref[s, :, sl].astype(F32)) * s_new, axis=0, keepdims=True)
            o = _rms(o, nw_ref[...]) * gate_ref[s, :, sl].astype(F32)
            o_ref[s, :, sl] = o.astype(o_ref.dtype)
        return carry

    lax.fori_loop(0, q_ref.shape[0], one_sequence, 0)


def _hgrn_step(a, hg_norm_l, state_hgrn, carried, layer, *, heads, dk):
    db = a["qh"].shape[0]
    w = heads * dk
    depth = state_hgrn.shape[0]
    bs = SAMPLE_ROWS_PER_STEP
    row = pl.BlockSpec((bs, 1, w), lambda i: (i, 0, 0))
    state = pl.BlockSpec((1, bs, heads, dk, dk), lambda i: (layer, i, 0, 0, 0))
    in_specs = [row, row, row, row, pl.BlockSpec((1, dk), lambda i: (0, 0)),
                pl.BlockSpec((None, bs, heads, dk, dk), lambda i: (layer, i, 0, 0, 0))]
    in_specs += [pl.BlockSpec(memory_space=pl.ANY)] * len(carried)
    return pl.pallas_call(
        functools.partial(_hgrn_step_kernel, heads=heads, dk=dk),
        grid=(db // bs,),
        out_shape=(jax.ShapeDtypeStruct((db, 1, w), BF16),
                   jax.ShapeDtypeStruct((depth, db, heads, dk, dk), F32)),
        in_specs=in_specs,
        out_specs=(row, state),
        input_output_aliases={len(in_specs) - 1: 1} if carried else {},
        compiler_params=_cparams(("arbitrary",)),
        name="hgrn_step",
    )(a["qh"], a["lf"], a["vh"], a["gh"], hg_norm_l.reshape(1, dk), state_hgrn, *carried)


def _da_prompt_kernel(q_ref, k_ref, vt_ref, bias_ref, gate_ref, nw_ref, lq1_ref, lk1_ref,
                      lq2_ref, lk2_ref, o_ref, acc_ref, s0_ref, s1_ref, p0_ref, p1_ref, al0_ref,
                      al1_ref, mb0_ref, mb1_ref, *, tq, tk, lam_init, hp, dv, pairs_per_trip):
    mb_refs = (mb0_ref, mb1_ref)
    i = pl.program_id(2)
    half = dv // 2
    row = lax.broadcasted_iota(jnp.int32, (dv, tq), 0)

    def qbd_of(hh):
        qt = q_ref[0, :, hh * dv:(hh + 1) * dv].astype(F32).T
        return jnp.concatenate([jnp.where(row < half, qt, 0.0), jnp.where(row >= half, qt, 0.0)],
                               axis=1).astype(BF16)

    qbd = [qbd_of(hh) for hh in range(hp)]
    s_refs, p_refs, al_refs = (s0_ref, s1_ref), (p0_ref, p1_ref), (al0_ref, al1_ref)
    acc_ref[...] = jnp.zeros_like(acc_ref)

    def scores(slot, blk):
        for hh in range(hp):
            k_blk = k_ref[0, pl.ds(pl.multiple_of(blk * tk, tk), tk), hh * dv:(hh + 1) * dv]
            s = _dot(k_blk, qbd[hh])
            s_refs[slot][hh] = s
            mb_refs[slot][hh] = jnp.max(s, axis=0, keepdims=True)

    def probs(slot, ms, bias_idx):
        out = []
        for hh in range(hp):
            s = s_refs[slot][hh]
            if bias_idx is None:
                m_blk = mb_refs[slot][hh]
            else:
                bias = bias_ref[hh, bias_idx]
                s = s + jnp.concatenate([bias, bias], axis=1)
                m_blk = jnp.max(s, axis=0, keepdims=True)
            m_new = jnp.maximum(ms[hh], m_blk)
            al_refs[slot][hh] = jnp.exp2(ms[hh] - m_new)
            p_refs[slot][hh] = jnp.exp2(s - m_new).astype(BF16)
            out.append(m_new)
        return tuple(out)

    def accumulate(slot, blk):
        kb = jnp.maximum(blk, 0)
        for hh in range(hp):
            vt = jnp.concatenate([vt_ref[0, hh, kb], jnp.ones((ONES_ROWS, tk), BF16)], axis=0)
            acc_ref[hh] = al_refs[slot][hh] * acc_ref[hh] + _dot(vt, p_refs[slot][hh])

    m = tuple(jnp.full((1, 2 * tq), 0.1 * NEG_BIG, F32) for _ in range(hp))

    def far_pair(e, ms):
        accumulate(1, e - 1)
        ms = probs(0, ms, None)
        scores(1, e + 1)
        accumulate(0, e)
        ms = probs(1, ms, None)
        scores(0, e + 2)
        return ms

    def far_pairs(u, ms):
        for v in range(pairs_per_trip):
            ms = far_pair(2 * (pairs_per_trip * u + v), ms)
        return ms

    def tail_odd(ms):
        accumulate(1, i - 2)
        ms = probs(0, ms, 0)
        scores(1, i)
        accumulate(0, i - 1)
        ms = probs(1, ms, 1)
        accumulate(1, i)
        return ms

    def tail_even(ms):
        accumulate(1, i - 3)
        ms = probs(0, ms, None)
        scores(1, i - 1)
        accumulate(0, i - 2)
        ms = probs(1, ms, 0)
        scores(0, i)
        accumulate(1, i - 1)
        ms = probs(0, ms, 1)
        accumulate(0, i)
        return ms

    def tail_zero(ms):
        ms = probs(0, ms, 1)
        accumulate(0, i)
        return ms

    p1_ref[...] = jnp.zeros_like(p1_ref)
    al1_ref[...] = jnp.ones_like(al1_ref)
    scores(0, 0)
    n_pairs = jnp.right_shift(jnp.maximum(i - 1, 0), 1)
    n_trips = n_pairs // pairs_per_trip
    m = lax.fori_loop(0, n_trips, far_pairs, m)
    if pairs_per_trip > 1:
        m = lax.fori_loop(n_trips * pairs_per_trip, n_pairs, lambda u, ms: far_pair(2 * u, ms), m)
    lax.cond(i == 0, tail_zero,
             lambda ms: lax.cond((i & 1) == 1, tail_odd, tail_even, ms), m)

    lam = _lam(lq1_ref, lk1_ref, lq2_ref, lk2_ref, lam_init)
    for hh in range(hp):
        acc = acc_ref[hh, :dv, :]
        inv = 1.0 / acc_ref[hh, dv:dv + 1, :]
        out_t = acc[:, :tq] * inv[:, :tq] - lam * (acc[:, tq:] * inv[:, tq:])
        out = _rms(out_t.T, nw_ref[...]) * (1.0 - lam_init)
        cols = slice(hh * dv, (hh + 1) * dv)
        o_ref[0, :, cols] = (out * gate_ref[0, :, cols].astype(F32)).astype(o_ref.dtype)


def _da_prompt(a, bias, da_norm_l, lam_rows, lam_init, *, heads, dv, pairs_per_trip):
    b, t, _ = a["dq"].shape
    tq, tk = DA_TQ, DA_TK
    hp = DA_HEADS_PER_STEP
    qspec = pl.BlockSpec((1, tq, hp * dv), lambda i, h, j: (i, j, h))
    vec = pl.BlockSpec((1, dv // 2), lambda i, h, j: (0, 0))
    kern = functools.partial(_da_prompt_kernel, tq=tq, tk=tk, lam_init=lam_init, hp=hp, dv=dv,
                             pairs_per_trip=pairs_per_trip)
    return pl.pallas_call(
        kern,
        grid=(b, heads // hp, t // tq),
        out_shape=jax.ShapeDtypeStruct((b, t, heads * dv), BF16),
        in_specs=[
            qspec,
            pl.BlockSpec((1, t, hp * dv), lambda i, h, j: (i, 0, h)),
            pl.BlockSpec((1, hp, t // tk, dv, tk), lambda i, h, j: (i, h, 0, 0, 0)),
            pl.BlockSpec((hp, 2, tk, tq), lambda i, h, j: (h, 0, 0, 0)),
            qspec,
            pl.BlockSpec((1, dv), lambda i, h, j: (0, 0)),
            vec, vec, vec, vec,
        ],
        out_specs=qspec,
        scratch_shapes=[pltpu.VMEM((hp, dv + ONES_ROWS, 2 * tq), F32),
                        pltpu.VMEM((hp, tk, 2 * tq), F32), pltpu.VMEM((hp, tk, 2 * tq), F32),
                        pltpu.VMEM((hp, tk, 2 * tq), BF16), pltpu.VMEM((hp, tk, 2 * tq), BF16),
                        pltpu.VMEM((hp, 1, 2 * tq), F32), pltpu.VMEM((hp, 1, 2 * tq), F32),
                        pltpu.VMEM((hp, 1, 2 * tq), F32), pltpu.VMEM((hp, 1, 2 * tq), F32)],
        compiler_params=_cparams(("arbitrary", "arbitrary", "arbitrary")),
        name="da_prompt",
    )(a["dq"], a["dkb"], a["dvt"], bias, a["dg"], da_norm_l.reshape(1, dv), *lam_rows)


def _da_sample_kernel(pt_ref, q_ref, kn_ref, vn_ref, gate_ref, bias_ref, bias0_ref, nw_ref,
                      lq1_ref, lk1_ref, lq2_ref, lk2_ref, *refs, heads, dv, n_pages, lam_init):
    del pt_ref
    k_refs = refs[:n_pages]
    v_refs = refs[n_pages:2 * n_pages]
    o_ref, m_ref, l_ref, acc_ref = refs[2 * n_pages:]
    j = pl.program_id(1)
    half = dv // 2
    rows = 2 * heads

    @pl.when(j == 0)
    def _():
        m_ref[...] = jnp.full_like(m_ref, 0.1 * NEG_BIG)
        l_ref[...] = jnp.zeros_like(l_ref)
        acc_ref[...] = jnp.zeros_like(acc_ref)

    def per_map(x):
        return jnp.concatenate([x[:, h * dv:(h + 1) * dv] for h in range(heads) for _ in range(2)],
                               axis=0)

    r_i = lax.broadcasted_iota(jnp.int32, (rows, dv), 0)
    c_i = lax.broadcasted_iota(jnp.int32, (rows, dv), 1)
    q8 = jnp.where(c_i // half == r_i % 2, per_map(q_ref[0].astype(F32)), 0.0).astype(BF16)

    s = jnp.concatenate([_dot_nt(q8, k_refs[g][0, 0].astype(BF16)) for g in range(n_pages)],
                        axis=1)
    s = s + bias_ref[...]
    m_old = m_ref[...]
    m_new = jnp.maximum(m_old, jnp.max(s, axis=-1, keepdims=True))
    alpha = jnp.exp2(m_old - m_new)
    p = jnp.exp2(s - m_new)
    l_ref[...] = alpha * l_ref[...] + jnp.sum(p, axis=-1, keepdims=True)
    m_ref[...] = m_new
    n_cols = k_refs[0].shape[2]
    pv = jnp.zeros((rows, dv), F32)
    for g in range(n_pages):
        pv = pv + _dot(p[:, g * n_cols:(g + 1) * n_cols].astype(BF16),
                       v_refs[g][0, 0].astype(BF16))
    acc_ref[...] = alpha * acc_ref[...] + pv

    @pl.when(j == pl.num_programs(1) - 1)
    def _():
        kn = per_map(kn_ref[0].astype(BF16).astype(F32))
        s_n = jnp.sum(q8.astype(F32) * kn, axis=-1, keepdims=True) + bias0_ref[...]
        m_o = m_ref[...]
        m_f = jnp.maximum(m_o, s_n)
        al = jnp.exp2(m_o - m_f)
        p_n = jnp.exp2(s_n - m_f)
        l_f = al * l_ref[...] + p_n
        vn = per_map(vn_ref[0].astype(BF16).astype(F32))
        z = (al * acc_ref[...] + p_n.astype(BF16).astype(F32) * vn) / l_f
        lam = _lam(lq1_ref, lk1_ref, lq2_ref, lk2_ref, lam_init)
        gate = gate_ref[0].astype(F32)
        for h in range(heads):
            sl = slice(h * dv, (h + 1) * dv)
            o_h = z[2 * h:2 * h + 1] - lam * z[2 * h + 1:2 * h + 2]
            o_h = _rms(o_h, nw_ref[...]) * (1.0 - lam_init)
            o_ref[0, :, sl] = (o_h * gate[:, sl]).astype(o_ref.dtype)


def _da_sample(a, cache_k, cache_v, page_table, layer, bias_rows, bias0, da_norm_l, lam_rows,
               lam_init, *, heads, dv):
    db = a["dq"].shape[0]
    width = heads * dv
    n_pages_seq = page_table.shape[1]
    page = cache_k.shape[2]
    g_n = PAGES_PER_STEP
    steps = n_pages_seq // g_n
    rows = 2 * heads
    row = pl.BlockSpec((1, 1, width), lambda i, j, pt: (i, 0, 0))
    vec = pl.BlockSpec((1, dv // 2), lambda i, j, pt: (0, 0))

    def page_spec(g):
        return pl.BlockSpec((1, 1, page * heads, dv),
                            lambda i, j, pt, g=g: (layer, pt[i, j * g_n + g], 0, 0))

    kern = functools.partial(_da_sample_kernel, heads=heads, dv=dv, n_pages=g_n, lam_init=lam_init)
    grid_spec = pltpu.PrefetchScalarGridSpec(
        num_scalar_prefetch=1,
        grid=(db, steps),
        in_specs=[row, row, row, row,
                  pl.BlockSpec((rows, g_n * page * heads), lambda i, j, pt: (0, j)),
                  pl.BlockSpec((rows, 1), lambda i, j, pt: (0, 0)),
                  pl.BlockSpec((1, dv), lambda i, j, pt: (0, 0)),
                  vec, vec, vec, vec]
                 + [page_spec(g) for g in range(g_n)] * 2,
        out_specs=row,
        scratch_shapes=[pltpu.VMEM((rows, 1), F32), pltpu.VMEM((rows, 1), F32),
                        pltpu.VMEM((rows, dv), F32)],
    )
    ck = cache_k.reshape(cache_k.shape[0], cache_k.shape[1], page * heads, dv)
    cv = cache_v.reshape(cache_v.shape[0], cache_v.shape[1], page * heads, dv)
    return pl.pallas_call(
        kern,
        grid_spec=grid_spec,
        out_shape=jax.ShapeDtypeStruct((db, 1, width), BF16),
        compiler_params=_cparams(("arbitrary", "arbitrary")),
        name="da_sample",
    )(page_table, a["dq"], a["dkf"], a["dvf"], a["dg"], bias_rows, bias0,
      da_norm_l.reshape(1, dv), *lam_rows, *([ck] * g_n), *([cv] * g_n))


def _out_kernel(x_ref, oh_ref, od_ref, *refs, heads, dh, hgw, daw, fused_mem):
    if fused_mem:
        mq_ref, mg_ref, mk_ref, mv_ref, w_ref, nw_ref, y_ref = refs
        mq = mq_ref[0]
        mg = mg_ref[0].astype(F32)
        parts = []
        for h in range(heads):
            sl = slice(h * dh, (h + 1) * dh)
            s = _dot_nt(mq[:, sl], mk_ref[0, :, sl].astype(BF16)) * (dh ** -0.5)
            p = jnp.exp(s - jnp.max(s, axis=-1, keepdims=True))
            o = _dot(p.astype(BF16), mv_ref[0, :, sl].astype(BF16))
            o = o / jnp.sum(p, axis=-1, keepdims=True)
            parts.append((o * mg[:, sl]).astype(BF16))
        om = jnp.concatenate(parts, axis=-1)
    else:
        om_ref, w_ref, nw_ref, y_ref = refs
        om = om_ref[0]
    y = _dot(oh_ref[0], w_ref[0, :hgw, :])
    y = y + _dot(od_ref[0], w_ref[0, hgw:hgw + daw, :])
    y = y + _dot(om, w_ref[0, hgw + daw:, :])
    y_ref[0] = x_ref[0] + _rms(y, nw_ref[0])


def _out_proj(x, oh, od, mem_args, layer, w_out_bf, norm_post, *, heads, dh, tm, fused_mem):
    b, t, d_model = x.shape
    depth, d_mix, _ = w_out_bf.shape
    hgw = oh.shape[-1]
    daw = od.shape[-1]
    mxw = heads * dh

    def act(width):
        return pl.BlockSpec((1, tm, width), lambda i, j: (i, j, 0))

    if fused_mem:
        mem = pl.BlockSpec((None, 1, mem_args[2].shape[2], mxw), lambda i, j: (layer, i, 0, 0))
        mem_specs = [act(mxw), act(mxw), mem, mem]
    else:
        mem_specs = [act(mxw)]
    return pl.pallas_call(
        functools.partial(_out_kernel, heads=heads, dh=dh, hgw=hgw, daw=daw, fused_mem=fused_mem),
        grid=(b, t // tm),
        out_shape=jax.ShapeDtypeStruct((b, t, d_model), F32),
        in_specs=[act(d_model), act(hgw), act(daw)] + mem_specs
                 + [pl.BlockSpec((1, d_mix, d_model), lambda i, j: (layer, 0, 0),
                                 pipeline_mode=pl.Buffered(1)),
                    pl.BlockSpec((1, 1, d_model), lambda i, j: (layer, 0, 0))],
        out_specs=act(d_model),
        compiler_params=_cparams(("arbitrary", "arbitrary")),
        name="out_proj" if fused_mem else "out_proj_rows",
    )(x, oh, od, *mem_args, w_out_bf, norm_post.reshape(depth, 1, d_model))


def _mem_sample_kernel(mq_ref, mg_ref, mk_ref, mv_ref, om_ref, *, heads, dh):
    pad = 8

    def one_sequence(n, carry):
        q = mq_ref[n]
        q_rows = jnp.concatenate([q[:, h * dh:(h + 1) * dh] for h in range(heads)]
                                 + [jnp.zeros((pad - heads, dh), BF16)], axis=0)
        s = _dot_nt(q_rows, mk_ref[n].astype(BF16)) * (dh ** -0.5)
        r_i = lax.broadcasted_iota(jnp.int32, s.shape, 0)
        c_i = lax.broadcasted_iota(jnp.int32, s.shape, 1)
        s = jnp.where(c_i % heads == r_i, s, NEG_BIG)
        p = jnp.exp(s - jnp.max(s, axis=-1, keepdims=True))
        o = _dot(p.astype(BF16), mv_ref[n].astype(BF16)) / jnp.sum(p, axis=-1, keepdims=True)
        mg = mg_ref[n].astype(F32)
        for h in range(heads):
            sl = slice(h * dh, (h + 1) * dh)
            om_ref[n, :, sl] = (o[h:h + 1] * mg[:, sl]).astype(om_ref.dtype)
        return carry

    lax.fori_loop(0, mq_ref.shape[0], one_sequence, 0)


def _mem_sample(a, cmk, cmv, layer, *, heads, dh):
    db = a["mq"].shape[0]
    width = heads * dh
    rows = cmk.shape[2]
    bs = SAMPLE_ROWS_PER_STEP
    row = pl.BlockSpec((bs, 1, width), lambda i: (i, 0, 0))
    mem = pl.BlockSpec((None, bs, rows, dh), lambda i: (layer, i, 0, 0))
    return pl.pallas_call(
        functools.partial(_mem_sample_kernel, heads=heads, dh=dh),
        grid=(db // bs,),
        out_shape=jax.ShapeDtypeStruct((db, 1, width), BF16),
        in_specs=[row, row, mem, mem],
        out_specs=row,
        compiler_params=_cparams(("arbitrary",)),
        name="mem_sample",
    )(a["mq"], a["mg"], cmk, cmv)


def kernel(x_prompt, x_sample, mem_prompt, cache_da_k, cache_da_v, cache_mem_k, cache_mem_v,
           state_hgrn, page_table, w_in, w_out, w_mem_kv, norm_pre, norm_post, mem_norm, hg_norm,
           da_norm, hg_lb, da_lq1, da_lk1, da_lq2, da_lk2, rel_bias):
    depth = w_in.shape[0]
    b, t, d_model = x_prompt.shape
    db = x_sample.shape[0]
    hg_heads, hg_dk = state_hgrn.shape[2], state_hgrn.shape[3]
    hgw = hg_heads * hg_dk
    da_heads, dv = cache_da_v.shape[3], cache_da_v.shape[4]
    daw = da_heads * dv
    mx_heads, mx_dh = cache_mem_k.shape[3], cache_mem_k.shape[4]
    mxw = mx_heads * mx_dh
    n_mem = mem_prompt.shape[1]
    page = cache_da_k.shape[2]
    past = page_table.shape[1] * page
    assert w_in.shape[2] == 4 * hgw + 4 * daw + 2 * mxw
    assert x_sample.shape[1] == 1 and MAX_DIST <= page <= DA_TK
    assert t % 1024 == 0 and page_table.shape[1] % PAGES_PER_STEP == 0
    assert db % SAMPLE_ROWS_PER_STEP == 0 and hg_heads % HG_HEADS_PER_STEP == 0

    w_in_bf = w_in.astype(BF16)
    w_out_bf = w_out.astype(BF16)
    w_mem_bf = w_mem_kv.astype(BF16)

    bias = _bias_tiles(rel_bias, da_heads)
    near = bias[:, 0, DA_TK - page:, 0]
    near_cols = jnp.repeat(jnp.repeat(near, 2, axis=0), da_heads, axis=1)
    bias_rows = jnp.concatenate(
        [jnp.zeros((2 * da_heads, (past - page) * da_heads), F32), near_cols], axis=1)
    col_head = jnp.arange(past * da_heads, dtype=jnp.int32) % da_heads
    row_head = jnp.arange(2 * da_heads, dtype=jnp.int32) // 2
    bias_rows = jnp.where(col_head[None, :] == row_head[:, None], bias_rows, NEG_BIG)
    bias0 = jnp.repeat(bias[:, 1, 0, 0], 2).reshape(2 * da_heads, 1)

    mk_p, mv_p = _mem_kv(mem_prompt, mem_norm, w_mem_bf)
    cmk = cache_mem_k.reshape(depth, db, n_mem * mx_heads, mx_dh)
    cmv = cache_mem_v.reshape(depth, db, n_mem * mx_heads, mx_dh)

    xp, xs = x_prompt, x_sample.reshape(1, db, d_model)
    outs = {n: [] for n in ("sp", "ks", "vs")}
    kv_all = ()
    s_s = ()
    dims = dict(hgw=hgw, daw=daw, mxw=mxw, da_heads=da_heads)
    for l in range(depth):
        lam_init = 0.8 - 0.6 * math.exp(-0.3 * l)
        lam_rows = [v[l].reshape(1, dv // 2) for v in (da_lq1, da_lk1, da_lq2, da_lk2)]

        a = _in_proj(xp, l, norm_pre, w_in_bf, hg_lb, kv_all, tm=512, emit_vt=True, **dims)
        kv_all = (a["dkf"], a["dvf"])
        oh, s_p = _hgrn_prompt(a, hg_norm[l], heads=hg_heads, dk=hg_dk, tt=512,
                               group=HG_GROUP, unroll=HG_UNROLL, hp=HG_HEADS_PER_STEP)
        od = _da_prompt(a, bias, da_norm[l], lam_rows, lam_init, heads=da_heads, dv=dv,
                        pairs_per_trip=DA_PAIRS_PER_TRIP)
        xp = _out_proj(xp, oh, od, (a["mq"], a["mg"], mk_p, mv_p), l, w_out_bf, norm_post,
                       heads=mx_heads, dh=mx_dh, tm=1024, fused_mem=True)
        outs["sp"].append(s_p)

        a = _in_proj(xs, l, norm_pre, w_in_bf, hg_lb, (), tm=db, emit_vt=False, **dims)
        a = {n: v.reshape(db, 1, v.shape[-1]) for n, v in a.items()}
        oh, s_s = _hgrn_step(a, hg_norm[l], state_hgrn, s_s, l, heads=hg_heads, dk=hg_dk)
        s_s = (s_s,)
        od = _da_sample(a, cache_da_k, cache_da_v, page_table, l, bias_rows, bias0, da_norm[l],
                        lam_rows, lam_init, heads=da_heads, dv=dv)
        om = _mem_sample(a, cmk, cmv, l, heads=mx_heads, dh=mx_dh)
        as_rows = lambda v: v.reshape(1, db, v.shape[-1])
        xs = _out_proj(xs, as_rows(oh), as_rows(od), (as_rows(om),), l, w_out_bf, norm_post,
                       heads=mx_heads, dh=mx_dh, tm=db, fused_mem=False)
        outs["ks"].append(a["dkf"].reshape(db, 1, da_heads, dv))
        outs["vs"].append(a["dvf"].reshape(db, 1, da_heads, dv))

    return (xp, xs.reshape(db, 1, d_model), kv_all[0], kv_all[1],
            jnp.stack(outs["sp"]), mk_p.reshape(depth, b, n_mem, mx_heads, mx_dh),
            mv_p.reshape(depth, b, n_mem, mx_heads, mx_dh), jnp.stack(outs["ks"]),
            jnp.stack(outs["vs"]), s_s[0])
```
